```python
import math
import jax, jax.numpy as jnp
from jax import lax
import numpy as np

D_MODEL = 1024
BATCH = 4
SEQ = 4096
DEPTH = 4
DEC_BATCH = 32
DEC_SEQ = 1
PAST_LEN = 8192
PAGE_SIZE = 128

N_A_LAYERS = DEPTH // 2
N_B_LAYERS = DEPTH - N_A_LAYERS
POOL_WINDOWS = (2, 4, 8, 16)
N_POOL_GROUPS = len(POOL_WINDOWS)
POOL_GROUP = D_MODEL // N_POOL_GROUPS
POOL_STATE = max(POOL_WINDOWS) - 1
BRANCHES = ((128, 1), (512, 4), (2048, 16))
N_BRANCH = len(BRANCHES)
HEADS = 8
HEAD_DIM = D_MODEL // HEADS
ATT_WIDTH = HEADS * HEAD_DIM
D_FF = 4 * D_MODEL
NUM_BUCKETS = 32
MAX_DISTANCE = 2048
Q_BLOCK = 128
EPS = 1e-6

kernel_name = "yoco_pool_dilated_attn_step"


def rmsnorm(x, g):
    xf = x.astype(jnp.float32)
    y = xf * lax.rsqrt(jnp.mean(xf * xf, axis=-1, keepdims=True) + EPS)
    return (y * g.astype(jnp.float32)).astype(x.dtype)


def t5_bucket(dist):
    max_exact = NUM_BUCKETS // 2
    df = jnp.maximum(dist, 1).astype(jnp.float32)
    large = max_exact + (jnp.log(df / max_exact) / math.log(MAX_DISTANCE / max_exact)
                         * (NUM_BUCKETS - max_exact)).astype(jnp.int32)
    large = jnp.minimum(large, NUM_BUCKETS - 1)
    return jnp.where(dist < max_exact, dist, large)


def branch_biases(rel_bias):
    out = []
    for g, (w, d) in enumerate(BRANCHES):
        dist = jnp.arange(w // d + 1, dtype=jnp.int32) * d
        out.append(rel_bias[t5_bucket(dist)][:, g * HEADS:(g + 1) * HEADS])
    return out


def pool_mix(h, buf, pos0, w_pool, scale):
    B, T, D = h.shape
    xp = jnp.concatenate([buf.astype(h.dtype), h], axis=1).astype(jnp.float32)
    cs = jnp.concatenate([jnp.zeros_like(xp[:, :1]), jnp.cumsum(xp, axis=1)], axis=1)
    pos = pos0 + jnp.arange(T)
    hi = cs[:, POOL_STATE + 1:]
    parts = []
    for g, w in enumerate(POOL_WINDOWS):
        sl = slice(g * POOL_GROUP, (g + 1) * POOL_GROUP)
        lo = cs[:, POOL_STATE + 1 - w:POOL_STATE + 1 - w + T, sl]
        cnt = jnp.minimum(pos + 1, w).astype(jnp.float32)[None, :, None]
        parts.append((hi[..., sl] - lo) / cnt - xp[:, POOL_STATE:, sl])
    pooled = jnp.stack(parts, axis=2).astype(h.dtype)
    y = jnp.einsum('btgc,gce->btge', pooled, w_pool).reshape(B, T, D) * scale
    new_buf = xp[:, -POOL_STATE:].astype(h.dtype)
    return y, new_buf


def branch_attend(q, k_ctx, v_ctx, q_idx, dil, n_keys, bias):
    idx = q_idx[:, None] - dil * jnp.arange(n_keys)[None, :]
    valid = idx >= 0
    idx = jnp.maximum(idx, 0)
    k_g = k_ctx[:, idx]
    v_g = v_ctx[:, idx]
    s = jnp.einsum('bqhd,bqkhd->bqhk', q, k_g).astype(jnp.float32) * (HEAD_DIM ** -0.5)
    s = s + bias.T.astype(jnp.float32)[None, None]
    s = jnp.where(valid[None, :, None, :], s, -jnp.inf)
    lse = jax.nn.logsumexp(s, axis=-1)
    p = jnp.exp(s - lse[..., None])
    o = jnp.einsum('bqhk,bqkhd->bqhd', p.astype(v_g.dtype), v_g)
    return o, lse


def dilated_mix(q3, ctxs, q_idxs, biases):
    outs, lses = [], []
    for g, (w, d) in enumerate(BRANCHES):
        o, l = branch_attend(q3[:, :, g], ctxs[g][0], ctxs[g][1], q_idxs[g], d, w // d + 1, biases[g])
        outs.append(o)
        lses.append(l)
    wts = jax.nn.softmax(jnp.stack(lses, axis=0), axis=0)
    return jnp.einsum('gbqh,gbqhd->bqhd', wts.astype(outs[0].dtype), jnp.stack(outs, axis=0))


def dilated_attention(q3, ctxs, offsets, biases):
    B, T = q3.shape[:2]
    if T >= Q_BLOCK and T % Q_BLOCK == 0:
        nb = T // Q_BLOCK
        qs = q3.reshape((B, nb, Q_BLOCK) + q3.shape[2:]).swapaxes(0, 1)

        def block(args):
            qb, start = args
            idxs = [off + start + jnp.arange(Q_BLOCK) for off in offsets]
            return dilated_mix(qb, ctxs, idxs, biases)

        out = lax.map(block, (qs, jnp.arange(nb) * Q_BLOCK))
        return out.swapaxes(0, 1).reshape(B, T, HEADS, HEAD_DIM)
    idxs = [off + jnp.arange(T) for off in offsets]
    return dilated_mix(q3, ctxs, idxs, biases)


def trunk(x, pool_state, kv_caches, pos0, norm_mix, pool_w, pool_scale, norm_mlp, mlp_in, mlp_out,
          norm_kv, w_kv, w_q, w_o, biases, norm_final):
    B, T, D = x.shape
    new_pool, new_kv = [], []
    ctxs, offsets = [], []
    for l in range(DEPTH):
        h = rmsnorm(x, norm_mix[l])
        if l < N_A_LAYERS:
            y, nbuf = pool_mix(h, pool_state[l], pos0, pool_w[l], pool_scale[l])
            new_pool.append(nbuf)
        else:
            if l == N_A_LAYERS:
                kv = (rmsnorm(x, norm_kv) @ w_kv).reshape(B, T, N_BRANCH, 2, HEADS, HEAD_DIM)
                for g, (w, d) in enumerate(BRANCHES):
                    cache = kv_caches[g].astype(kv.dtype)
                    full = jnp.concatenate([cache, kv[:, :, g]], axis=1)
                    ctxs.append((full[:, :, 0], full[:, :, 1]))
                    offsets.append(cache.shape[1])
                    new_kv.append(kv[:, max(T - w, 0):, g])
            lb = l - N_A_LAYERS
            q3 = (h @ w_q[lb]).reshape(B, T, N_BRANCH, HEADS, HEAD_DIM)
            o = dilated_attention(q3, ctxs, offsets, biases)
            y = o.reshape(B, T, ATT_WIDTH) @ w_o[lb]
        x = x + y
        h = rmsnorm(x, norm_mlp[l])
        x = x + jnp.square(jax.nn.relu(h @ mlp_in[l])) @ mlp_out[l]
    return rmsnorm(x, norm_final), jnp.stack(new_pool, axis=0), new_kv


def setup_inputs(seed: int = 0) -> dict:
    key = jax.random.key(seed)
    ks = jax.random.split(key, 24)
    f32 = jnp.float32
    nrm = lambda k, shape: jax.random.normal(k, shape, f32)
    win = [min(w, PAST_LEN) for (w, d) in BRANCHES]
    return {
        "x_prompt": nrm(ks[0], (BATCH, SEQ, D_MODEL)),
        "x_sample": nrm(ks[1], (DEC_BATCH, DEC_SEQ, D_MODEL)),
        "state_pool": nrm(ks[2], (N_A_LAYERS, DEC_BATCH, POOL_STATE, D_MODEL)),
        "cache_kv_w128": nrm(ks[3], (DEC_BATCH, win[0], 2, HEADS, HEAD_DIM)),
        "cache_kv_w512": nrm(ks[4], (DEC_BATCH, win[1], 2, HEADS, HEAD_DIM)),
        "cache_kv_w2048": nrm(ks[5], (DEC_BATCH, win[2], 2, HEADS, HEAD_DIM)),
        "norm_mix": 1.0 + 0.05 * nrm(ks[6], (DEPTH, D_MODEL)),
        "pool_w": nrm(ks[7], (N_A_LAYERS, N_POOL_GROUPS, POOL_GROUP, POOL_GROUP)) * POOL_GROUP ** -0.5,
        "pool_scale": 1.0 + 0.05 * nrm(ks[8], (N_A_LAYERS, D_MODEL)),
        "norm_mlp": 1.0 + 0.05 * nrm(ks[9], (DEPTH, D_MODEL)),
        "mlp_in": nrm(ks[10], (DEPTH, D_MODEL, D_FF)) * D_MODEL ** -0.5,
        "mlp_out": nrm(ks[11], (DEPTH, D_FF, D_MODEL)) * D_FF ** -0.5,
        "norm_kv": 1.0 + 0.05 * nrm(ks[12], (D_MODEL,)),
        "w_kv": nrm(ks[13], (D_MODEL, N_BRANCH * 2 * ATT_WIDTH)) * D_MODEL ** -0.5,
        "w_q": nrm(ks[14], (N_B_LAYERS, D_MODEL, N_BRANCH * ATT_WIDTH)) * D_MODEL ** -0.5,
        "w_o": nrm(ks[15], (N_B_LAYERS, ATT_WIDTH, D_MODEL)) * ATT_WIDTH ** -0.5,
        "rel_bias": 0.2 * nrm(ks[16], (NUM_BUCKETS, N_BRANCH * HEADS)),
        "norm_final": 1.0 + 0.05 * nrm(ks[17], (D_MODEL,)),
    }


def reference(x_prompt, x_sample, state_pool, cache_kv_w128, cache_kv_w512, cache_kv_w2048,
              norm_mix, pool_w, pool_scale, norm_mlp, mlp_in, mlp_out, norm_kv, w_kv, w_q, w_o,
              rel_bias, norm_final):
    biases = branch_biases(rel_bias)
    weights = (norm_mix, pool_w, pool_scale, norm_mlp, mlp_in, mlp_out, norm_kv, w_kv, w_q, w_o, biases, norm_final)
    Bp = x_prompt.shape[0]
    pool0 = jnp.zeros((N_A_LAYERS, Bp, POOL_STATE, D_MODEL), x_prompt.dtype)
    empty = [jnp.zeros((Bp, 0, 2, HEADS, HEAD_DIM), x_prompt.dtype) for _ in BRANCHES]
    y_prompt, pool_p, kv_p = trunk(x_prompt, pool0, empty, 0, *weights)
    y_sample, pool_s, kv_s = trunk(x_sample, state_pool, [cache_kv_w128, cache_kv_w512, cache_kv_w2048],
                                   PAST_LEN, *weights)
    return (y_prompt, y_sample, pool_p, pool_s, kv_p[0], kv_s[0], kv_p[1], kv_s[1], kv_p[2], kv_s[2])
```

```python
import functools
import math

import jax
import jax.numpy as jnp
import numpy as np
from jax import lax
from jax.experimental import pallas as pl
from jax.experimental.pallas import tpu as pltpu

F32 = jnp.float32
BF16 = jnp.bfloat16

D_MODEL = 1024
HEADS = 8
HEAD_DIM = 128
D_FF = 4 * D_MODEL
POOL_WINDOWS = (2, 4, 8, 16)
POOL_GROUP = D_MODEL // len(POOL_WINDOWS)
POOL_STATE = max(POOL_WINDOWS) - 1
BRANCHES = ((128, 1), (512, 4), (2048, 16))
N_KEYS = 129
NUM_BUCKETS = 32
MAX_DISTANCE = 2048
SAMPLE_POS0 = 8192
EPS = 1e-6
MASKED = -1e30

LANES = 128
N_SLABS = D_MODEL // LANES
Q_SUB = 128
VMEM_LIMIT = 56 * 1024 * 1024


def _cparams(*sem):
    return pltpu.CompilerParams(dimension_semantics=sem, vmem_limit_bytes=VMEM_LIMIT)


def _rms(x, g):
    ms = jnp.mean(x * x, axis=-1, keepdims=True)
    return x * lax.rsqrt(ms + EPS) * g


def _pool_kernel(x_ref, g_ref, w_ref, sc_ref, o_ref, nb_ref, hbuf, *, tm):
    t = pl.program_id(1)

    @pl.when(t == 0)
    def _():
        hbuf[0:16, :] = jnp.zeros((16, D_MODEL), F32)

    x = x_ref[0]
    h = _rms(x, g_ref[...])
    hbuf[16:16 + tm, :] = h
    pos = t * tm + lax.broadcasted_iota(jnp.int32, (tm, 1), 0)
    for gi, w in enumerate(POOL_WINDOWS):
        sl = slice(gi * POOL_GROUP, (gi + 1) * POOL_GROUP)
        hg = h[:, sl]
        s = hg
        for j in range(1, w):
            s = s + hbuf[16 - j:16 - j + tm, sl]
        cnt = jnp.minimum(pos + 1, w).astype(F32)
        pooled = s / cnt - hg
        y = jnp.dot(pooled.astype(BF16), w_ref[gi], preferred_element_type=F32)
        o_ref[0, :, sl] = x[:, sl] + y * sc_ref[:, sl]
    nb_ref[0] = hbuf[tm + 1:tm + 16, :]
    hbuf[0:16, :] = hbuf[tm:tm + 16, :]


def _pool_layer(x, gain, w_pool, scale, tm=512):
    B, T, D = x.shape
    return pl.pallas_call(
        functools.partial(_pool_kernel, tm=tm),
        grid=(B, T // tm),
        in_specs=[
            pl.BlockSpec((1, tm, D), lambda b, t: (b, t, 0)),
            pl.BlockSpec((1, D), lambda b, t: (0, 0)),
            pl.BlockSpec(w_pool.shape, lambda b, t: (0, 0, 0)),
            pl.BlockSpec((1, D), lambda b, t: (0, 0)),
        ],
        out_specs=[
            pl.BlockSpec((1, tm, D), lambda b, t: (b, t, 0)),
            pl.BlockSpec((1, POOL_STATE, D), lambda b, t: (b, 0, 0)),
        ],
        out_shape=[
            jax.ShapeDtypeStruct((B, T, D), F32),
            jax.ShapeDtypeStruct((B, POOL_STATE, D), F32),
        ],
        scratch_shapes=[pltpu.VMEM((16 + tm, D), F32)],
        compiler_params=_cparams("arbitrary", "arbitrary"),
        name="pool_layer",
    )(x, gain, w_pool, scale)


def _pool_sample_kernel(x_ref, st_ref, g_ref, w_ref, sc_ref, o_ref, h_ref):
    x = x_ref[...]
    h = _rms(x, g_ref[...])
    h_ref[...] = h
    for gi, w in enumerate(POOL_WINDOWS):
        sl = slice(gi * POOL_GROUP, (gi + 1) * POOL_GROUP)
        hg = h[:, sl]
        s = hg
        for j in range(1, w):
            s = s + st_ref[:, POOL_STATE - j, sl]
        pooled = s / float(min(SAMPLE_POS0 + 1, w)) - hg
        y = jnp.dot(pooled.astype(BF16), w_ref[gi], preferred_element_type=F32)
        o_ref[:, sl] = x[:, sl] + y * sc_ref[:, sl]


def _pool_sample(x, state, gain, w_pool, scale):
    Bs, D = x.shape
    return pl.pallas_call(
        _pool_sample_kernel,
        out_shape=[jax.ShapeDtypeStruct((Bs, D), F32), jax.ShapeDtypeStruct((Bs, D), F32)],
        compiler_params=pltpu.CompilerParams(vmem_limit_bytes=VMEM_LIMIT),
        name="pool_sample",
    )(x, state, gain, w_pool, scale)


def _mlp_kernel(x_ref, g_ref, w1_ref, w2_ref, gf_ref, o_ref, hs, *, nk, final_norm):
    k = pl.program_id(1)

    @pl.when(k == 0)
    def _():
        x = x_ref[...]
        hs[...] = _rms(x, g_ref[...]).astype(BF16)
        o_ref[...] = x

    u = jnp.dot(hs[...], w1_ref[...], preferred_element_type=F32)
    u = jnp.square(jnp.maximum(u, 0.0)).astype(BF16)
    o_ref[...] += jnp.dot(u, w2_ref[...], preferred_element_type=F32)

    if final_norm:
        @pl.when(k == nk - 1)
        def _():
            o_ref[...] = _rms(o_ref[...], gf_ref[...])


def _mlp(x, gain, w1, w2, gain_final, final_norm, tm, tf=512):
    M, D = x.shape
    nk = D_FF // tf
    return pl.pallas_call(
        functools.partial(_mlp_kernel, nk=nk, final_norm=final_norm),
        grid=(M // tm, nk),
        in_specs=[
            pl.BlockSpec((tm, D), lambda i, k: (i, 0)),
            pl.BlockSpec((1, D), lambda i, k: (0, 0)),
            pl.BlockSpec((D, tf), lambda i, k: (0, k)),
            pl.BlockSpec((tf, D), lambda i, k: (k, 0)),
            pl.BlockSpec((1, D), lambda i, k: (0, 0)),
        ],
        out_specs=pl.BlockSpec((tm, D), lambda i, k: (i, 0)),
        out_shape=jax.ShapeDtypeStruct((M, D), F32),
        scratch_shapes=[pltpu.VMEM((tm, D), BF16)],
        compiler_params=_cparams("arbitrary", "arbitrary"),
        name="mlp",
    )(x, gain, w1, w2, gain_final)


def _proj_kernel(x_ref, g_ref, w_ref, *refs, tm, nc, dil, scale, tail_rows):
    if tail_rows:
        o_ref, tail_ref, hsl, lhs, tsl = refs
    else:
        o_ref, hsl, lhs = refs
    n = tm // dil
    h = _rms(x_ref[0], g_ref[...])
    if dil == 1:
        lhs[...] = h.astype(BF16)
    else:
        for c in range(N_SLABS):
            hsl[c] = h[:, c * LANES:(c + 1) * LANES]
        for r in range(dil):
            for c in range(N_SLABS):
                lhs[r * n:(r + 1) * n, c * LANES:(c + 1) * LANES] = (
                    hsl[c, pl.ds(r, n, stride=dil), :].astype(BF16))
    res = jnp.dot(lhs[...], w_ref[...], preferred_element_type=F32)
    if scale != 1.0:
        res = res * scale
    for r in range(dil):
        o_ref[0, r] = res[r * n:(r + 1) * n].astype(o_ref.dtype)

    if tail_rows:
        i = pl.program_id(1)
        nt = pl.num_programs(1)
        if tail_rows < tm:
            @pl.when(i == nt - 1)
            def _():
                tail_ref[0] = res[tm - tail_rows:tm]
        else:
            @pl.when(i >= nt - tail_rows // tm)
            def _():
                if dil == 1:
                    tail_ref[0] = res
                else:
                    for c in range(nc // LANES):
                        for r in range(dil):
                            tsl[c, pl.ds(r, n, stride=dil), :] = (
                                res[r * n:(r + 1) * n, c * LANES:(c + 1) * LANES])
                    for c in range(nc // LANES):
                        tail_ref[0, :, c * LANES:(c + 1) * LANES] = tsl[c]


def _proj(x, gain, w, dil, scale=1.0, tail_rows=0, tm=512):
    B, T, D = x.shape
    nc = w.shape[1]
    nt = T // tm
    out_specs = [pl.BlockSpec((1, dil, tm // dil, nc), lambda b, i: (b, 0, i, 0))]
    out_shape = [jax.ShapeDtypeStruct((B, dil, T // dil, nc), BF16)]
    scratch = [pltpu.VMEM((N_SLABS, tm, LANES), F32), pltpu.VMEM((tm, D), BF16)]
    if tail_rows:
        if tail_rows < tm:
            out_specs.append(pl.BlockSpec((1, tail_rows, nc), lambda b, i: (b, 0, 0)))
        else:
            first = nt - tail_rows // tm
            out_specs.append(pl.BlockSpec((1, tm, nc), lambda b, i: (b, jnp.maximum(i - first, 0), 0)))
        out_shape.append(jax.ShapeDtypeStruct((B, tail_rows, nc), F32))
        scratch.append(pltpu.VMEM((nc // LANES, tm, LANES), F32))
    return pl.pallas_call(
        functools.partial(_proj_kernel, tm=tm, nc=nc, dil=dil, scale=scale, tail_rows=tail_rows),
        grid=(B, nt),
        in_specs=[
            pl.BlockSpec((1, tm, D), lambda b, i: (b, i, 0)),
            pl.BlockSpec((1, D), lambda b, i: (0, 0)),
            pl.BlockSpec((D, nc), lambda b, i: (0, 0)),
        ],
        out_specs=out_specs,
        out_shape=out_shape,
        scratch_shapes=scratch,
        compiler_params=_cparams("arbitrary", "arbitrary"),
        name=f"proj_d{dil}",
    )(x, gain, w)


def _proj_sample_kernel(x_ref, g_ref, w_ref, o_ref, *, scale):
    h = _rms(x_ref[...], g_ref[...]).astype(BF16)
    res = jnp.dot(h, w_ref[...], preferred_element_type=F32)
    if scale != 1.0:
        res = res * scale
    o_ref[...] = res


def _proj_sample(x, gain, w, scale=1.0, tn=1024):
    Bs, D = x.shape
    nc = w.shape[1]
    return pl.pallas_call(
        functools.partial(_proj_sample_kernel, scale=scale),
        grid=(nc // tn,),
        in_specs=[
            pl.BlockSpec((Bs, D), lambda j: (0, 0)),
            pl.BlockSpec((1, D), lambda j: (0, 0)),
            pl.BlockSpec((D, tn), lambda j: (0, j)),
        ],
        out_specs=pl.BlockSpec((Bs, tn), lambda j: (0, j)),
        out_shape=jax.ShapeDtypeStruct((Bs, nc), F32),
        compiler_params=_cparams("arbitrary"),
        name="proj_sample",
    )(x, gain, w)


def _attn_kernel(q_ref, kv_ref, bias_ref, o_ref, lse_ref, kc, vc, *, tq):
    first = pl.program_id(2) == 0

    @pl.when(first)
    def _():
        kc[0:Q_SUB, :] = jnp.zeros((Q_SUB, D_MODEL), BF16)
        vc[0:Q_SUB, :] = jnp.zeros((Q_SUB, D_MODEL), BF16)

    kc[Q_SUB:Q_SUB + tq, :] = kv_ref[0, 0, :, 0:D_MODEL]
    vc[Q_SUB:Q_SUB + tq, :] = kv_ref[0, 0, :, D_MODEL:2 * D_MODEL]
    col = lax.broadcasted_iota(jnp.int32, (Q_SUB, 2 * Q_SUB), 1)
    lane = lax.broadcasted_iota(jnp.int32, (Q_SUB, LANES), 1)
    no_prev = jnp.logical_and(first, col < Q_SUB)
    for s in range(tq // Q_SUB):
        r0 = s * Q_SUB
        lse_tile = jnp.zeros((Q_SUB, LANES), F32)
        for h in range(HEADS):
            cs = slice(h * HEAD_DIM, (h + 1) * HEAD_DIM)
            q = q_ref[0, 0, r0:r0 + Q_SUB, cs]
            k = kc[r0:r0 + 2 * Q_SUB, cs]
            sc = lax.dot_general(q, k, (((1,), (1,)), ((), ())), preferred_element_type=F32)
            sc = sc + bias_ref[h]
            if s == 0:
                sc = jnp.where(no_prev, MASKED, sc)
            m = jnp.max(sc, axis=-1, keepdims=True)
            p = jnp.exp(sc - m)
            l = jnp.sum(p, axis=-1, keepdims=True)
            v = vc[r0:r0 + 2 * Q_SUB, cs]
            o = jnp.dot(p.astype(BF16), v, preferred_element_type=F32)
            o_ref[0, 0, r0:r0 + Q_SUB, cs] = (o / l).astype(o_ref.dtype)
            lse_tile = jnp.where(lane == h, m + jnp.log(l), lse_tile)
        lse_ref[0, 0, r0:r0 + Q_SUB, :] = lse_tile
    kc[0:Q_SUB, :] = kc[tq:tq + Q_SUB, :]
    vc[0:Q_SUB, :] = vc[tq:tq + Q_SUB, :]


def _attn_branch(q, kv, bias, tq=256):
    B, dil, L, D = q.shape
    tq = min(tq, L)
    return pl.pallas_call(
        functools.partial(_attn_kernel, tq=tq),
        grid=(B, dil, L // tq),
        in_specs=[
            pl.BlockSpec((1, 1, tq, D), lambda b, r, i: (b, r, i, 0)),
            pl.BlockSpec((1, 1, tq, 2 * D), lambda b, r, i: (b, r, i, 0)),
            pl.BlockSpec(bias.shape, lambda b, r, i: (0, 0, 0)),
        ],
        out_specs=[
            pl.BlockSpec((1, 1, tq, D), lambda b, r, i: (b, r, i, 0)),
            pl.BlockSpec((1, 1, tq, LANES), lambda b, r, i: (b, r, i, 0)),
        ],
        out_shape=[
            jax.ShapeDtypeStruct((B, dil, L, D), BF16),
            jax.ShapeDtypeStruct((B, dil, L, LANES), F32),
        ],
        scratch_shapes=[pltpu.VMEM((Q_SUB + tq, D), BF16), pltpu.VMEM((Q_SUB + tq, D), BF16)],
        compiler_params=_cparams("arbitrary", "arbitrary", "arbitrary"),
        name=f"attn_d{dil}",
    )(q, kv, bias)


def _merge_kernel(x_ref, o0_ref, o1_ref, o2_ref, l0_ref, l1_ref, l2_ref, wo_ref, out_ref,
                  osl, lsl1, lsl2, acc, lhs, *, tm):
    o_refs = (o0_ref, o1_ref, o2_ref)
    l_refs = (l0_ref, l1_ref, l2_ref)
    lsls = (None, lsl1, lsl2)
    lses = []
    for g, (_, dil) in enumerate(BRANCHES):
        if dil == 1:
            lses.append(l_refs[g][0, 0])
        else:
            n = tm // dil
            for r in range(dil):
                lsls[g][pl.ds(r, n, stride=dil), :] = l_refs[g][0, r]
            lses.append(lsls[g][...])
    m = jnp.maximum(jnp.maximum(lses[0], lses[1]), lses[2])
    es = [jnp.exp(l - m) for l in lses]
    den = es[0] + es[1] + es[2]
    ws = [e / den for e in es]
    for g, (_, dil) in enumerate(BRANCHES):
        n = tm // dil
        if dil > 1:
            for r in range(dil):
                for c in range(N_SLABS):
                    osl[c, pl.ds(r, n, stride=dil), :] = (
                        o_refs[g][0, r, :, c * LANES:(c + 1) * LANES].astype(F32))
        for c in range(N_SLABS):
            og = osl[c] if dil > 1 else o_refs[g][0, 0, :, c * LANES:(c + 1) * LANES].astype(F32)
            contrib = ws[g][:, c:c + 1] * og
            if g == 0:
                acc[c] = contrib
            else:
                acc[c] += contrib
    for c in range(N_SLABS):
        lhs[:, c * LANES:(c + 1) * LANES] = acc[c].astype(BF16)
    out_ref[0] = x_ref[0] + jnp.dot(lhs[...], wo_ref[...], preferred_element_type=F32)


def _merge_out(x, os_, lses, wo, tm=512):
    B, T, D = x.shape
    in_specs = [pl.BlockSpec((1, tm, D), lambda b, i: (b, i, 0))]
    for (_, dil) in BRANCHES:
        in_specs.append(pl.BlockSpec((1, dil, tm // dil, D), lambda b, i: (b, 0, i, 0)))
    for (_, dil) in BRANCHES:
        in_specs.append(pl.BlockSpec((1, dil, tm // dil, LANES), lambda b, i: (b, 0, i, 0)))
    in_specs.append(pl.BlockSpec((D, D), lambda b, i: (0, 0)))
    return pl.pallas_call(
        functools.partial(_merge_kernel, tm=tm),
        grid=(B, T // tm),
        in_specs=in_specs,
        out_specs=pl.BlockSpec((1, tm, D), lambda b, i: (b, i, 0)),
        out_shape=jax.ShapeDtypeStruct((B, T, D), F32),
        scratch_shapes=[
            pltpu.VMEM((N_SLABS, tm, LANES), F32),
            pltpu.VMEM((tm, LANES), F32),
            pltpu.VMEM((tm, LANES), F32),
            pltpu.VMEM((N_SLABS, tm, LANES), F32),
            pltpu.VMEM((tm, D), BF16),
        ],
        compiler_params=_cparams("arbitrary", "arbitrary"),
        name="merge_out",
    )(x, *os_, *lses, wo)


def _attn_sample_kernel(q_ref, kvn_ref, c0_ref, c1_ref, c2_ref, bias_ref, o_ref):
    caches = (c0_ref, c1_ref, c2_ref)
    sc_c, sc_n = [], []
    for g in range(len(BRANCHES)):
        q = q_ref[0, g]
        kc = caches[g][0, :, 0]
        sc_c.append(jnp.sum(kc * q[None], axis=-1, keepdims=True) + bias_ref[g, 0:N_KEYS - 1])
        sc_n.append(jnp.sum(kvn_ref[0, g, 0] * q, axis=-1, keepdims=True) + bias_ref[g, N_KEYS - 1])
    m = sc_n[0]
    for g in range(len(BRANCHES)):
        m = jnp.maximum(m, jnp.maximum(jnp.max(sc_c[g], axis=0), sc_n[g]))
    num = jnp.zeros((HEADS, HEAD_DIM), F32)
    den = jnp.zeros((HEADS, HEAD_DIM), F32)
    for g in range(len(BRANCHES)):
        p_c = jnp.exp(sc_c[g] - m[None])
        p_n = jnp.exp(sc_n[g] - m)
        den = den + jnp.sum(p_c, axis=0) + p_n
        num = num + jnp.sum(p_c * caches[g][0, :, 1], axis=0) + p_n * kvn_ref[0, g, 1]
    o_ref[0] = num / den


def _attn_sample(q, kvn, caches, bias):
    Bs = q.shape[0]
    in_specs = [
        pl.BlockSpec((1,) + q.shape[1:], lambda b: (b, 0, 0, 0)),
        pl.BlockSpec((1,) + kvn.shape[1:], lambda b: (b, 0, 0, 0, 0)),
    ]
    for c in caches:
        in_specs.append(pl.BlockSpec((1, N_KEYS - 1, None, 2, HEADS, HEAD_DIM),
                                     lambda b: (b, 0, 0, 0, 0, 0)))
    in_specs.append(pl.BlockSpec(bias.shape, lambda b: (0, 0, 0, 0)))
    return pl.pallas_call(
        _attn_sample_kernel,
        grid=(Bs,),
        in_specs=in_specs,
        out_specs=pl.BlockSpec((1, HEADS, HEAD_DIM), lambda b: (b, 0, 0)),
        out_shape=jax.ShapeDtypeStruct((Bs, HEADS, HEAD_DIM), F32),
        compiler_params=_cparams("arbitrary"),
        name="attn_sample",
    )(q, kvn, *caches, bias)


def _out_sample_kernel(x_ref, o_ref, wo_ref, out_ref):
    out_ref[...] = x_ref[...] + jnp.dot(o_ref[...].astype(BF16), wo_ref[...],
                                        preferred_element_type=F32)


def _out_sample(x, o, wo):
    return pl.pallas_call(
        _out_sample_kernel,
        out_shape=jax.ShapeDtypeStruct(x.shape, F32),
        compiler_params=pltpu.CompilerParams(vmem_limit_bytes=VMEM_LIMIT),
        name="out_sample",
    )(x, o, wo)


def _t5_bucket(dist):
    max_exact = NUM_BUCKETS // 2
    df = jnp.maximum(dist, 1).astype(F32)
    large = max_exact + (jnp.log(df / max_exact) / math.log(MAX_DISTANCE / max_exact)
                         * (NUM_BUCKETS - max_exact)).astype(jnp.int32)
    large = jnp.minimum(large, NUM_BUCKETS - 1)
    return jnp.where(dist < max_exact, dist, large)


def _bias_tables(rel_bias):
    out = []
    for g, (w, d) in enumerate(BRANCHES):
        dist = jnp.arange(N_KEYS, dtype=jnp.int32) * d
        out.append(rel_bias[_t5_bucket(dist)][:, g * HEADS:(g + 1) * HEADS])
    return out


def _block_bias(tab):
    qi = np.arange(Q_SUB)[:, None]
    c = np.arange(2 * Q_SUB)[None, :]
    j = qi + Q_SUB - c
    valid = (j >= 0) & (j <= N_KEYS - 1)
    b = tab[np.clip(j, 0, N_KEYS - 1)]
    b = jnp.where(valid[:, :, None], b, MASKED)
    return jnp.transpose(b, (2, 0, 1))


def kernel(x_prompt, x_sample, state_pool, cache_kv_w128, cache_kv_w512, cache_kv_w2048,
           norm_mix, pool_w, pool_scale, norm_mlp, mlp_in, mlp_out, norm_kv, w_kv, w_q, w_o,
           rel_bias, norm_final):
    B, T, D = x_prompt.shape
    Bs = x_sample.shape[0]
    n_a = pool_w.shape[0]
    depth = norm_mix.shape[0]
    n_br = len(BRANCHES)
    qk_scale = HEAD_DIM ** -0.5

    pool_w_b = pool_w.astype(BF16)
    mlp_in_b = mlp_in.astype(BF16)
    mlp_out_b = mlp_out.astype(BF16)
    w_kv_b = w_kv.astype(BF16)
    w_q_b = w_q.astype(BF16)
    w_o_b = w_o.astype(BF16)
    row = lambda v: v.reshape(1, D)

    tabs = _bias_tables(rel_bias)
    blk_bias = [_block_bias(t) for t in tabs]
    smp_bias = jnp.stack([jnp.broadcast_to(t[::-1][:, :, None], (N_KEYS, HEADS, HEAD_DIM))
                          for t in tabs])

    caches = [c.reshape(Bs, N_KEYS - 1, d, 2, HEADS, HEAD_DIM)
              for c, (w, d) in zip((cache_kv_w128, cache_kv_w512, cache_kv_w2048), BRANCHES)]

    xp = x_prompt
    xs = x_sample.reshape(Bs, D)
    pool_p, pool_s, kv_p = [], [], []
    kvs = kvn = None
    for l in range(depth):
        last = l == depth - 1
        if l < n_a:
            xp, nbuf = _pool_layer(xp, row(norm_mix[l]), pool_w_b[l], row(pool_scale[l]))
            pool_p.append(nbuf)
            xs, hs = _pool_sample(xs, state_pool[l], row(norm_mix[l]), pool_w_b[l], row(pool_scale[l]))
            pool_s.append(jnp.concatenate([state_pool[l][:, 1:], hs[:, None, :]], axis=1))
        else:
            lb = l - n_a
            if l == n_a:
                kvs = []
                for g, (w, d) in enumerate(BRANCHES):
                    wg = w_kv_b[:, g * 2 * D:(g + 1) * 2 * D]
                    kv_g, tail = _proj(xp, row(norm_kv), wg, d, tail_rows=min(w, T))
                    kvs.append(kv_g)
                    kv_p.append(tail.reshape(B, min(w, T), 2, HEADS, HEAD_DIM))
                kvn = _proj_sample(xs, row(norm_kv), w_kv_b)
            os_, lses = [], []
            for g, (w, d) in enumerate(BRANCHES):
                wg = w_q_b[lb][:, g * D:(g + 1) * D]
                (q_g,) = _proj(xp, row(norm_mix[l]), wg, d, scale=qk_scale)
                o_g, lse_g = _attn_branch(q_g, kvs[g], blk_bias[g])
                os_.append(o_g)
                lses.append(lse_g)
            xp = _merge_out(xp, os_, lses, w_o_b[lb])
            qs = _proj_sample(xs, row(norm_mix[l]), w_q_b[lb], scale=qk_scale)
            o_s = _attn_sample(qs.reshape(Bs, n_br, HEADS, HEAD_DIM),
                               kvn.reshape(Bs, n_br, 2, HEADS, HEAD_DIM), caches, smp_bias)
            xs = _out_sample(xs, o_s.reshape(Bs, D), w_o_b[lb])
        xp = _mlp(xp.reshape(B * T, D), row(norm_mlp[l]), mlp_in_b[l], mlp_out_b[l],
                  row(norm_final), last, tm=1024).reshape(B, T, D)
        xs = _mlp(xs, row(norm_mlp[l]), mlp_in_b[l], mlp_out_b[l], row(norm_final), last, tm=Bs)

    kv_s = [kvn[:, g * 2 * D:(g + 1) * 2 * D].reshape(Bs, 1, 2, HEADS, HEAD_DIM) for g in range(n_br)]
    return (xp, xs.reshape(Bs, 1, D), jnp.stack(pool_p), jnp.stack(pool_s),
            kv_p[0], kv_s[0], kv_p[1], kv_s[1], kv_p[2], kv_s[2])
```

```python
import functools
import math

import jax
import jax.numpy as jnp
import numpy as np
from jax import lax
from jax.experimental import pallas as pl
from jax.experimental.pallas import tpu as pltpu

F32 = jnp.float32
BF16 = jnp.bfloat16

D_MODEL = 1024
HEADS = 8
HEAD_DIM = 128
D_FF = 4 * D_MODEL
POOL_WINDOWS = (2, 4, 8, 16)
POOL_GROUP = D_MODEL // len(POOL_WINDOWS)
POOL_STATE = max(POOL_WINDOWS) - 1
BRANCHES = ((128, 1), (512, 4), (2048, 16))
N_KEYS = 129
NUM_BUCKETS = 32
MAX_DISTANCE = 2048
SAMPLE_POS0 = 8192
EPS = 1e-6
MASKED = -1e30

LANES = 128
N_SLABS = D_MODEL // LANES
Q_SUB = 128
VMEM_LIMIT = 56 * 1024 * 1024


def _cparams(*sem):
    return pltpu.CompilerParams(dimension_semantics=sem, vmem_limit_bytes=VMEM_LIMIT)


def _rms(x, g):
    ms = jnp.mean(x * x, axis=-1, keepdims=True)
    return x * lax.rsqrt(ms + EPS) * g


def _pool_kernel(x_ref, g_ref, w_ref, sc_ref, o_ref, nb_ref, hbuf, *, tm):
    t = pl.program_id(1)

    @pl.when(t == 0)
    def _():
        hbuf[0:16, :] = jnp.zeros((16, D_MODEL), F32)

    x = x_ref[0]
    h = _rms(x, g_ref[...])
    hbuf[16:16 + tm, :] = h
    pos = t * tm + lax.broadcasted_iota(jnp.int32, (tm, 1), 0)
    for gi, w in enumerate(POOL_WINDOWS):
        sl = slice(gi * POOL_GROUP, (gi + 1) * POOL_GROUP)
        hg = h[:, sl]
        s = hg
        for j in range(1, w):
            s = s + hbuf[16 - j:16 - j + tm, sl]
        cnt = jnp.minimum(pos + 1, w).astype(F32)
        pooled = s / cnt - hg
        y = jnp.dot(pooled.astype(BF16), w_ref[gi], preferred_element_type=F32)
        o_ref[0, :, sl] = x[:, sl] + y * sc_ref[:, sl]
    nb_ref[0] = hbuf[tm + 1:tm + 16, :]
    hbuf[0:16, :] = hbuf[tm:tm + 16, :]


def _pool_layer(x, gain, w_pool, scale, tm=512):
    B, T, D = x.shape
    return pl.pallas_call(
        functools.partial(_pool_kernel, tm=tm),
        grid=(B, T // tm),
        in_specs=[
            pl.BlockSpec((1, tm, D), lambda b, t: (b, t, 0)),
            pl.BlockSpec((1, D), lambda b, t: (0, 0)),
            pl.BlockSpec(w_pool.shape, lambda b, t: (0, 0, 0)),
            pl.BlockSpec((1, D), lambda b, t: (0, 0)),
        ],
        out_specs=[
            pl.BlockSpec((1, tm, D), lambda b, t: (b, t, 0)),
            pl.BlockSpec((1, POOL_STATE, D), lambda b, t: (b, 0, 0)),
        ],
        out_shape=[
            jax.ShapeDtypeStruct((B, T, D), F32),
            jax.ShapeDtypeStruct((B, POOL_STATE, D), F32),
        ],
        scratch_shapes=[pltpu.VMEM((16 + tm, D), F32)],
        compiler_params=_cparams("arbitrary", "arbitrary"),
        name="pool_layer",
    )(x, gain, w_pool, scale)


def _pool_sample_kernel(x_ref, st_ref, g_ref, w_ref, sc_ref, o_ref, h_ref):
    x = x_ref[...]
    h = _rms(x, g_ref[...])
    h_ref[...] = h
    for gi, w in enumerate(POOL_WINDOWS):
        sl = slice(gi * POOL_GROUP, (gi + 1) * POOL_GROUP)
        hg = h[:, sl]
        s = hg
        for j in range(1, w):
            s = s + st_ref[:, POOL_STATE - j, sl]
        pooled = s / float(min(SAMPLE_POS0 + 1, w)) - hg
        y = jnp.dot(pooled.astype(BF16), w_ref[gi], preferred_element_type=F32)
        o_ref[:, sl] = x[:, sl] + y * sc_ref[:, sl]


def _pool_sample(x, state, gain, w_pool, scale):
    Bs, D = x.shape
    return pl.pallas_call(
        _pool_sample_kernel,
        out_shape=[jax.ShapeDtypeStruct((Bs, D), F32), jax.ShapeDtypeStruct((Bs, D), F32)],
        compiler_params=pltpu.CompilerParams(vmem_limit_bytes=VMEM_LIMIT),
        name="pool_sample",
    )(x, state, gain, w_pool, scale)


def _mlp_kernel(x_ref, g_ref, w1_ref, w2_ref, gf_ref, o_ref, hs, us, *, tc, final_norm):
    x = x_ref[...]
    hs[...] = _rms(x, g_ref[...]).astype(BF16)
    for c in range(D_FF // tc):
        u = jnp.dot(hs[...], w1_ref[:, c * tc:(c + 1) * tc], preferred_element_type=F32)
        us[:, c * tc:(c + 1) * tc] = jnp.square(jnp.maximum(u, 0.0)).astype(BF16)
    y = x + jnp.dot(us[...], w2_ref[...], preferred_element_type=F32)
    if final_norm:
        y = _rms(y, gf_ref[...])
    o_ref[...] = y


def _mlp(x, gains, w1, w2, gain_final, layer, final_norm, tm, tc=512):
    M, D = x.shape
    return pl.pallas_call(
        functools.partial(_mlp_kernel, tc=tc, final_norm=final_norm),
        grid=(M // tm,),
        in_specs=[
            pl.BlockSpec((tm, D), lambda i: (i, 0)),
            pl.BlockSpec((None, 1, D), lambda i: (layer, 0, 0)),
            pl.BlockSpec((None, D, D_FF), lambda i: (layer, 0, 0), pipeline_mode=pl.Buffered(1)),
            pl.BlockSpec((None, D_FF, D), lambda i: (layer, 0, 0), pipeline_mode=pl.Buffered(1)),
            pl.BlockSpec((1, D), lambda i: (0, 0)),
        ],
        out_specs=pl.BlockSpec((tm, D), lambda i: (i, 0)),
        out_shape=jax.ShapeDtypeStruct((M, D), F32),
        scratch_shapes=[pltpu.VMEM((tm, D), BF16), pltpu.VMEM((tm, D_FF), BF16)],
        compiler_params=_cparams("arbitrary"),
        name="mlp",
    )(x, gains, w1, w2, gain_final)


def _proj_kernel(x_ref, g_ref, w_ref, *refs, tm, nc, dil, scale, tail_rows):
    if tail_rows:
        o_ref, tail_ref, hsl, lhs, tsl = refs
    else:
        o_ref, hsl, lhs = refs
    n = tm // dil
    h = _rms(x_ref[0], g_ref[...])
    if dil == 1:
        lhs[...] = h.astype(BF16)
    else:
        for c in range(N_SLABS):
            hsl[c] = h[:, c * LANES:(c + 1) * LANES]
        for r in range(dil):
            for c in range(N_SLABS):
                lhs[r * n:(r + 1) * n, c * LANES:(c + 1) * LANES] = (
                    hsl[c, pl.ds(r, n, stride=dil), :].astype(BF16))
    res = jnp.dot(lhs[...], w_ref[...], preferred_element_type=F32)
    if scale != 1.0:
        res = res * scale
    for r in range(dil):
        o_ref[0, r] = res[r * n:(r + 1) * n].astype(o_ref.dtype)

    if tail_rows:
        i = pl.program_id(1)
        nt = pl.num_programs(1)
        if tail_rows < tm:
            @pl.when(i == nt - 1)
            def _():
                tail_ref[0] = res[tm - tail_rows:tm]
        else:
            @pl.when(i >= nt - tail_rows // tm)
            def _():
                if dil == 1:
                    tail_ref[0] = res
                else:
                    for c in range(nc // LANES):
                        for r in range(dil):
                            tsl[c, pl.ds(r, n, stride=dil), :] = (
                                res[r * n:(r + 1) * n, c * LANES:(c + 1) * LANES])
                    for c in range(nc // LANES):
                        tail_ref[0, :, c * LANES:(c + 1) * LANES] = tsl[c]


def _proj(x, gains, gi, w, wi, cb, nc, dil, scale=1.0, tail_rows=0, tm=512):
    B, T, D = x.shape
    nt = T // tm
    out_specs = [pl.BlockSpec((1, dil, tm // dil, nc), lambda b, i: (b, 0, i, 0))]
    out_shape = [jax.ShapeDtypeStruct((B, dil, T // dil, nc), BF16)]
    scratch = [pltpu.VMEM((N_SLABS, tm, LANES), F32), pltpu.VMEM((tm, D), BF16)]
    if tail_rows:
        if tail_rows < tm:
            out_specs.append(pl.BlockSpec((1, tail_rows, nc), lambda b, i: (b, 0, 0)))
        else:
            first = nt - tail_rows // tm
            out_specs.append(pl.BlockSpec((1, tm, nc), lambda b, i: (b, jnp.maximum(i - first, 0), 0)))
        out_shape.append(jax.ShapeDtypeStruct((B, tail_rows, nc), F32))
        scratch.append(pltpu.VMEM((nc // LANES, tm, LANES), F32))
    return pl.pallas_call(
        functools.partial(_proj_kernel, tm=tm, nc=nc, dil=dil, scale=scale, tail_rows=tail_rows),
        grid=(B, nt),
        in_specs=[
            pl.BlockSpec((1, tm, D), lambda b, i: (b, i, 0)),
            pl.BlockSpec((None, 1, D), lambda b, i: (gi, 0, 0)),
            pl.BlockSpec((None, D, nc), lambda b, i: (wi, 0, cb)),
        ],
        out_specs=out_specs,
        out_shape=out_shape,
        scratch_shapes=scratch,
        compiler_params=_cparams("arbitrary", "arbitrary"),
        name=f"proj_d{dil}",
    )(x, gains, w)


def _proj_sample_kernel(x_ref, g_ref, w_ref, o_ref, *, scale):
    h = _rms(x_ref[...], g_ref[...]).astype(BF16)
    res = jnp.dot(h, w_ref[...], preferred_element_type=F32)
    if scale != 1.0:
        res = res * scale
    o_ref[...] = res


def _proj_sample(x, gains, gi, w, wi, scale=1.0, tn=1024):
    Bs, D = x.shape
    nc = w.shape[2]
    return pl.pallas_call(
        functools.partial(_proj_sample_kernel, scale=scale),
        grid=(nc // tn,),
        in_specs=[
            pl.BlockSpec((Bs, D), lambda j: (0, 0)),
            pl.BlockSpec((None, 1, D), lambda j: (gi, 0, 0)),
            pl.BlockSpec((None, D, tn), lambda j: (wi, 0, j)),
        ],
        out_specs=pl.BlockSpec((Bs, tn), lambda j: (0, j)),
        out_shape=jax.ShapeDtypeStruct((Bs, nc), F32),
        compiler_params=_cparams("arbitrary"),
        name="proj_sample",
    )(x, gains, w)


def _attn_kernel(q_ref, kv_ref, bias_ref, o_ref, lse_ref, kc, vc, *, tq):
    first = pl.program_id(2) == 0

    @pl.when(first)
    def _():
        kc[0:Q_SUB, :] = jnp.zeros((Q_SUB, D_MODEL), BF16)
        vc[0:Q_SUB, :] = jnp.zeros((Q_SUB, D_MODEL), BF16)

    kc[Q_SUB:Q_SUB + tq, :] = kv_ref[0, 0, :, 0:D_MODEL]
    vc[Q_SUB:Q_SUB + tq, :] = kv_ref[0, 0, :, D_MODEL:2 * D_MODEL]
    col = lax.broadcasted_iota(jnp.int32, (Q_SUB, 2 * Q_SUB), 1)
    lane = lax.broadcasted_iota(jnp.int32, (Q_SUB, LANES), 1)
    no_prev = jnp.logical_and(first, col < Q_SUB)
    for s in range(tq // Q_SUB):
        r0 = s * Q_SUB
        lse_tile = jnp.zeros((Q_SUB, LANES), F32)
        for h in range(HEADS):
            cs = slice(h * HEAD_DIM, (h + 1) * HEAD_DIM)
            q = q_ref[0, 0, r0:r0 + Q_SUB, cs]
            k = kc[r0:r0 + 2 * Q_SUB, cs]
            sc = lax.dot_general(q, k, (((1,), (1,)), ((), ())), preferred_element_type=F32)
            sc = sc + bias_ref[h]
            if s == 0:
                sc = jnp.where(no_prev, MASKED, sc)
            m = jnp.max(sc, axis=-1, keepdims=True)
            p = jnp.exp(sc - m)
            l = jnp.sum(p, axis=-1, keepdims=True)
            v = vc[r0:r0 + 2 * Q_SUB, cs]
            o = jnp.dot(p.astype(BF16), v, preferred_element_type=F32)
            o_ref[0, 0, r0:r0 + Q_SUB, cs] = (o / l).astype(o_ref.dtype)
            lse_tile = jnp.where(lane == h, m + jnp.log(l), lse_tile)
        lse_ref[0, 0, r0:r0 + Q_SUB, :] = lse_tile
    kc[0:Q_SUB, :] = kc[tq:tq + Q_SUB, :]
    vc[0:Q_SUB, :] = vc[tq:tq + Q_SUB, :]


def _attn_branch(q, kv, bias, tq=256):
    B, dil, L, D = q.shape
    tq = min(tq, L)
    return pl.pallas_call(
        functools.partial(_attn_kernel, tq=tq),
        grid=(B, dil, L // tq),
        in_specs=[
            pl.BlockSpec((1, 1, tq, D), lambda b, r, i: (b, r, i, 0)),
            pl.BlockSpec((1, 1, tq, 2 * D), lambda b, r, i: (b, r, i, 0)),
            pl.BlockSpec(bias.shape, lambda b, r, i: (0, 0, 0)),
        ],
        out_specs=[
            pl.BlockSpec((1, 1, tq, D), lambda b, r, i: (b, r, i, 0)),
            pl.BlockSpec((1, 1, tq, LANES), lambda b, r, i: (b, r, i, 0)),
        ],
        out_shape=[
            jax.ShapeDtypeStruct((B, dil, L, D), BF16),
            jax.ShapeDtypeStruct((B, dil, L, LANES), F32),
        ],
        scratch_shapes=[pltpu.VMEM((Q_SUB + tq, D), BF16), pltpu.VMEM((Q_SUB + tq, D), BF16)],
        compiler_params=_cparams("arbitrary", "arbitrary", "arbitrary"),
        name=f"attn_d{dil}",
    )(q, kv, bias)


def _merge_kernel(x_ref, o0_ref, o1_ref, o2_ref, l0_ref, l1_ref, l2_ref, wo_ref, out_ref,
                  osl, lsl1, lsl2, acc, lhs, *, tm):
    o_refs = (o0_ref, o1_ref, o2_ref)
    l_refs = (l0_ref, l1_ref, l2_ref)
    lsls = (None, lsl1, lsl2)
    lses = []
    for g, (_, dil) in enumerate(BRANCHES):
        if dil == 1:
            lses.append(l_refs[g][0, 0])
        else:
            n = tm // dil
            for r in range(dil):
                lsls[g][pl.ds(r, n, stride=dil), :] = l_refs[g][0, r]
            lses.append(lsls[g][...])
    m = jnp.maximum(jnp.maximum(lses[0], lses[1]), lses[2])
    es = [jnp.exp(l - m) for l in lses]
    den = es[0] + es[1] + es[2]
    ws = [e / den for e in es]
    for g, (_, dil) in enumerate(BRANCHES):
        n = tm // dil
        if dil > 1:
            for r in range(dil):
                for c in range(N_SLABS):
                    osl[c, pl.ds(r, n, stride=dil), :] = (
                        o_refs[g][0, r, :, c * LANES:(c + 1) * LANES].astype(F32))
        for c in range(N_SLABS):
            og = osl[c] if dil > 1 else o_refs[g][0, 0, :, c * LANES:(c + 1) * LANES].astype(F32)
            contrib = ws[g][:, c:c + 1] * og
            if g == 0:
                acc[c] = contrib
            else:
                acc[c] += contrib
    for c in range(N_SLABS):
        lhs[:, c * LANES:(c + 1) * LANES] = acc[c].astype(BF16)
    out_ref[0] = x_ref[0] + jnp.dot(lhs[...], wo_ref[...], preferred_element_type=F32)


def _merge_out(x, os_, lses, wo, wi, tm=512):
    B, T, D = x.shape
    in_specs = [pl.BlockSpec((1, tm, D), lambda b, i: (b, i, 0))]
    for (_, dil) in BRANCHES:
        in_specs.append(pl.BlockSpec((1, dil, tm // dil, D), lambda b, i: (b, 0, i, 0)))
    for (_, dil) in BRANCHES:
        in_specs.append(pl.BlockSpec((1, dil, tm // dil, LANES), lambda b, i: (b, 0, i, 0)))
    in_specs.append(pl.BlockSpec((None, D, D), lambda b, i: (wi, 0, 0)))
    return pl.pallas_call(
        functools.partial(_merge_kernel, tm=tm),
        grid=(B, T // tm),
        in_specs=in_specs,
        out_specs=pl.BlockSpec((1, tm, D), lambda b, i: (b, i, 0)),
        out_shape=jax.ShapeDtypeStruct((B, T, D), F32),
        scratch_shapes=[
            pltpu.VMEM((N_SLABS, tm, LANES), F32),
            pltpu.VMEM((tm, LANES), F32),
            pltpu.VMEM((tm, LANES), F32),
            pltpu.VMEM((N_SLABS, tm, LANES), F32),
            pltpu.VMEM((tm, D), BF16),
        ],
        compiler_params=_cparams("arbitrary", "arbitrary"),
        name="merge_out",
    )(x, *os_, *lses, wo)


def _attn_sample_kernel(q_ref, kvn_ref, c0_ref, c1_ref, c2_ref, bias_ref, o_ref):
    caches = (c0_ref, c1_ref, c2_ref)
    sc_c, sc_n = [], []
    for g in range(len(BRANCHES)):
        q = q_ref[0, g]
        kc = caches[g][0, :, 0]
        sc_c.append(jnp.sum(kc * q[None], axis=-1, keepdims=True) + bias_ref[g, 0:N_KEYS - 1])
        sc_n.append(jnp.sum(kvn_ref[0, g, 0] * q, axis=-1, keepdims=True) + bias_ref[g, N_KEYS - 1])
    m = sc_n[0]
    for g in range(len(BRANCHES)):
        m = jnp.maximum(m, jnp.maximum(jnp.max(sc_c[g], axis=0), sc_n[g]))
    num = jnp.zeros((HEADS, HEAD_DIM), F32)
    den = jnp.zeros((HEADS, HEAD_DIM), F32)
    for g in range(len(BRANCHES)):
        p_c = jnp.exp(sc_c[g] - m[None])
        p_n = jnp.exp(sc_n[g] - m)
        den = den + jnp.sum(p_c, axis=0) + p_n
        num = num + jnp.sum(p_c * caches[g][0, :, 1], axis=0) + p_n * kvn_ref[0, g, 1]
    o_ref[0] = num / den


def _attn_sample(q, kvn, caches, bias):
    Bs = q.shape[0]
    in_specs = [
        pl.BlockSpec((1,) + q.shape[1:], lambda b: (b, 0, 0, 0)),
        pl.BlockSpec((1,) + kvn.shape[1:], lambda b: (b, 0, 0, 0, 0)),
    ]
    for c in caches:
        in_specs.append(pl.BlockSpec((1, N_KEYS - 1, None, 2, HEADS, HEAD_DIM),
                                     lambda b: (b, 0, 0, 0, 0, 0)))
    in_specs.append(pl.BlockSpec(bias.shape, lambda b: (0, 0, 0, 0)))
    return pl.pallas_call(
        _attn_sample_kernel,
        grid=(Bs,),
        in_specs=in_specs,
        out_specs=pl.BlockSpec((1, HEADS, HEAD_DIM), lambda b: (b, 0, 0)),
        out_shape=jax.ShapeDtypeStruct((Bs, HEADS, HEAD_DIM), F32),
        compiler_params=_cparams("arbitrary"),
        name="attn_sample",
    )(q, kvn, *caches, bias)


def _out_sample_kernel(x_ref, o_ref, wo_ref, out_ref):
    out_ref[...] = x_ref[...] + jnp.dot(o_ref[...].astype(BF16), wo_ref[...],
                                        preferred_element_type=F32)


def _out_sample(x, o, wo):
    return pl.pallas_call(
        _out_sample_kernel,
        out_shape=jax.ShapeDtypeStruct(x.shape, F32),
        compiler_params=pltpu.CompilerParams(vmem_limit_bytes=VMEM_LIMIT),
        name="out_sample",
    )(x, o, wo)


def _t5_bucket(dist):
    max_exact = NUM_BUCKETS // 2
    df = jnp.maximum(dist, 1).astype(F32)
    large = max_exact + (jnp.log(df / max_exact) / math.log(MAX_DISTANCE / max_exact)
                         * (NUM_BUCKETS - max_exact)).astype(jnp.int32)
    large = jnp.minimum(large, NUM_BUCKETS - 1)
    return jnp.where(dist < max_exact, dist, large)


def _bias_tables(rel_bias):
    out = []
    for g, (w, d) in enumerate(BRANCHES):
        dist = jnp.arange(N_KEYS, dtype=jnp.int32) * d
        out.append(rel_bias[_t5_bucket(dist)][:, g * HEADS:(g + 1) * HEADS])
    return out


def _block_bias(tab):
    pad_lo = jnp.full((HEADS, Q_SUB - 1), MASKED, F32)
    pad_hi = jnp.full((HEADS, Q_SUB), MASKED, F32)
    f = jnp.concatenate([pad_lo, tab[::-1].T, pad_hi], axis=1)
    width = f.shape[1]
    flat = jnp.tile(f, (1, Q_SUB))[:, :Q_SUB * (width - 1)]
    toep = flat.reshape(HEADS, Q_SUB, width - 1)
    return toep[:, :, Q_SUB - 1:Q_SUB - 1 + 2 * Q_SUB]


def kernel(x_prompt, x_sample, state_pool, cache_kv_w128, cache_kv_w512, cache_kv_w2048,
           norm_mix, pool_w, pool_scale, norm_mlp, mlp_in, mlp_out, norm_kv, w_kv, w_q, w_o,
           rel_bias, norm_final):
    B, T, D = x_prompt.shape
    Bs = x_sample.shape[0]
    n_a = pool_w.shape[0]
    depth = norm_mix.shape[0]
    n_br = len(BRANCHES)
    qk_scale = HEAD_DIM ** -0.5

    pool_w_b = pool_w.astype(BF16)
    mlp_in_b = mlp_in.astype(BF16)
    mlp_out_b = mlp_out.astype(BF16)
    w_kv_b = w_kv.astype(BF16)[None]
    w_q_b = w_q.astype(BF16)
    w_o_b = w_o.astype(BF16)
    row = lambda v: v.reshape(1, D)
    g_mix = norm_mix.reshape(depth, 1, D)
    g_mlp = norm_mlp.reshape(depth, 1, D)
    g_kv = norm_kv.reshape(1, 1, D)

    tabs = _bias_tables(rel_bias)
    blk_bias = [_block_bias(t) for t in tabs]
    smp_bias = jnp.stack([jnp.broadcast_to(t[::-1][:, :, None], (N_KEYS, HEADS, HEAD_DIM))
                          for t in tabs])

    caches = [c.reshape(Bs, N_KEYS - 1, d, 2, HEADS, HEAD_DIM)
              for c, (w, d) in zip((cache_kv_w128, cache_kv_w512, cache_kv_w2048), BRANCHES)]

    xp = x_prompt
    xs = x_sample.reshape(Bs, D)
    pool_p, pool_s, kv_p = [], [], []
    kvs = kvn = None
    for l in range(depth):
        last = l == depth - 1
        if l < n_a:
            xp, nbuf = _pool_layer(xp, row(norm_mix[l]), pool_w_b[l], row(pool_scale[l]))
            pool_p.append(nbuf)
            xs, hs = _pool_sample(xs, state_pool[l], row(norm_mix[l]), pool_w_b[l], row(pool_scale[l]))
            pool_s.append(jnp.concatenate([state_pool[l][:, 1:], hs[:, None, :]], axis=1))
        else:
            lb = l - n_a
            if l == n_a:
                kvs = []
                for g, (w, d) in enumerate(BRANCHES):
                    kv_g, tail = _proj(xp, g_kv, 0, w_kv_b, 0, g, 2 * D, d, tail_rows=min(w, T))
                    kvs.append(kv_g)
                    kv_p.append(tail.reshape(B, min(w, T), 2, HEADS, HEAD_DIM))
                kvn = _proj_sample(xs, g_kv, 0, w_kv_b, 0)
            os_, lses = [], []
            for g, (w, d) in enumerate(BRANCHES):
                (q_g,) = _proj(xp, g_mix, l, w_q_b, lb, g, D, d, scale=qk_scale)
                o_g, lse_g = _attn_branch(q_g, kvs[g], blk_bias[g])
                os_.append(o_g)
                lses.append(lse_g)
            xp = _merge_out(xp, os_, lses, w_o_b, lb)
            qs = _proj_sample(xs, g_mix, l, w_q_b, lb, scale=qk_scale)
            o_s = _attn_sample(qs.reshape(Bs, n_br, HEADS, HEAD_DIM),
                               kvn.reshape(Bs, n_br, 2, HEADS, HEAD_DIM), caches, smp_bias)
            xs = _out_sample(xs, o_s.reshape(Bs, D), w_o_b[lb])
        xp = _mlp(xp.reshape(B * T, D), g_mlp, mlp_in_b, mlp_out_b, row(norm_final), l, last,
                  tm=1024).reshape(B, T, D)
        xs = _mlp(xs, g_mlp, mlp_in_b, mlp_out_b, row(norm_final), l, last, tm=Bs)

    kv_s = [kvn[:, g * 2 * D:(g + 1) * 2 * D].reshape(Bs, 1, 2, HEADS, HEAD_DIM) for g in range(n_br)]
    return (xp, xs.reshape(Bs, 1, D), jnp.stack(pool_p), jnp.stack(pool_s),
            kv_p[0], kv_s[0], kv_p[1], kv_s[1], kv_p[2], kv_s[2])
```

```python
import functools
import math

import jax
import jax.numpy as jnp
import numpy as np
from jax import lax
from jax.experimental import pallas as pl
from jax.experimental.pallas import tpu as pltpu

F32 = jnp.float32
BF16 = jnp.bfloat16

D_MODEL = 1024
HEADS = 8
HEAD_DIM = 128
D_FF = 4 * D_MODEL
POOL_WINDOWS = (2, 4, 8, 16)
POOL_GROUP = D_MODEL // len(POOL_WINDOWS)
POOL_STATE = max(POOL_WINDOWS) - 1
BRANCHES = ((128, 1), (512, 4), (2048, 16))
N_KEYS = 129
NUM_BUCKETS = 32
MAX_DISTANCE = 2048
SAMPLE_POS0 = 8192
EPS = 1e-6
MASKED = -1e30

LANES = 128
N_SLABS = D_MODEL // LANES
Q_SUB = 128
VMEM_LIMIT = 56 * 1024 * 1024


def _cparams(*sem):
    return pltpu.CompilerParams(dimension_semantics=sem, vmem_limit_bytes=VMEM_LIMIT)


def _rms(x, g):
    ms = jnp.mean(x * x, axis=-1, keepdims=True)
    return x * lax.rsqrt(ms + EPS) * g


POOL_BASE = 24
POOL_LO = 8


def _pool_kernel(x_ref, g_ref, w_ref, sc_ref, o_ref, nb_ref, hbuf, sbuf, *, tm):
    t = pl.program_id(1)
    end = POOL_BASE + tm

    @pl.when(jnp.logical_and(pl.program_id(0) == 0, t == 0))
    def _():
        hbuf[0:POOL_BASE, :] = jnp.zeros((POOL_BASE, D_MODEL), F32)
        sbuf[...] = jnp.zeros(sbuf.shape, F32)

    @pl.when(t == 0)
    def _():
        hbuf[POOL_BASE - 16:POOL_BASE, :] = jnp.zeros((16, D_MODEL), F32)

    x = x_ref[0]
    h = _rms(x, g_ref[...])
    hbuf[POOL_BASE:end, :] = h
    pos = t * tm + lax.broadcasted_iota(jnp.int32, (tm, 1), 0)
    for gi, w in enumerate(POOL_WINDOWS):
        sl = slice(gi * POOL_GROUP, (gi + 1) * POOL_GROUP)
        n_stage = w.bit_length() - 1
        s = None
        for si in range(n_stage):
            sh = 1 << si
            lo = POOL_BASE if si == n_stage - 1 else POOL_LO
            if si == 0:
                val = hbuf[lo:end, sl] + hbuf[lo - sh:end - sh, sl]
            else:
                val = sbuf[si - 1, lo:end, :] + sbuf[si - 1, lo - sh:end - sh, :]
            if si == n_stage - 1:
                s = val
            else:
                sbuf[si, lo:end, :] = val
        hg = h[:, sl]
        cnt = jnp.minimum(pos + 1, w).astype(F32)
        pooled = s / cnt - hg
        y = jnp.dot(pooled.astype(BF16), w_ref[gi], preferred_element_type=F32)
        o_ref[0, :, sl] = x[:, sl] + y * sc_ref[:, sl]
    nb_ref[0] = hbuf[end - POOL_STATE:end, :]
    hbuf[POOL_BASE - 16:POOL_BASE, :] = hbuf[end - 16:end, :]


def _pool_layer(x, gain, w_pool, scale, tm=512):
    B, T, D = x.shape
    return pl.pallas_call(
        functools.partial(_pool_kernel, tm=tm),
        grid=(B, T // tm),
        in_specs=[
            pl.BlockSpec((1, tm, D), lambda b, t: (b, t, 0)),
            pl.BlockSpec((1, D), lambda b, t: (0, 0)),
            pl.BlockSpec(w_pool.shape, lambda b, t: (0, 0, 0)),
            pl.BlockSpec((1, D), lambda b, t: (0, 0)),
        ],
        out_specs=[
            pl.BlockSpec((1, tm, D), lambda b, t: (b, t, 0)),
            pl.BlockSpec((1, POOL_STATE, D), lambda b, t: (b, 0, 0)),
        ],
        out_shape=[
            jax.ShapeDtypeStruct((B, T, D), F32),
            jax.ShapeDtypeStruct((B, POOL_STATE, D), F32),
        ],
        scratch_shapes=[pltpu.VMEM((POOL_BASE + tm, D), F32),
                        pltpu.VMEM((len(POOL_WINDOWS) - 1, POOL_BASE + tm, POOL_GROUP), F32)],
        compiler_params=_cparams("arbitrary", "arbitrary"),
        name="pool_layer",
    )(x, gain, w_pool, scale)


def _pool_sample_kernel(x_ref, st_ref, g_ref, w_ref, sc_ref, o_ref, h_ref):
    x = x_ref[...]
    h = _rms(x, g_ref[...])
    h_ref[...] = h
    for gi, w in enumerate(POOL_WINDOWS):
        sl = slice(gi * POOL_GROUP, (gi + 1) * POOL_GROUP)
        hg = h[:, sl]
        s = hg
        for j in range(1, w):
            s = s + st_ref[:, POOL_STATE - j, sl]
        pooled = s / float(min(SAMPLE_POS0 + 1, w)) - hg
        y = jnp.dot(pooled.astype(BF16), w_ref[gi], preferred_element_type=F32)
        o_ref[:, sl] = x[:, sl] + y * sc_ref[:, sl]


def _pool_sample(x, state, gain, w_pool, scale):
    Bs, D = x.shape
    return pl.pallas_call(
        _pool_sample_kernel,
        out_shape=[jax.ShapeDtypeStruct((Bs, D), F32), jax.ShapeDtypeStruct((Bs, D), F32)],
        compiler_params=pltpu.CompilerParams(vmem_limit_bytes=VMEM_LIMIT),
        name="pool_sample",
    )(x, state, gain, w_pool, scale)


def _mlp_kernel(x_ref, g_ref, w1_ref, w2_ref, gf_ref, o_ref, hs, us, *, tc, final_norm):
    x = x_ref[...]
    hs[...] = _rms(x, g_ref[...]).astype(BF16)
    for c in range(D_FF // tc):
        u = jnp.dot(hs[...], w1_ref[:, c * tc:(c + 1) * tc], preferred_element_type=F32)
        us[:, c * tc:(c + 1) * tc] = jnp.square(jnp.maximum(u, 0.0)).astype(BF16)
    y = x + jnp.dot(us[...], w2_ref[...], preferred_element_type=F32)
    if final_norm:
        y = _rms(y, gf_ref[...])
    o_ref[...] = y


def _mlp(x, gains, w1, w2, gain_final, layer, final_norm, tm, tc=512):
    M, D = x.shape
    return pl.pallas_call(
        functools.partial(_mlp_kernel, tc=tc, final_norm=final_norm),
        grid=(M // tm,),
        in_specs=[
            pl.BlockSpec((tm, D), lambda i: (i, 0)),
            pl.BlockSpec((None, 1, D), lambda i: (layer, 0, 0)),
            pl.BlockSpec((None, D, D_FF), lambda i: (layer, 0, 0), pipeline_mode=pl.Buffered(1)),
            pl.BlockSpec((None, D_FF, D), lambda i: (layer, 0, 0), pipeline_mode=pl.Buffered(1)),
            pl.BlockSpec((1, D), lambda i: (0, 0)),
        ],
        out_specs=pl.BlockSpec((tm, D), lambda i: (i, 0)),
        out_shape=jax.ShapeDtypeStruct((M, D), F32),
        scratch_shapes=[pltpu.VMEM((tm, D), BF16), pltpu.VMEM((tm, D_FF), BF16)],
        compiler_params=_cparams("arbitrary"),
        name="mlp",
    )(x, gains, w1, w2, gain_final)


def _proj_kernel(x_ref, g_ref, w_ref, *refs, tm, nc, scale, tails):
    n_br = len(BRANCHES)
    o_refs = refs[:n_br]
    if tails:
        tail_refs = refs[n_br:2 * n_br]
        hsl, tsl = refs[2 * n_br], refs[2 * n_br + 1]
        lhs = refs[2 * n_br + 2:]
        i = pl.program_id(1)
        nt = pl.num_programs(1)
    else:
        hsl = refs[n_br]
        lhs = refs[n_br + 1:]
    h = _rms(x_ref[0], g_ref[...])
    for c in range(N_SLABS):
        hsl[c] = h[:, c * LANES:(c + 1) * LANES]
    for g, (_, dil) in enumerate(BRANCHES):
        n = tm // dil
        if dil == 1:
            lhs[g][...] = h.astype(BF16)
        else:
            for r in range(dil):
                for c in range(N_SLABS):
                    lhs[g][r * n:(r + 1) * n, c * LANES:(c + 1) * LANES] = (
                        hsl[c, pl.ds(r, n, stride=dil), :].astype(BF16))
        res = jnp.dot(lhs[g][...], w_ref[:, g * nc:(g + 1) * nc], preferred_element_type=F32)
        if scale != 1.0:
            res = res * scale
        for r in range(dil):
            o_refs[g][0, r] = res[r * n:(r + 1) * n].astype(o_refs[g].dtype)

        if tails:
            tail_ref, rows = tail_refs[g], tails[g]
            if rows < tm:
                @pl.when(i == nt - 1)
                def _(tail_ref=tail_ref, rows=rows, res=res):
                    tail_ref[0] = res[tm - rows:tm]
            else:
                @pl.when(i >= nt - rows // tm)
                def _(tail_ref=tail_ref, res=res, dil=dil, n=n):
                    if dil == 1:
                        tail_ref[0] = res
                    else:
                        for c in range(nc // LANES):
                            for r in range(dil):
                                tsl[c, pl.ds(r, n, stride=dil), :] = (
                                    res[r * n:(r + 1) * n, c * LANES:(c + 1) * LANES])
                        for c in range(nc // LANES):
                            tail_ref[0, :, c * LANES:(c + 1) * LANES] = tsl[c]


def _proj(x, gains, gi, w, wi, nc, tm, scale=1.0, tails=()):
    B, T, D = x.shape
    nt = T // tm
    out_specs, out_shape = [], []
    for (_, dil) in BRANCHES:
        out_specs.append(pl.BlockSpec((1, dil, tm // dil, nc), lambda b, i: (b, 0, i, 0)))
        out_shape.append(jax.ShapeDtypeStruct((B, dil, T // dil, nc), BF16))
    scratch = [pltpu.VMEM((N_SLABS, tm, LANES), F32)]
    if tails:
        for rows, (_, dil) in zip(tails, BRANCHES):
            assert rows >= tm or dil == 1
            if rows < tm:
                out_specs.append(pl.BlockSpec((1, rows, nc), lambda b, i: (b, 0, 0)))
            else:
                first = nt - rows // tm
                out_specs.append(pl.BlockSpec(
                    (1, tm, nc), lambda b, i, first=first: (b, jnp.maximum(i - first, 0), 0)))
            out_shape.append(jax.ShapeDtypeStruct((B, rows, nc), F32))
        scratch.append(pltpu.VMEM((nc // LANES, tm, LANES), F32))
    scratch += [pltpu.VMEM((tm, D), BF16) for _ in BRANCHES]
    return pl.pallas_call(
        functools.partial(_proj_kernel, tm=tm, nc=nc, scale=scale, tails=tuple(tails)),
        grid=(B, nt),
        in_specs=[
            pl.BlockSpec((1, tm, D), lambda b, i: (b, i, 0)),
            pl.BlockSpec((None, 1, D), lambda b, i: (gi, 0, 0)),
            pl.BlockSpec((None, D, len(BRANCHES) * nc), lambda b, i: (wi, 0, 0),
                         pipeline_mode=pl.Buffered(1)),
        ],
        out_specs=out_specs,
        out_shape=out_shape,
        scratch_shapes=scratch,
        compiler_params=_cparams("arbitrary", "arbitrary"),
        name="proj_kv" if tails else "proj_q",
    )(x, gains, w)


def _proj_sample_kernel(x_ref, g_ref, w_ref, o_ref, *, scale):
    h = _rms(x_ref[...], g_ref[...]).astype(BF16)
    res = jnp.dot(h, w_ref[...], preferred_element_type=F32)
    if scale != 1.0:
        res = res * scale
    o_ref[...] = res


def _proj_sample(x, gains, gi, w, wi, scale=1.0, tn=1024):
    Bs, D = x.shape
    nc = w.shape[2]
    return pl.pallas_call(
        functools.partial(_proj_sample_kernel, scale=scale),
        grid=(nc // tn,),
        in_specs=[
            pl.BlockSpec((Bs, D), lambda j: (0, 0)),
            pl.BlockSpec((None, 1, D), lambda j: (gi, 0, 0)),
            pl.BlockSpec((None, D, tn), lambda j: (wi, 0, j)),
        ],
        out_specs=pl.BlockSpec((Bs, tn), lambda j: (0, j)),
        out_shape=jax.ShapeDtypeStruct((Bs, nc), F32),
        compiler_params=_cparams("arbitrary"),
        name="proj_sample",
    )(x, gains, w)


def _attn_kernel(q_ref, kv_ref, bias_ref, o_ref, lse_ref, kc, vc, *, tq):
    first = pl.program_id(2) == 0

    @pl.when(first)
    def _():
        kc[0:Q_SUB, :] = jnp.zeros((Q_SUB, D_MODEL), BF16)
        vc[0:Q_SUB, :] = jnp.zeros((Q_SUB, D_MODEL), BF16)

    kc[Q_SUB:Q_SUB + tq, :] = kv_ref[0, 0, :, 0:D_MODEL]
    vc[Q_SUB:Q_SUB + tq, :] = kv_ref[0, 0, :, D_MODEL:2 * D_MODEL]
    col = lax.broadcasted_iota(jnp.int32, (Q_SUB, 2 * Q_SUB), 1)
    lane = lax.broadcasted_iota(jnp.int32, (Q_SUB, LANES), 1)
    no_prev = jnp.logical_and(first, col < Q_SUB)
    for s in range(tq // Q_SUB):
        r0 = s * Q_SUB
        lse_tile = jnp.zeros((Q_SUB, LANES), F32)
        for h in range(HEADS):
            cs = slice(h * HEAD_DIM, (h + 1) * HEAD_DIM)
            q = q_ref[0, 0, r0:r0 + Q_SUB, cs]
            k = kc[r0:r0 + 2 * Q_SUB, cs]
            sc = lax.dot_general(q, k, (((1,), (1,)), ((), ())), preferred_element_type=F32)
            sc = sc + bias_ref[h]
            if s == 0:
                sc = jnp.where(no_prev, MASKED, sc)
            m = jnp.max(sc, axis=-1, keepdims=True)
            p = jnp.exp(sc - m)
            l = jnp.sum(p, axis=-1, keepdims=True)
            v = vc[r0:r0 + 2 * Q_SUB, cs]
            o = jnp.dot(p.astype(BF16), v, preferred_element_type=F32)
            o_ref[0, 0, r0:r0 + Q_SUB, cs] = (o / l).astype(o_ref.dtype)
            lse_tile = jnp.where(lane == h, m + jnp.log(l), lse_tile)
        lse_ref[0, 0, r0:r0 + Q_SUB, :] = lse_tile
    kc[0:Q_SUB, :] = kc[tq:tq + Q_SUB, :]
    vc[0:Q_SUB, :] = vc[tq:tq + Q_SUB, :]


def _attn_branch(q, kv, bias, tq=256):
    B, dil, L, D = q.shape
    tq = min(tq, L)
    return pl.pallas_call(
        functools.partial(_attn_kernel, tq=tq),
        grid=(B, dil, L // tq),
        in_specs=[
            pl.BlockSpec((1, 1, tq, D), lambda b, r, i: (b, r, i, 0)),
            pl.BlockSpec((1, 1, tq, 2 * D), lambda b, r, i: (b, r, i, 0)),
            pl.BlockSpec(bias.shape, lambda b, r, i: (0, 0, 0)),
        ],
        out_specs=[
            pl.BlockSpec((1, 1, tq, D), lambda b, r, i: (b, r, i, 0)),
            pl.BlockSpec((1, 1, tq, LANES), lambda b, r, i: (b, r, i, 0)),
        ],
        out_shape=[
            jax.ShapeDtypeStruct((B, dil, L, D), BF16),
            jax.ShapeDtypeStruct((B, dil, L, LANES), F32),
        ],
        scratch_shapes=[pltpu.VMEM((Q_SUB + tq, D), BF16), pltpu.VMEM((Q_SUB + tq, D), BF16)],
        compiler_params=_cparams("arbitrary", "arbitrary", "arbitrary"),
        name=f"attn_d{dil}",
    )(q, kv, bias)


def _merge_kernel(x_ref, o0_ref, o1_ref, o2_ref, l0_ref, l1_ref, l2_ref, ex_ref, wo_ref, out_ref,
                  osl1, osl2, lsl1, lsl2, lhs, *, tm):
    o_refs = (o0_ref, o1_ref, o2_ref)
    l_refs = (l0_ref, l1_ref, l2_ref)
    lsls = (None, lsl1, lsl2)
    osls = (None, osl1, osl2)
    lses = []
    for g, (_, dil) in enumerate(BRANCHES):
        if dil == 1:
            lses.append(l_refs[g][0, 0])
        else:
            n = tm // dil
            for r in range(dil):
                lsls[g][pl.ds(r, n, stride=dil), :] = l_refs[g][0, r]
                for c in range(N_SLABS):
                    osls[g][c, pl.ds(r, n, stride=dil), :] = (
                        o_refs[g][0, r, :, c * LANES:(c + 1) * LANES].astype(F32))
            lses.append(lsls[g][...])
    m = jnp.maximum(jnp.maximum(lses[0], lses[1]), lses[2])
    es = [jnp.exp(l - m) for l in lses]
    den = es[0] + es[1] + es[2]

    def split_bf16(w):
        hi = w.astype(BF16)
        lo = (w - hi.astype(F32)).astype(BF16)
        return jnp.concatenate([hi, lo], axis=1)

    w1s = split_bf16(es[1] / den)
    w2s = split_bf16(es[2] / den)
    for cc in range(N_SLABS // 2):
        wide = slice(2 * cc * LANES, (2 * cc + 2) * LANES)
        w1 = jnp.dot(w1s, ex_ref[:, wide], preferred_element_type=F32)
        w2 = jnp.dot(w2s, ex_ref[:, wide], preferred_element_type=F32)
        for k in range(2):
            c = 2 * cc + k
            cs = slice(c * LANES, (c + 1) * LANES)
            ks = slice(k * LANES, (k + 1) * LANES)
            o0 = o0_ref[0, 0, :, cs].astype(F32)
            merged = o0 + w1[:, ks] * (osl1[c] - o0) + w2[:, ks] * (osl2[c] - o0)
            lhs[:, cs] = merged.astype(BF16)
    out_ref[0] = x_ref[0] + jnp.dot(lhs[...], wo_ref[...], preferred_element_type=F32)


def _merge_out(x, os_, lses, wo, wi, tm=512):
    B, T, D = x.shape
    expand = np.zeros((2 * LANES, D), np.float32)
    for h in range(HEADS):
        expand[h, h * HEAD_DIM:(h + 1) * HEAD_DIM] = 1.0
        expand[LANES + h, h * HEAD_DIM:(h + 1) * HEAD_DIM] = 1.0
    in_specs = [pl.BlockSpec((1, tm, D), lambda b, i: (b, i, 0))]
    for (_, dil) in BRANCHES:
        in_specs.append(pl.BlockSpec((1, dil, tm // dil, D), lambda b, i: (b, 0, i, 0)))
    for (_, dil) in BRANCHES:
        in_specs.append(pl.BlockSpec((1, dil, tm // dil, LANES), lambda b, i: (b, 0, i, 0)))
    in_specs.append(pl.BlockSpec((2 * LANES, D), lambda b, i: (0, 0)))
    in_specs.append(pl.BlockSpec((None, D, D), lambda b, i: (wi, 0, 0)))
    return pl.pallas_call(
        functools.partial(_merge_kernel, tm=tm),
        grid=(B, T // tm),
        in_specs=in_specs,
        out_specs=pl.BlockSpec((1, tm, D), lambda b, i: (b, i, 0)),
        out_shape=jax.ShapeDtypeStruct((B, T, D), F32),
        scratch_shapes=[
            pltpu.VMEM((N_SLABS, tm, LANES), F32),
            pltpu.VMEM((N_SLABS, tm, LANES), F32),
            pltpu.VMEM((tm, LANES), F32),
            pltpu.VMEM((tm, LANES), F32),
            pltpu.VMEM((tm, D), BF16),
        ],
        compiler_params=_cparams("arbitrary", "arbitrary"),
        name="merge_out",
    )(x, *os_, *lses, jnp.asarray(expand, BF16), wo)


def _attn_sample_kernel(q_ref, kvn_ref, c0_ref, c1_ref, c2_ref, bias_ref, o_ref):
    caches = (c0_ref, c1_ref, c2_ref)
    sc_c, sc_n = [], []
    for g in range(len(BRANCHES)):
        q = q_ref[0, g]
        kc = caches[g][0, :, 0]
        sc_c.append(jnp.sum(kc * q[None], axis=-1, keepdims=True) + bias_ref[g, 0:N_KEYS - 1])
        sc_n.append(jnp.sum(kvn_ref[0, g, 0] * q, axis=-1, keepdims=True) + bias_ref[g, N_KEYS - 1])
    m = sc_n[0]
    for g in range(len(BRANCHES)):
        m = jnp.maximum(m, jnp.maximum(jnp.max(sc_c[g], axis=0), sc_n[g]))
    num = jnp.zeros((HEADS, HEAD_DIM), F32)
    den = jnp.zeros((HEADS, HEAD_DIM), F32)
    for g in range(len(BRANCHES)):
        p_c = jnp.exp(sc_c[g] - m[None])
        p_n = jnp.exp(sc_n[g] - m)
        den = den + jnp.sum(p_c, axis=0) + p_n
        num = num + jnp.sum(p_c * caches[g][0, :, 1], axis=0) + p_n * kvn_ref[0, g, 1]
    o_ref[0] = num / den


def _attn_sample(q, kvn, caches, bias):
    Bs = q.shape[0]
    in_specs = [
        pl.BlockSpec((1,) + q.shape[1:], lambda b: (b, 0, 0, 0)),
        pl.BlockSpec((1,) + kvn.shape[1:], lambda b: (b, 0, 0, 0, 0)),
    ]
    for c in caches:
        in_specs.append(pl.BlockSpec((1, N_KEYS - 1, None, 2, HEADS, HEAD_DIM),
                                     lambda b: (b, 0, 0, 0, 0, 0)))
    in_specs.append(pl.BlockSpec(bias.shape, lambda b: (0, 0, 0, 0)))
    return pl.pallas_call(
        _attn_sample_kernel,
        grid=(Bs,),
        in_specs=in_specs,
        out_specs=pl.BlockSpec((1, HEADS, HEAD_DIM), lambda b: (b, 0, 0)),
        out_shape=jax.ShapeDtypeStruct((Bs, HEADS, HEAD_DIM), F32),
        compiler_params=_cparams("arbitrary"),
        name="attn_sample",
    )(q, kvn, *caches, bias)


def _out_sample_kernel(x_ref, o_ref, wo_ref, out_ref):
    out_ref[...] = x_ref[...] + jnp.dot(o_ref[...].astype(BF16), wo_ref[...],
                                        preferred_element_type=F32)


def _out_sample(x, o, wo):
    return pl.pallas_call(
        _out_sample_kernel,
        out_shape=jax.ShapeDtypeStruct(x.shape, F32),
        compiler_params=pltpu.CompilerParams(vmem_limit_bytes=VMEM_LIMIT),
        name="out_sample",
    )(x, o, wo)


def _t5_bucket(dist):
    max_exact = NUM_BUCKETS // 2
    df = jnp.maximum(dist, 1).astype(F32)
    large = max_exact + (jnp.log(df / max_exact) / math.log(MAX_DISTANCE / max_exact)
                         * (NUM_BUCKETS - max_exact)).astype(jnp.int32)
    large = jnp.minimum(large, NUM_BUCKETS - 1)
    return jnp.where(dist < max_exact, dist, large)


def _bias_tables(rel_bias):
    out = []
    for g, (w, d) in enumerate(BRANCHES):
        dist = jnp.arange(N_KEYS, dtype=jnp.int32) * d
        out.append(rel_bias[_t5_bucket(dist)][:, g * HEADS:(g + 1) * HEADS])
    return out


def _block_bias(tab):
    pad_lo = jnp.full((HEADS, Q_SUB - 1), MASKED, F32)
    pad_hi = jnp.full((HEADS, Q_SUB), MASKED, F32)
    f = jnp.concatenate([pad_lo, tab[::-1].T, pad_hi], axis=1)
    width = f.shape[1]
    flat = jnp.tile(f, (1, Q_SUB))[:, :Q_SUB * (width - 1)]
    toep = flat.reshape(HEADS, Q_SUB, width - 1)
    return toep[:, :, Q_SUB - 1:Q_SUB - 1 + 2 * Q_SUB]


def kernel(x_prompt, x_sample, state_pool, cache_kv_w128, cache_kv_w512, cache_kv_w2048,
           norm_mix, pool_w, pool_scale, norm_mlp, mlp_in, mlp_out, norm_kv, w_kv, w_q, w_o,
           rel_bias, norm_final):
    B, T, D = x_prompt.shape
    Bs = x_sample.shape[0]
    n_a = pool_w.shape[0]
    depth = norm_mix.shape[0]
    n_br = len(BRANCHES)
    qk_scale = HEAD_DIM ** -0.5

    pool_w_b = pool_w.astype(BF16)
    mlp_in_b = mlp_in.astype(BF16)
    mlp_out_b = mlp_out.astype(BF16)
    w_kv_b = w_kv.astype(BF16)[None]
    w_q_b = w_q.astype(BF16)
    w_o_b = w_o.astype(BF16)
    row = lambda v: v.reshape(1, D)
    g_mix = norm_mix.reshape(depth, 1, D)
    g_mlp = norm_mlp.reshape(depth, 1, D)
    g_kv = norm_kv.reshape(1, 1, D)

    tabs = _bias_tables(rel_bias)
    blk_bias = [_block_bias(t) for t in tabs]
    smp_bias = jnp.stack([jnp.broadcast_to(t[::-1][:, :, None], (N_KEYS, HEADS, HEAD_DIM))
                          for t in tabs])

    caches = [c.reshape(Bs, N_KEYS - 1, d, 2, HEADS, HEAD_DIM)
              for c, (w, d) in zip((cache_kv_w128, cache_kv_w512, cache_kv_w2048), BRANCHES)]

    xp = x_prompt
    xs = x_sample.reshape(Bs, D)
    pool_p, pool_s, kv_p = [], [], []
    kvs = kvn = None
    for l in range(depth):
        last = l == depth - 1
        if l < n_a:
            xp, nbuf = _pool_layer(xp, row(norm_mix[l]), pool_w_b[l], row(pool_scale[l]))
            pool_p.append(nbuf)
            xs, hs = _pool_sample(xs, state_pool[l], row(norm_mix[l]), pool_w_b[l], row(pool_scale[l]))
            pool_s.append(jnp.concatenate([state_pool[l][:, 1:], hs[:, None, :]], axis=1))
        else:
            lb = l - n_a
            if l == n_a:
                tails = tuple(min(w, T) for (w, _) in BRANCHES)
                res = _proj(xp, g_kv, 0, w_kv_b, 0, 2 * D, 256, tails=tails)
                kvs = res[:n_br]
                kv_p = [t.reshape(B, rows, 2, HEADS, HEAD_DIM) for t, rows in zip(res[n_br:], tails)]
                kvn = _proj_sample(xs, g_kv, 0, w_kv_b, 0)
            qs_p = _proj(xp, g_mix, l, w_q_b, lb, D, 512, scale=qk_scale)
            os_, lses = [], []
            for g in range(n_br):
                o_g, lse_g = _attn_branch(qs_p[g], kvs[g], blk_bias[g])
                os_.append(o_g)
                lses.append(lse_g)
            xp = _merge_out(xp, os_, lses, w_o_b, lb)
            qs = _proj_sample(xs, g_mix, l, w_q_b, lb, scale=qk_scale)
            o_s = _attn_sample(qs.reshape(Bs, n_br, HEADS, HEAD_DIM),
                               kvn.reshape(Bs, n_br, 2, HEADS, HEAD_DIM), caches, smp_bias)
            xs = _out_sample(xs, o_s.reshape(Bs, D), w_o_b[lb])
        xp = _mlp(xp.reshape(B * T, D), g_mlp, mlp_in_b, mlp_out_b, row(norm_final), l, last,
                  tm=1024).reshape(B, T, D)
        xs = _mlp(xs, g_mlp, mlp_in_b, mlp_out_b, row(norm_final), l, last, tm=Bs)

    kv_s = [kvn[:, g * 2 * D:(g + 1) * 2 * D].reshape(Bs, 1, 2, HEADS, HEAD_DIM) for g in range(n_br)]
    return (xp, xs.reshape(Bs, 1, D), jnp.stack(pool_p), jnp.stack(pool_s),
            kv_p[0], kv_s[0], kv_p[1], kv_s[1], kv_p[2], kv_s[2])
```

```python
import functools
import math

import jax
import jax.numpy as jnp
import numpy as np
from jax import lax
from jax.experimental import pallas as pl
from jax.experimental.pallas import tpu as pltpu

F32 = jnp.float32
BF16 = jnp.bfloat16

D_MODEL = 1024
HEADS = 8
HEAD_DIM = 128
D_FF = 4 * D_MODEL
POOL_WINDOWS = (2, 4, 8, 16)
POOL_GROUP = D_MODEL // len(POOL_WINDOWS)
POOL_STATE = max(POOL_WINDOWS) - 1
BRANCHES = ((128, 1), (512, 4), (2048, 16))
N_KEYS = 129
NUM_BUCKETS = 32
MAX_DISTANCE = 2048
SAMPLE_POS0 = 8192
EPS = 1e-6
MASKED = -1e30
LOG2E = math.log2(math.e)

LANES = 128
N_SLABS = D_MODEL // LANES
Q_SUB = 128
VMEM_LIMIT = 56 * 1024 * 1024


def _cparams(*sem):
    return pltpu.CompilerParams(dimension_semantics=sem, vmem_limit_bytes=VMEM_LIMIT)


def _rms(x, g):
    ms = jnp.mean(x * x, axis=-1, keepdims=True)
    return x * lax.rsqrt(ms + EPS) * g


POOL_BASE = 24
POOL_LO = 8


def _pool_kernel(x_ref, g_ref, w_ref, sc_ref, o_ref, nb_ref, hbuf, sbuf, *, tm):
    t = pl.program_id(1)
    end = POOL_BASE + tm

    @pl.when(jnp.logical_and(pl.program_id(0) == 0, t == 0))
    def _():
        hbuf[0:POOL_BASE, :] = jnp.zeros((POOL_BASE, D_MODEL), F32)
        sbuf[...] = jnp.zeros(sbuf.shape, F32)

    @pl.when(t == 0)
    def _():
        hbuf[POOL_BASE - 16:POOL_BASE, :] = jnp.zeros((16, D_MODEL), F32)

    x = x_ref[0]
    h = _rms(x, g_ref[...])
    hbuf[POOL_BASE:end, :] = h
    pos = t * tm + lax.broadcasted_iota(jnp.int32, (tm, 1), 0)
    for gi, w in enumerate(POOL_WINDOWS):
        sl = slice(gi * POOL_GROUP, (gi + 1) * POOL_GROUP)
        n_stage = w.bit_length() - 1
        s = None
        for si in range(n_stage):
            sh = 1 << si
            lo = POOL_BASE if si == n_stage - 1 else POOL_LO
            if si == 0:
                val = hbuf[lo:end, sl] + hbuf[lo - sh:end - sh, sl]
            else:
                val = sbuf[si - 1, lo:end, :] + sbuf[si - 1, lo - sh:end - sh, :]
            if si == n_stage - 1:
                s = val
            else:
                sbuf[si, lo:end, :] = val
        hg = h[:, sl]
        cnt = jnp.minimum(pos + 1, w).astype(F32)
        pooled = s / cnt - hg
        y = jnp.dot(pooled.astype(BF16), w_ref[gi], preferred_element_type=F32)
        o_ref[0, :, sl] = x[:, sl] + y * sc_ref[:, sl]
    nb_ref[0] = hbuf[end - POOL_STATE:end, :]
    hbuf[POOL_BASE - 16:POOL_BASE, :] = hbuf[end - 16:end, :]


def _pool_layer(x, gain, w_pool, scale, tm=1024):
    B, T, D = x.shape
    return pl.pallas_call(
        functools.partial(_pool_kernel, tm=tm),
        grid=(B, T // tm),
        in_specs=[
            pl.BlockSpec((1, tm, D), lambda b, t: (b, t, 0)),
            pl.BlockSpec((1, D), lambda b, t: (0, 0)),
            pl.BlockSpec(w_pool.shape, lambda b, t: (0, 0, 0)),
            pl.BlockSpec((1, D), lambda b, t: (0, 0)),
        ],
        out_specs=[
            pl.BlockSpec((1, tm, D), lambda b, t: (b, t, 0)),
            pl.BlockSpec((1, POOL_STATE, D), lambda b, t: (b, 0, 0)),
        ],
        out_shape=[
            jax.ShapeDtypeStruct((B, T, D), F32),
            jax.ShapeDtypeStruct((B, POOL_STATE, D), F32),
        ],
        scratch_shapes=[pltpu.VMEM((POOL_BASE + tm, D), F32),
                        pltpu.VMEM((len(POOL_WINDOWS) - 1, POOL_BASE + tm, POOL_GROUP), F32)],
        compiler_params=_cparams("arbitrary", "arbitrary"),
        name="pool_layer",
    )(x, gain, w_pool, scale)


def _pool_sample_kernel(x_ref, st_ref, g_ref, w_ref, sc_ref, o_ref, h_ref):
    x = x_ref[...]
    h = _rms(x, g_ref[...])
    h_ref[...] = h
    for gi, w in enumerate(POOL_WINDOWS):
        sl = slice(gi * POOL_GROUP, (gi + 1) * POOL_GROUP)
        hg = h[:, sl]
        s = hg
        for j in range(1, w):
            s = s + st_ref[:, POOL_STATE - j, sl]
        pooled = s / float(min(SAMPLE_POS0 + 1, w)) - hg
        y = jnp.dot(pooled.astype(BF16), w_ref[gi], preferred_element_type=F32)
        o_ref[:, sl] = x[:, sl] + y * sc_ref[:, sl]


def _pool_sample(x, state, gain, w_pool, scale):
    Bs, D = x.shape
    return pl.pallas_call(
        _pool_sample_kernel,
        out_shape=[jax.ShapeDtypeStruct((Bs, D), F32), jax.ShapeDtypeStruct((Bs, D), F32)],
        compiler_params=pltpu.CompilerParams(vmem_limit_bytes=VMEM_LIMIT),
        name="pool_sample",
    )(x, state, gain, w_pool, scale)


def _mlp_kernel(x_ref, g_ref, w1_ref, w2_ref, gf_ref, o_ref, hs, us, *, tc, final_norm):
    x = x_ref[...]
    hs[...] = _rms(x, g_ref[...]).astype(BF16)
    for c in range(D_FF // tc):
        u = jnp.dot(hs[...], w1_ref[:, c * tc:(c + 1) * tc], preferred_element_type=F32)
        us[:, c * tc:(c + 1) * tc] = jnp.square(jnp.maximum(u, 0.0)).astype(BF16)
    y = x + jnp.dot(us[...], w2_ref[...], preferred_element_type=F32)
    if final_norm:
        y = _rms(y, gf_ref[...])
    o_ref[...] = y


def _mlp(x, gains, w1, w2, gain_final, layer, final_norm, tm, tc=512):
    M, D = x.shape
    return pl.pallas_call(
        functools.partial(_mlp_kernel, tc=tc, final_norm=final_norm),
        grid=(M // tm,),
        in_specs=[
            pl.BlockSpec((tm, D), lambda i: (i, 0)),
            pl.BlockSpec((None, 1, D), lambda i: (layer, 0, 0)),
            pl.BlockSpec((None, D, D_FF), lambda i: (layer, 0, 0), pipeline_mode=pl.Buffered(1)),
            pl.BlockSpec((None, D_FF, D), lambda i: (layer, 0, 0), pipeline_mode=pl.Buffered(1)),
            pl.BlockSpec((1, D), lambda i: (0, 0)),
        ],
        out_specs=pl.BlockSpec((tm, D), lambda i: (i, 0)),
        out_shape=jax.ShapeDtypeStruct((M, D), F32),
        scratch_shapes=[pltpu.VMEM((tm, D), BF16), pltpu.VMEM((tm, D_FF), BF16)],
        compiler_params=_cparams("arbitrary"),
        name="mlp",
    )(x, gains, w1, w2, gain_final)


def _proj_kernel(x_ref, g_ref, w_ref, *refs, tm, nc, scale, tails):
    n_br = len(BRANCHES)
    o_refs = refs[:n_br]
    if tails:
        tail_refs = refs[n_br:2 * n_br]
        hsl, tsl = refs[2 * n_br], refs[2 * n_br + 1]
        lhs = refs[2 * n_br + 2:]
        i = pl.program_id(1)
        nt = pl.num_programs(1)
    else:
        hsl = refs[n_br]
        lhs = refs[n_br + 1:]
    h = _rms(x_ref[0], g_ref[...])
    for c in range(N_SLABS):
        hsl[c] = h[:, c * LANES:(c + 1) * LANES]
    for g, (_, dil) in enumerate(BRANCHES):
        n = tm // dil
        if dil == 1:
            lhs[g][...] = h.astype(BF16)
        else:
            for r in range(dil):
                for c in range(N_SLABS):
                    lhs[g][r * n:(r + 1) * n, c * LANES:(c + 1) * LANES] = (
                        hsl[c, pl.ds(r, n, stride=dil), :].astype(BF16))
        res = jnp.dot(lhs[g][...], w_ref[:, g * nc:(g + 1) * nc], preferred_element_type=F32)
        if scale != 1.0:
            res = res * scale
        for r in range(dil):
            o_refs[g][0, r] = res[r * n:(r + 1) * n].astype(o_refs[g].dtype)

        if tails:
            tail_ref, rows = tail_refs[g], tails[g]
            if rows < tm:
                @pl.when(i == nt - 1)
                def _(tail_ref=tail_ref, rows=rows, res=res):
                    tail_ref[0] = res[tm - rows:tm]
            else:
                @pl.when(i >= nt - rows // tm)
                def _(tail_ref=tail_ref, res=res, dil=dil, n=n):
                    if dil == 1:
                        tail_ref[0] = res
                    else:
                        for c in range(nc // LANES):
                            for r in range(dil):
                                tsl[c, pl.ds(r, n, stride=dil), :] = (
                                    res[r * n:(r + 1) * n, c * LANES:(c + 1) * LANES])
                        for c in range(nc // LANES):
                            tail_ref[0, :, c * LANES:(c + 1) * LANES] = tsl[c]


def _proj(x, gains, gi, w, wi, nc, tm, scale=1.0, tails=()):
    B, T, D = x.shape
    nt = T // tm
    out_specs, out_shape = [], []
    for (_, dil) in BRANCHES:
        out_specs.append(pl.BlockSpec((1, dil, tm // dil, nc), lambda b, i: (b, 0, i, 0)))
        out_shape.append(jax.ShapeDtypeStruct((B, dil, T // dil, nc), BF16))
    scratch = [pltpu.VMEM((N_SLABS, tm, LANES), F32)]
    if tails:
        for rows, (_, dil) in zip(tails, BRANCHES):
            assert rows >= tm or dil == 1
            if rows < tm:
                out_specs.append(pl.BlockSpec((1, rows, nc), lambda b, i: (b, 0, 0)))
            else:
                first = nt - rows // tm
                out_specs.append(pl.BlockSpec(
                    (1, tm, nc), lambda b, i, first=first: (b, jnp.maximum(i - first, 0), 0)))
            out_shape.append(jax.ShapeDtypeStruct((B, rows, nc), F32))
        scratch.append(pltpu.VMEM((nc // LANES, tm, LANES), F32))
    scratch += [pltpu.VMEM((tm, D), BF16) for _ in BRANCHES]
    return pl.pallas_call(
        functools.partial(_proj_kernel, tm=tm, nc=nc, scale=scale, tails=tuple(tails)),
        grid=(B, nt),
        in_specs=[
            pl.BlockSpec((1, tm, D), lambda b, i: (b, i, 0)),
            pl.BlockSpec((None, 1, D), lambda b, i: (gi, 0, 0)),
            pl.BlockSpec((None, D, len(BRANCHES) * nc), lambda b, i: (wi, 0, 0),
                         pipeline_mode=pl.Buffered(1)),
        ],
        out_specs=out_specs,
        out_shape=out_shape,
        scratch_shapes=scratch,
        compiler_params=_cparams("arbitrary", "arbitrary"),
        name="proj_kv" if tails else "proj_q",
    )(x, gains, w)


def _proj_sample_kernel(x_ref, g_ref, w_ref, o_ref, *, scale):
    h = _rms(x_ref[...], g_ref[...]).astype(BF16)
    res = jnp.dot(h, w_ref[...], preferred_element_type=F32)
    if scale != 1.0:
        res = res * scale
    o_ref[...] = res


def _proj_sample(x, gains, gi, w, wi, scale=1.0, tn=1024):
    Bs, D = x.shape
    nc = w.shape[2]
    return pl.pallas_call(
        functools.partial(_proj_sample_kernel, scale=scale),
        grid=(nc // tn,),
        in_specs=[
            pl.BlockSpec((Bs, D), lambda j: (0, 0)),
            pl.BlockSpec((None, 1, D), lambda j: (gi, 0, 0)),
            pl.BlockSpec((None, D, tn), lambda j: (wi, 0, j)),
        ],
        out_specs=pl.BlockSpec((Bs, tn), lambda j: (0, j)),
        out_shape=jax.ShapeDtypeStruct((Bs, nc), F32),
        compiler_params=_cparams("arbitrary"),
        name="proj_sample",
    )(x, gains, w)


def _attn_kernel(q_ref, kv_ref, bias_ref, o_ref, ml_ref, kprev, vprev, *, tq, rb):
    first = pl.program_id(2) == 0
    lane = lax.broadcasted_iota(jnp.int32, (Q_SUB, LANES), 1)
    ones = jnp.ones((2 * Q_SUB, HEAD_DIM), BF16)
    if rb == 1:
        @pl.when(first)
        def _():
            kprev[...] = jnp.zeros(kprev.shape, BF16)
            vprev[...] = jnp.zeros(vprev.shape, BF16)
    for rr in range(rb):
        for s in range(tq // Q_SUB):
            r0 = s * Q_SUB

            def keys(h, col0, s=s, r0=r0, rr=rr):
                cs = slice(col0 + h * HEAD_DIM, col0 + (h + 1) * HEAD_DIM)
                if s > 0:
                    return kv_ref[0, rr, r0 - Q_SUB:r0 + Q_SUB, cs]
                cur = kv_ref[0, rr, 0:Q_SUB, cs]
                if rb == 1:
                    prev = (kprev if col0 == 0 else vprev)[:, h * HEAD_DIM:(h + 1) * HEAD_DIM]
                else:
                    prev = jnp.zeros((Q_SUB, HEAD_DIM), BF16)
                return jnp.concatenate([prev, cur], axis=0)

            if s > 0:
                boff = 0
            elif rb > 1:
                boff = HEADS
            else:
                boff = jnp.where(first, HEADS, 0)
            m_tile = jnp.zeros((Q_SUB, LANES), F32)
            l_tile = jnp.ones((Q_SUB, LANES), F32)
            for h in range(HEADS):
                cs = slice(h * HEAD_DIM, (h + 1) * HEAD_DIM)
                q = q_ref[0, rr, r0:r0 + Q_SUB, cs]
                sc = lax.dot_general(q, keys(h, 0), (((1,), (1,)), ((), ())),
                                     preferred_element_type=F32)
                sc = sc + bias_ref[boff + h]
                m = jnp.max(sc, axis=-1, keepdims=True)
                p = jnp.exp2(sc - m).astype(BF16)
                v_ext = jnp.concatenate([keys(h, D_MODEL), ones], axis=1)
                o = jnp.dot(p, v_ext, preferred_element_type=F32)
                o_ref[0, rr, r0:r0 + Q_SUB, cs] = o[:, :HEAD_DIM].astype(o_ref.dtype)
                m_tile = jnp.where(lane == h, m, m_tile)
                l_tile = jnp.where(lane == h, o[:, HEAD_DIM:], l_tile)
            ml_ref[0, rr, r0:r0 + Q_SUB, 0:LANES] = m_tile
            ml_ref[0, rr, r0:r0 + Q_SUB, LANES:2 * LANES] = l_tile
    if rb == 1:
        kprev[...] = kv_ref[0, 0, tq - Q_SUB:tq, 0:D_MODEL]
        vprev[...] = kv_ref[0, 0, tq - Q_SUB:tq, D_MODEL:2 * D_MODEL]


def _attn_branch(q, kv, bias, tq, rb):
    B, dil, L, D = q.shape
    assert rb == 1 or tq == L
    return pl.pallas_call(
        functools.partial(_attn_kernel, tq=tq, rb=rb),
        grid=(B, dil // rb, L // tq),
        in_specs=[
            pl.BlockSpec((1, rb, tq, D), lambda b, r, i: (b, r, i, 0)),
            pl.BlockSpec((1, rb, tq, 2 * D), lambda b, r, i: (b, r, i, 0)),
            pl.BlockSpec(bias.shape, lambda b, r, i: (0, 0, 0)),
        ],
        out_specs=[
            pl.BlockSpec((1, rb, tq, D), lambda b, r, i: (b, r, i, 0)),
            pl.BlockSpec((1, rb, tq, 2 * LANES), lambda b, r, i: (b, r, i, 0)),
        ],
        out_shape=[
            jax.ShapeDtypeStruct((B, dil, L, D), BF16),
            jax.ShapeDtypeStruct((B, dil, L, 2 * LANES), F32),
        ],
        scratch_shapes=[pltpu.VMEM((Q_SUB, D), BF16), pltpu.VMEM((Q_SUB, D), BF16)],
        compiler_params=_cparams("arbitrary", "arbitrary", "arbitrary"),
        name=f"attn_d{dil}",
    )(q, kv, bias)


def _merge_kernel(x_ref, o0_ref, o1_ref, o2_ref, ml0_ref, ml1_ref, ml2_ref, ex_ref, wo_ref, out_ref,
                  osl1, osl2, msl, lhs, *, tm):
    o_refs = (o0_ref, o1_ref, o2_ref)
    ml_refs = (ml0_ref, ml1_ref, ml2_ref)
    osls = (None, osl1, osl2)
    ms, ls = [], []
    for g, (_, dil) in enumerate(BRANCHES):
        if dil == 1:
            ms.append(ml_refs[g][0, 0, :, 0:LANES])
            ls.append(ml_refs[g][0, 0, :, LANES:2 * LANES])
        else:
            n = tm // dil
            for r in range(dil):
                msl[g - 1, 0, pl.ds(r, n, stride=dil), :] = ml_refs[g][0, r, :, 0:LANES]
                msl[g - 1, 1, pl.ds(r, n, stride=dil), :] = ml_refs[g][0, r, :, LANES:2 * LANES]
                for c in range(N_SLABS):
                    osls[g][c, pl.ds(r, n, stride=dil), :] = (
                        o_refs[g][0, r, :, c * LANES:(c + 1) * LANES].astype(F32))
            ms.append(msl[g - 1, 0])
            ls.append(msl[g - 1, 1])
    m = jnp.maximum(jnp.maximum(ms[0], ms[1]), ms[2])
    us = [jnp.exp2(mg - m) for mg in ms]
    den = us[0] * ls[0] + us[1] * ls[1] + us[2] * ls[2]

    def split_bf16(w):
        hi = w.astype(BF16)
        lo = (w - hi.astype(F32)).astype(BF16)
        return jnp.concatenate([hi, lo], axis=1)

    a_split = [split_bf16(u / den) for u in us]
    for cc in range(N_SLABS // 2):
        wide = slice(2 * cc * LANES, (2 * cc + 2) * LANES)
        a = [jnp.dot(w, ex_ref[:, wide], preferred_element_type=F32) for w in a_split]
        for k in range(2):
            c = 2 * cc + k
            cs = slice(c * LANES, (c + 1) * LANES)
            ks = slice(k * LANES, (k + 1) * LANES)
            o0 = o0_ref[0, 0, :, cs].astype(F32)
            merged = a[0][:, ks] * o0 + a[1][:, ks] * osl1[c] + a[2][:, ks] * osl2[c]
            lhs[:, cs] = merged.astype(BF16)
    out_ref[0] = x_ref[0] + jnp.dot(lhs[...], wo_ref[...], preferred_element_type=F32)


def _merge_out(x, os_, lses, wo, wi, tm=512):
    B, T, D = x.shape
    expand = np.zeros((2 * LANES, D), np.float32)
    for h in range(HEADS):
        expand[h, h * HEAD_DIM:(h + 1) * HEAD_DIM] = 1.0
        expand[LANES + h, h * HEAD_DIM:(h + 1) * HEAD_DIM] = 1.0
    in_specs = [pl.BlockSpec((1, tm, D), lambda b, i: (b, i, 0))]
    for (_, dil) in BRANCHES:
        in_specs.append(pl.BlockSpec((1, dil, tm // dil, D), lambda b, i: (b, 0, i, 0)))
    for (_, dil) in BRANCHES:
        in_specs.append(pl.BlockSpec((1, dil, tm // dil, 2 * LANES), lambda b, i: (b, 0, i, 0)))
    in_specs.append(pl.BlockSpec((2 * LANES, D), lambda b, i: (0, 0)))
    in_specs.append(pl.BlockSpec((None, D, D), lambda b, i: (wi, 0, 0)))
    return pl.pallas_call(
        functools.partial(_merge_kernel, tm=tm),
        grid=(B, T // tm),
        in_specs=in_specs,
        out_specs=pl.BlockSpec((1, tm, D), lambda b, i: (b, i, 0)),
        out_shape=jax.ShapeDtypeStruct((B, T, D), F32),
        scratch_shapes=[
            pltpu.VMEM((N_SLABS, tm, LANES), F32),
            pltpu.VMEM((N_SLABS, tm, LANES), F32),
            pltpu.VMEM((len(BRANCHES) - 1, 2, tm, LANES), F32),
            pltpu.VMEM((tm, D), BF16),
        ],
        compiler_params=_cparams("arbitrary", "arbitrary"),
        name="merge_out",
    )(x, *os_, *lses, jnp.asarray(expand, BF16), wo)


def _attn_sample_kernel(q_ref, kvn_ref, c0_ref, c1_ref, c2_ref, bias_ref, o_ref):
    caches = (c0_ref, c1_ref, c2_ref)
    sc_c, sc_n = [], []
    for g in range(len(BRANCHES)):
        q = q_ref[0, g]
        kc = caches[g][0, :, 0]
        sc_c.append(jnp.sum(kc * q[None], axis=-1, keepdims=True) + bias_ref[g, 0:N_KEYS - 1])
        sc_n.append(jnp.sum(kvn_ref[0, g, 0] * q, axis=-1, keepdims=True) + bias_ref[g, N_KEYS - 1])
    m = sc_n[0]
    for g in range(len(BRANCHES)):
        m = jnp.maximum(m, jnp.maximum(jnp.max(sc_c[g], axis=0), sc_n[g]))
    num = jnp.zeros((HEADS, HEAD_DIM), F32)
    den = jnp.zeros((HEADS, HEAD_DIM), F32)
    for g in range(len(BRANCHES)):
        p_c = jnp.exp(sc_c[g] - m[None])
        p_n = jnp.exp(sc_n[g] - m)
        den = den + jnp.sum(p_c, axis=0) + p_n
        num = num + jnp.sum(p_c * caches[g][0, :, 1], axis=0) + p_n * kvn_ref[0, g, 1]
    o_ref[0] = num / den


def _attn_sample(q, kvn, caches, bias):
    Bs = q.shape[0]
    in_specs = [
        pl.BlockSpec((1,) + q.shape[1:], lambda b: (b, 0, 0, 0)),
        pl.BlockSpec((1,) + kvn.shape[1:], lambda b: (b, 0, 0, 0, 0)),
    ]
    for c in caches:
        in_specs.append(pl.BlockSpec((1, N_KEYS - 1, None, 2, HEADS, HEAD_DIM),
                                     lambda b: (b, 0, 0, 0, 0, 0)))
    in_specs.append(pl.BlockSpec(bias.shape, lambda b: (0, 0, 0, 0)))
    return pl.pallas_call(
        _attn_sample_kernel,
        grid=(Bs,),
        in_specs=in_specs,
        out_specs=pl.BlockSpec((1, HEADS, HEAD_DIM), lambda b: (b, 0, 0)),
        out_shape=jax.ShapeDtypeStruct((Bs, HEADS, HEAD_DIM), F32),
        compiler_params=_cparams("arbitrary"),
        name="attn_sample",
    )(q, kvn, *caches, bias)


def _out_sample_kernel(x_ref, o_ref, wo_ref, out_ref):
    out_ref[...] = x_ref[...] + jnp.dot(o_ref[...].astype(BF16), wo_ref[...],
                                        preferred_element_type=F32)


def _out_sample(x, o, wo):
    return pl.pallas_call(
        _out_sample_kernel,
        out_shape=jax.ShapeDtypeStruct(x.shape, F32),
        compiler_params=pltpu.CompilerParams(vmem_limit_bytes=VMEM_LIMIT),
        name="out_sample",
    )(x, o, wo)


def _t5_bucket(dist):
    max_exact = NUM_BUCKETS // 2
    df = jnp.maximum(dist, 1).astype(F32)
    large = max_exact + (jnp.log(df / max_exact) / math.log(MAX_DISTANCE / max_exact)
                         * (NUM_BUCKETS - max_exact)).astype(jnp.int32)
    large = jnp.minimum(large, NUM_BUCKETS - 1)
    return jnp.where(dist < max_exact, dist, large)


def _bias_tables(rel_bias):
    out = []
    for g, (w, d) in enumerate(BRANCHES):
        dist = jnp.arange(N_KEYS, dtype=jnp.int32) * d
        out.append(rel_bias[_t5_bucket(dist)][:, g * HEADS:(g + 1) * HEADS])
    return out


def _block_bias(tab):
    pad_lo = jnp.full((HEADS, Q_SUB - 1), MASKED, F32)
    pad_hi = jnp.full((HEADS, Q_SUB), MASKED, F32)
    f = jnp.concatenate([pad_lo, tab[::-1].T, pad_hi], axis=1)
    width = f.shape[1]
    flat = jnp.tile(f, (1, Q_SUB))[:, :Q_SUB * (width - 1)]
    toep = flat.reshape(HEADS, Q_SUB, width - 1)
    return toep[:, :, Q_SUB - 1:Q_SUB - 1 + 2 * Q_SUB]


def kernel(x_prompt, x_sample, state_pool, cache_kv_w128, cache_kv_w512, cache_kv_w2048,
           norm_mix, pool_w, pool_scale, norm_mlp, mlp_in, mlp_out, norm_kv, w_kv, w_q, w_o,
           rel_bias, norm_final):
    B, T, D = x_prompt.shape
    Bs = x_sample.shape[0]
    n_a = pool_w.shape[0]
    depth = norm_mix.shape[0]
    n_br = len(BRANCHES)
    qk_scale = HEAD_DIM ** -0.5

    pool_w_b = pool_w.astype(BF16)
    mlp_in_b = mlp_in.astype(BF16)
    mlp_out_b = mlp_out.astype(BF16)
    w_kv_b = w_kv.astype(BF16)[None]
    w_q_b = w_q.astype(BF16)
    w_o_b = w_o.astype(BF16)
    row = lambda v: v.reshape(1, D)
    g_mix = norm_mix.reshape(depth, 1, D)
    g_mlp = norm_mlp.reshape(depth, 1, D)
    g_kv = norm_kv.reshape(1, 1, D)

    tabs = _bias_tables(rel_bias)
    no_prev = np.arange(2 * Q_SUB)[None, None, :] < Q_SUB
    blk_bias = []
    for t in tabs:
        bb = _block_bias(t * LOG2E)
        blk_bias.append(jnp.concatenate([bb, jnp.where(no_prev, MASKED, bb)], axis=0))
    smp_bias = jnp.stack([jnp.broadcast_to(t[::-1][:, :, None], (N_KEYS, HEADS, HEAD_DIM))
                          for t in tabs])

    caches = [c.reshape(Bs, N_KEYS - 1, d, 2, HEADS, HEAD_DIM)
              for c, (w, d) in zip((cache_kv_w128, cache_kv_w512, cache_kv_w2048), BRANCHES)]

    xp = x_prompt
    xs = x_sample.reshape(Bs, D)
    pool_p, pool_s, kv_p = [], [], []
    kvs = kvn = None
    for l in range(depth):
        last = l == depth - 1
        if l < n_a:
            xp, nbuf = _pool_layer(xp, row(norm_mix[l]), pool_w_b[l], row(pool_scale[l]))
            pool_p.append(nbuf)
            xs, hs = _pool_sample(xs, state_pool[l], row(norm_mix[l]), pool_w_b[l], row(pool_scale[l]))
            pool_s.append(jnp.concatenate([state_pool[l][:, 1:], hs[:, None, :]], axis=1))
        else:
            lb = l - n_a
            if l == n_a:
                tails = tuple(min(w, T) for (w, _) in BRANCHES)
                res = _proj(xp, g_kv, 0, w_kv_b, 0, 2 * D, 256, tails=tails)
                kvs = res[:n_br]
                kv_p = [t.reshape(B, rows, 2, HEADS, HEAD_DIM) for t, rows in zip(res[n_br:], tails)]
                kvn = _proj_sample(xs, g_kv, 0, w_kv_b, 0)
            qs_p = _proj(xp, g_mix, l, w_q_b, lb, D, 1024, scale=qk_scale * LOG2E)
            os_, lses = [], []
            for g, (w, d) in enumerate(BRANCHES):
                L = T // d
                tq, rb = (L, min(d, 1024 // L)) if L < 1024 else (1024, 1)
                o_g, ml_g = _attn_branch(qs_p[g], kvs[g], blk_bias[g], tq, rb)
                os_.append(o_g)
                lses.append(ml_g)
            xp = _merge_out(xp, os_, lses, w_o_b, lb)
            qs = _proj_sample(xs, g_mix, l, w_q_b, lb, scale=qk_scale)
            o_s = _attn_sample(qs.reshape(Bs, n_br, HEADS, HEAD_DIM),
                               kvn.reshape(Bs, n_br, 2, HEADS, HEAD_DIM), caches, smp_bias)
            xs = _out_sample(xs, o_s.reshape(Bs, D), w_o_b[lb])
        xp = _mlp(xp.reshape(B * T, D), g_mlp, mlp_in_b, mlp_out_b, row(norm_final), l, last,
                  tm=1024).reshape(B, T, D)
        xs = _mlp(xs, g_mlp, mlp_in_b, mlp_out_b, row(norm_final), l, last, tm=Bs)

    kv_s = [kvn[:, g * 2 * D:(g + 1) * 2 * D].reshape(Bs, 1, 2, HEADS, HEAD_DIM) for g in range(n_br)]
    return (xp, xs.reshape(Bs, 1, D), jnp.stack(pool_p), jnp.stack(pool_s),
            kv_p[0], kv_s[0], kv_p[1], kv_s[1], kv_p[2], kv_s[2])
```

```python
import functools
import math

import jax
import jax.numpy as jnp
import numpy as np
from jax import lax
from jax.experimental import pallas as pl
from jax.experimental.pallas import tpu as pltpu

F32 = jnp.float32
BF16 = jnp.bfloat16

D_MODEL = 1024
HEADS = 8
HEAD_DIM = 128
D_FF = 4 * D_MODEL
POOL_WINDOWS = (2, 4, 8, 16)
POOL_GROUP = D_MODEL // len(POOL_WINDOWS)
POOL_STATE = max(POOL_WINDOWS) - 1
BRANCHES = ((128, 1), (512, 4), (2048, 16))
N_KEYS = 129
NUM_BUCKETS = 32
MAX_DISTANCE = 2048
SAMPLE_POS0 = 8192
EPS = 1e-6
MASKED = -1e30
LOG2E = math.log2(math.e)

LANES = 128
N_SLABS = D_MODEL // LANES
Q_SUB = 128
ATTN_ROWS = 2048
VMEM_LIMIT = 56 * 1024 * 1024


def _cparams(*sem):
    return pltpu.CompilerParams(dimension_semantics=sem, vmem_limit_bytes=VMEM_LIMIT)


def _rms(x, g):
    ms = jnp.mean(x * x, axis=-1, keepdims=True)
    return x * lax.rsqrt(ms + EPS) * g


POOL_BASE = 24
POOL_LO = 8


def _pool_kernel(x_ref, g_ref, w_ref, sc_ref, o_ref, nb_ref, hbuf, sbuf, *, tm):
    t = pl.program_id(1)
    end = POOL_BASE + tm

    @pl.when(jnp.logical_and(pl.program_id(0) == 0, t == 0))
    def _():
        hbuf[0:POOL_BASE, :] = jnp.zeros((POOL_BASE, D_MODEL), F32)
        sbuf[...] = jnp.zeros(sbuf.shape, F32)

    @pl.when(t == 0)
    def _():
        hbuf[POOL_BASE - 16:POOL_BASE, :] = jnp.zeros((16, D_MODEL), F32)

    x = x_ref[0]
    h = _rms(x, g_ref[...])
    hbuf[POOL_BASE:end, :] = h
    pos = t * tm + lax.broadcasted_iota(jnp.int32, (tm, 1), 0)
    for gi, w in enumerate(POOL_WINDOWS):
        sl = slice(gi * POOL_GROUP, (gi + 1) * POOL_GROUP)
        n_stage = w.bit_length() - 1
        s = None
        for si in range(n_stage):
            sh = 1 << si
            lo = POOL_BASE if si == n_stage - 1 else POOL_LO
            if si == 0:
                val = hbuf[lo:end, sl] + hbuf[lo - sh:end - sh, sl]
            else:
                val = sbuf[si - 1, lo:end, :] + sbuf[si - 1, lo - sh:end - sh, :]
            if si == n_stage - 1:
                s = val
            else:
                sbuf[si, lo:end, :] = val
        hg = h[:, sl]
        cnt = jnp.minimum(pos + 1, w).astype(F32)
        pooled = s / cnt - hg
        y = jnp.dot(pooled.astype(BF16), w_ref[gi], preferred_element_type=F32)
        o_ref[0, :, sl] = x[:, sl] + y * sc_ref[:, sl]
    nb_ref[0] = hbuf[end - POOL_STATE:end, :]
    hbuf[POOL_BASE - 16:POOL_BASE, :] = hbuf[end - 16:end, :]


def _pool_layer(x, gain, w_pool, scale, tm=1024):
    B, T, D = x.shape
    return pl.pallas_call(
        functools.partial(_pool_kernel, tm=tm),
        grid=(B, T // tm),
        in_specs=[
            pl.BlockSpec((1, tm, D), lambda b, t: (b, t, 0)),
            pl.BlockSpec((1, D), lambda b, t: (0, 0)),
            pl.BlockSpec(w_pool.shape, lambda b, t: (0, 0, 0)),
            pl.BlockSpec((1, D), lambda b, t: (0, 0)),
        ],
        out_specs=[
            pl.BlockSpec((1, tm, D), lambda b, t: (b, t, 0)),
            pl.BlockSpec((1, POOL_STATE, D), lambda b, t: (b, 0, 0)),
        ],
        out_shape=[
            jax.ShapeDtypeStruct((B, T, D), F32),
            jax.ShapeDtypeStruct((B, POOL_STATE, D), F32),
        ],
        scratch_shapes=[pltpu.VMEM((POOL_BASE + tm, D), F32),
                        pltpu.VMEM((len(POOL_WINDOWS) - 1, POOL_BASE + tm, POOL_GROUP), F32)],
        compiler_params=_cparams("arbitrary", "arbitrary"),
        name="pool_layer",
    )(x, gain, w_pool, scale)


def _pool_sample_kernel(x_ref, st_ref, g_ref, w_ref, sc_ref, o_ref, h_ref):
    x = x_ref[...]
    h = _rms(x, g_ref[...])
    h_ref[...] = h
    for gi, w in enumerate(POOL_WINDOWS):
        sl = slice(gi * POOL_GROUP, (gi + 1) * POOL_GROUP)
        hg = h[:, sl]
        s = hg
        for j in range(1, w):
            s = s + st_ref[:, POOL_STATE - j, sl]
        pooled = s / float(min(SAMPLE_POS0 + 1, w)) - hg
        y = jnp.dot(pooled.astype(BF16), w_ref[gi], preferred_element_type=F32)
        o_ref[:, sl] = x[:, sl] + y * sc_ref[:, sl]


def _pool_sample(x, state, gain, w_pool, scale):
    Bs, D = x.shape
    return pl.pallas_call(
        _pool_sample_kernel,
        out_shape=[jax.ShapeDtypeStruct((Bs, D), F32), jax.ShapeDtypeStruct((Bs, D), F32)],
        compiler_params=pltpu.CompilerParams(vmem_limit_bytes=VMEM_LIMIT),
        name="pool_sample",
    )(x, state, gain, w_pool, scale)


def _mlp_kernel(x_ref, g_ref, w1_ref, w2_ref, gf_ref, o_ref, hs, us, *, tc, final_norm):
    x = x_ref[...]
    hs[...] = _rms(x, g_ref[...]).astype(BF16)
    for c in range(D_FF // tc):
        u = jnp.dot(hs[...], w1_ref[:, c * tc:(c + 1) * tc], preferred_element_type=F32)
        us[:, c * tc:(c + 1) * tc] = jnp.square(jnp.maximum(u, 0.0)).astype(BF16)
    y = x + jnp.dot(us[...], w2_ref[...], preferred_element_type=F32)
    if final_norm:
        y = _rms(y, gf_ref[...])
    o_ref[...] = y


def _mlp(x, gains, w1, w2, gain_final, layer, final_norm, tm, tc=512):
    M, D = x.shape
    return pl.pallas_call(
        functools.partial(_mlp_kernel, tc=tc, final_norm=final_norm),
        grid=(M // tm,),
        in_specs=[
            pl.BlockSpec((tm, D), lambda i: (i, 0)),
            pl.BlockSpec((None, 1, D), lambda i: (layer, 0, 0)),
            pl.BlockSpec((None, D, D_FF), lambda i: (layer, 0, 0), pipeline_mode=pl.Buffered(1)),
            pl.BlockSpec((None, D_FF, D), lambda i: (layer, 0, 0), pipeline_mode=pl.Buffered(1)),
            pl.BlockSpec((1, D), lambda i: (0, 0)),
        ],
        out_specs=pl.BlockSpec((tm, D), lambda i: (i, 0)),
        out_shape=jax.ShapeDtypeStruct((M, D), F32),
        scratch_shapes=[pltpu.VMEM((tm, D), BF16), pltpu.VMEM((tm, D_FF), BF16)],
        compiler_params=_cparams("arbitrary"),
        name="mlp",
    )(x, gains, w1, w2, gain_final)


def _proj_kernel(x_ref, g_ref, w_ref, *refs, tm, nc, scale, tails):
    n_br = len(BRANCHES)
    o_refs = refs[:n_br]
    if tails:
        tail_refs = refs[n_br:2 * n_br]
        hsl, tsl = refs[2 * n_br], refs[2 * n_br + 1]
        lhs = refs[2 * n_br + 2:]
        i = pl.program_id(1)
        nt = pl.num_programs(1)
    else:
        hsl = refs[n_br]
        lhs = refs[n_br + 1:]
    h = _rms(x_ref[0], g_ref[...])
    for c in range(N_SLABS):
        hsl[c] = h[:, c * LANES:(c + 1) * LANES]
    for g, (_, dil) in enumerate(BRANCHES):
        n = tm // dil
        if dil == 1:
            lhs[g][...] = h.astype(BF16)
        else:
            for r in range(dil):
                for c in range(N_SLABS):
                    lhs[g][r * n:(r + 1) * n, c * LANES:(c + 1) * LANES] = (
                        hsl[c, pl.ds(r, n, stride=dil), :].astype(BF16))
        res = jnp.dot(lhs[g][...], w_ref[:, g * nc:(g + 1) * nc], preferred_element_type=F32)
        if scale != 1.0:
            res = res * scale
        for r in range(dil):
            o_refs[g][0, r] = res[r * n:(r + 1) * n].astype(o_refs[g].dtype)

        if tails:
            tail_ref, rows = tail_refs[g], tails[g]
            if rows < tm:
                @pl.when(i == nt - 1)
                def _(tail_ref=tail_ref, rows=rows, res=res):
                    tail_ref[0] = res[tm - rows:tm]
            else:
                @pl.when(i >= nt - rows // tm)
                def _(tail_ref=tail_ref, res=res, dil=dil, n=n):
                    if dil == 1:
                        tail_ref[0] = res
                    else:
                        for c in range(nc // LANES):
                            for r in range(dil):
                                tsl[c, pl.ds(r, n, stride=dil), :] = (
                                    res[r * n:(r + 1) * n, c * LANES:(c + 1) * LANES])
                        for c in range(nc // LANES):
                            tail_ref[0, :, c * LANES:(c + 1) * LANES] = tsl[c]


def _proj(x, gains, gi, w, wi, nc, tm, scale=1.0, tails=()):
    B, T, D = x.shape
    nt = T // tm
    out_specs, out_shape = [], []
    for (_, dil) in BRANCHES:
        out_specs.append(pl.BlockSpec((1, dil, tm // dil, nc), lambda b, i: (b, 0, i, 0)))
        out_shape.append(jax.ShapeDtypeStruct((B, dil, T // dil, nc), BF16))
    scratch = [pltpu.VMEM((N_SLABS, tm, LANES), F32)]
    if tails:
        for rows, (_, dil) in zip(tails, BRANCHES):
            assert rows >= tm or dil == 1
            if rows < tm:
                out_specs.append(pl.BlockSpec((1, rows, nc), lambda b, i: (b, 0, 0),
                                              pipeline_mode=pl.Buffered(1)))
            else:
                first = nt - rows // tm
                out_specs.append(pl.BlockSpec(
                    (1, tm, nc), lambda b, i, first=first: (b, jnp.maximum(i - first, 0), 0),
                    pipeline_mode=pl.Buffered(1)))
            out_shape.append(jax.ShapeDtypeStruct((B, rows, nc), F32))
        scratch.append(pltpu.VMEM((nc // LANES, tm, LANES), F32))
    scratch += [pltpu.VMEM((tm, D), BF16) for _ in BRANCHES]
    return pl.pallas_call(
        functools.partial(_proj_kernel, tm=tm, nc=nc, scale=scale, tails=tuple(tails)),
        grid=(B, nt),
        in_specs=[
            pl.BlockSpec((1, tm, D), lambda b, i: (b, i, 0)),
            pl.BlockSpec((None, 1, D), lambda b, i: (gi, 0, 0)),
            pl.BlockSpec((None, D, len(BRANCHES) * nc), lambda b, i: (wi, 0, 0),
                         pipeline_mode=pl.Buffered(1)),
        ],
        out_specs=out_specs,
        out_shape=out_shape,
        scratch_shapes=scratch,
        compiler_params=_cparams("arbitrary", "arbitrary"),
        name="proj_kv" if tails else "proj_q",
    )(x, gains, w)


def _proj_sample_kernel(x_ref, g_ref, w_ref, o_ref, *, scale):
    h = _rms(x_ref[...], g_ref[...]).astype(BF16)
    res = jnp.dot(h, w_ref[...], preferred_element_type=F32)
    if scale != 1.0:
        res = res * scale
    o_ref[...] = res


def _proj_sample(x, gains, gi, w, wi, scale=1.0, tn=1024):
    Bs, D = x.shape
    nc = w.shape[2]
    return pl.pallas_call(
        functools.partial(_proj_sample_kernel, scale=scale),
        grid=(nc // tn,),
        in_specs=[
            pl.BlockSpec((Bs, D), lambda j: (0, 0)),
            pl.BlockSpec((None, 1, D), lambda j: (gi, 0, 0)),
            pl.BlockSpec((None, D, tn), lambda j: (wi, 0, j)),
        ],
        out_specs=pl.BlockSpec((Bs, tn), lambda j: (0, j)),
        out_shape=jax.ShapeDtypeStruct((Bs, nc), F32),
        compiler_params=_cparams("arbitrary"),
        name="proj_sample",
    )(x, gains, w)


def _attn_kernel(q_ref, kv_ref, bias_ref, o_ref, ml_ref, kprev, vprev, *, tq, rb):
    first = pl.program_id(2) == 0
    lane = lax.broadcasted_iota(jnp.int32, (Q_SUB, LANES), 1)
    ones = jnp.ones((2 * Q_SUB, HEAD_DIM), BF16)
    if rb == 1:
        @pl.when(first)
        def _():
            kprev[...] = jnp.zeros(kprev.shape, BF16)
            vprev[...] = jnp.zeros(vprev.shape, BF16)
    for rr in range(rb):
        for s in range(tq // Q_SUB):
            r0 = s * Q_SUB

            def keys(h, col0, s=s, r0=r0, rr=rr):
                cs = slice(col0 + h * HEAD_DIM, col0 + (h + 1) * HEAD_DIM)
                if s > 0:
                    return kv_ref[0, rr, r0 - Q_SUB:r0 + Q_SUB, cs]
                cur = kv_ref[0, rr, 0:Q_SUB, cs]
                if rb == 1:
                    prev = (kprev if col0 == 0 else vprev)[:, h * HEAD_DIM:(h + 1) * HEAD_DIM]
                else:
                    prev = jnp.zeros((Q_SUB, HEAD_DIM), BF16)
                return jnp.concatenate([prev, cur], axis=0)

            if s > 0:
                boff = 0
            elif rb > 1:
                boff = HEADS
            else:
                boff = jnp.where(first, HEADS, 0)
            m_tile = jnp.zeros((Q_SUB, LANES), F32)
            l_tile = jnp.ones((Q_SUB, LANES), F32)
            for h in range(HEADS):
                cs = slice(h * HEAD_DIM, (h + 1) * HEAD_DIM)
                q = q_ref[0, rr, r0:r0 + Q_SUB, cs]
                sc = lax.dot_general(q, keys(h, 0), (((1,), (1,)), ((), ())),
                                     preferred_element_type=F32)
                sc = sc + bias_ref[boff + h]
                m = jnp.max(sc, axis=-1, keepdims=True)
                p = jnp.exp2(sc - m).astype(BF16)
                v_ext = jnp.concatenate([keys(h, D_MODEL), ones], axis=1)
                o = jnp.dot(p, v_ext, preferred_element_type=F32)
                o_ref[0, rr, r0:r0 + Q_SUB, cs] = o[:, :HEAD_DIM].astype(o_ref.dtype)
                m_tile = jnp.where(lane == h, m, m_tile)
                l_tile = jnp.where(lane == h, o[:, HEAD_DIM:], l_tile)
            ml_ref[0, rr, r0:r0 + Q_SUB, 0:LANES] = m_tile
            ml_ref[0, rr, r0:r0 + Q_SUB, LANES:2 * LANES] = l_tile
    if rb == 1:
        kprev[...] = kv_ref[0, 0, tq - Q_SUB:tq, 0:D_MODEL]
        vprev[...] = kv_ref[0, 0, tq - Q_SUB:tq, D_MODEL:2 * D_MODEL]


def _attn_branch(q, kv, bias, tq, rb):
    B, dil, L, D = q.shape
    assert rb == 1 or tq == L
    return pl.pallas_call(
        functools.partial(_attn_kernel, tq=tq, rb=rb),
        grid=(B, dil // rb, L // tq),
        in_specs=[
            pl.BlockSpec((1, rb, tq, D), lambda b, r, i: (b, r, i, 0)),
            pl.BlockSpec((1, rb, tq, 2 * D), lambda b, r, i: (b, r, i, 0)),
            pl.BlockSpec(bias.shape, lambda b, r, i: (0, 0, 0)),
        ],
        out_specs=[
            pl.BlockSpec((1, rb, tq, D), lambda b, r, i: (b, r, i, 0)),
            pl.BlockSpec((1, rb, tq, 2 * LANES), lambda b, r, i: (b, r, i, 0)),
        ],
        out_shape=[
            jax.ShapeDtypeStruct((B, dil, L, D), BF16),
            jax.ShapeDtypeStruct((B, dil, L, 2 * LANES), F32),
        ],
        scratch_shapes=[pltpu.VMEM((Q_SUB, D), BF16), pltpu.VMEM((Q_SUB, D), BF16)],
        compiler_params=_cparams("arbitrary", "arbitrary", "arbitrary"),
        name=f"attn_d{dil}",
    )(q, kv, bias)


def _merge_kernel(x_ref, o0_ref, o1_ref, o2_ref, ml0_ref, ml1_ref, ml2_ref, ex_ref, wo_ref, out_ref,
                  osl1, osl2, msl, lhs, *, tm):
    o_refs = (o0_ref, o1_ref, o2_ref)
    ml_refs = (ml0_ref, ml1_ref, ml2_ref)
    osls = (None, osl1, osl2)
    ms, ls = [], []
    for g, (_, dil) in enumerate(BRANCHES):
        if dil == 1:
            ms.append(ml_refs[g][0, 0, :, 0:LANES])
            ls.append(ml_refs[g][0, 0, :, LANES:2 * LANES])
        else:
            n = tm // dil
            for r in range(dil):
                msl[g - 1, 0, pl.ds(r, n, stride=dil), :] = ml_refs[g][0, r, :, 0:LANES]
                msl[g - 1, 1, pl.ds(r, n, stride=dil), :] = ml_refs[g][0, r, :, LANES:2 * LANES]
                for c in range(N_SLABS):
                    osls[g][c, pl.ds(r, n, stride=dil), :] = (
                        o_refs[g][0, r, :, c * LANES:(c + 1) * LANES].astype(F32))
            ms.append(msl[g - 1, 0])
            ls.append(msl[g - 1, 1])
    m = jnp.maximum(jnp.maximum(ms[0], ms[1]), ms[2])
    us = [jnp.exp2(mg - m) for mg in ms]
    den = us[0] * ls[0] + us[1] * ls[1] + us[2] * ls[2]

    def split_bf16(w):
        hi = w.astype(BF16)
        lo = (w - hi.astype(F32)).astype(BF16)
        return jnp.concatenate([hi, lo], axis=1)

    a_split = [split_bf16(u / den) for u in us]
    for cc in range(N_SLABS // 2):
        wide = slice(2 * cc * LANES, (2 * cc + 2) * LANES)
        a = [jnp.dot(w, ex_ref[:, wide], preferred_element_type=F32) for w in a_split]
        for k in range(2):
            c = 2 * cc + k
            cs = slice(c * LANES, (c + 1) * LANES)
            ks = slice(k * LANES, (k + 1) * LANES)
            o0 = o0_ref[0, 0, :, cs].astype(F32)
            merged = a[0][:, ks] * o0 + a[1][:, ks] * osl1[c] + a[2][:, ks] * osl2[c]
            lhs[:, cs] = merged.astype(BF16)
    out_ref[0] = x_ref[0] + jnp.dot(lhs[...], wo_ref[...], preferred_element_type=F32)


def _merge_out(x, os_, lses, wo, wi, tm=512):
    B, T, D = x.shape
    expand = np.zeros((2 * LANES, D), np.float32)
    for h in range(HEADS):
        expand[h, h * HEAD_DIM:(h + 1) * HEAD_DIM] = 1.0
        expand[LANES + h, h * HEAD_DIM:(h + 1) * HEAD_DIM] = 1.0
    in_specs = [pl.BlockSpec((1, tm, D), lambda b, i: (b, i, 0))]
    for (_, dil) in BRANCHES:
        in_specs.append(pl.BlockSpec((1, dil, tm // dil, D), lambda b, i: (b, 0, i, 0)))
    for (_, dil) in BRANCHES:
        in_specs.append(pl.BlockSpec((1, dil, tm // dil, 2 * LANES), lambda b, i: (b, 0, i, 0)))
    in_specs.append(pl.BlockSpec((2 * LANES, D), lambda b, i: (0, 0)))
    in_specs.append(pl.BlockSpec((None, D, D), lambda b, i: (wi, 0, 0)))
    return pl.pallas_call(
        functools.partial(_merge_kernel, tm=tm),
        grid=(B, T // tm),
        in_specs=in_specs,
        out_specs=pl.BlockSpec((1, tm, D), lambda b, i: (b, i, 0)),
        out_shape=jax.ShapeDtypeStruct((B, T, D), F32),
        scratch_shapes=[
            pltpu.VMEM((N_SLABS, tm, LANES), F32),
            pltpu.VMEM((N_SLABS, tm, LANES), F32),
            pltpu.VMEM((len(BRANCHES) - 1, 2, tm, LANES), F32),
            pltpu.VMEM((tm, D), BF16),
        ],
        compiler_params=_cparams("arbitrary", "arbitrary"),
        name="merge_out",
    )(x, *os_, *lses, jnp.asarray(expand, BF16), wo)


def _attn_sample_kernel(q_ref, kvn_ref, c0_ref, c1_ref, c2_ref, bias_ref, o_ref, *, nb):
    caches = (c0_ref, c1_ref, c2_ref)
    for i in range(nb):
        sc_c, sc_n = [], []
        for g in range(len(BRANCHES)):
            q = q_ref[i, g]
            kc = caches[g][i, :, 0]
            sc_c.append(jnp.sum(kc * q[None], axis=-1, keepdims=True) + bias_ref[g, 0:N_KEYS - 1])
            sc_n.append(jnp.sum(kvn_ref[i, g, 0] * q, axis=-1, keepdims=True)
                        + bias_ref[g, N_KEYS - 1])
        m = sc_n[0]
        for g in range(len(BRANCHES)):
            m = jnp.maximum(m, jnp.maximum(jnp.max(sc_c[g], axis=0), sc_n[g]))
        num = jnp.zeros((HEADS, HEAD_DIM), F32)
        den = jnp.zeros((HEADS, HEAD_DIM), F32)
        for g in range(len(BRANCHES)):
            p_c = jnp.exp(sc_c[g] - m[None])
            p_n = jnp.exp(sc_n[g] - m)
            den = den + jnp.sum(p_c, axis=0) + p_n
            num = num + jnp.sum(p_c * caches[g][i, :, 1], axis=0) + p_n * kvn_ref[i, g, 1]
        o_ref[i] = num / den


def _attn_sample(q, kvn, caches, bias, nb=4):
    Bs = q.shape[0]
    in_specs = [
        pl.BlockSpec((nb,) + q.shape[1:], lambda b: (b, 0, 0, 0)),
        pl.BlockSpec((nb,) + kvn.shape[1:], lambda b: (b, 0, 0, 0, 0)),
    ]
    for c in caches:
        in_specs.append(pl.BlockSpec((nb, N_KEYS - 1, None, 2, HEADS, HEAD_DIM),
                                     lambda b: (b, 0, 0, 0, 0, 0)))
    in_specs.append(pl.BlockSpec(bias.shape, lambda b: (0, 0, 0, 0)))
    return pl.pallas_call(
        functools.partial(_attn_sample_kernel, nb=nb),
        grid=(Bs // nb,),
        in_specs=in_specs,
        out_specs=pl.BlockSpec((nb, HEADS, HEAD_DIM), lambda b: (b, 0, 0)),
        out_shape=jax.ShapeDtypeStruct((Bs, HEADS, HEAD_DIM), F32),
        compiler_params=_cparams("arbitrary"),
        name="attn_sample",
    )(q, kvn, *caches, bias)


def _out_sample_kernel(x_ref, o_ref, wo_ref, out_ref):
    out_ref[...] = x_ref[...] + jnp.dot(o_ref[...].astype(BF16), wo_ref[...],
                                        preferred_element_type=F32)


def _out_sample(x, o, wo):
    return pl.pallas_call(
        _out_sample_kernel,
        out_shape=jax.ShapeDtypeStruct(x.shape, F32),
        compiler_params=pltpu.CompilerParams(vmem_limit_bytes=VMEM_LIMIT),
        name="out_sample",
    )(x, o, wo)


def _t5_bucket(dist):
    max_exact = NUM_BUCKETS // 2
    df = jnp.maximum(dist, 1).astype(F32)
    large = max_exact + (jnp.log(df / max_exact) / math.log(MAX_DISTANCE / max_exact)
                         * (NUM_BUCKETS - max_exact)).astype(jnp.int32)
    large = jnp.minimum(large, NUM_BUCKETS - 1)
    return jnp.where(dist < max_exact, dist, large)


def _bias_tables(rel_bias):
    out = []
    for g, (w, d) in enumerate(BRANCHES):
        dist = jnp.arange(N_KEYS, dtype=jnp.int32) * d
        out.append(rel_bias[_t5_bucket(dist)][:, g * HEADS:(g + 1) * HEADS])
    return out


def _block_bias(tab):
    pad_lo = jnp.full((HEADS, Q_SUB - 1), MASKED, F32)
    pad_hi = jnp.full((HEADS, Q_SUB), MASKED, F32)
    f = jnp.concatenate([pad_lo, tab[::-1].T, pad_hi], axis=1)
    width = f.shape[1]
    flat = jnp.tile(f, (1, Q_SUB))[:, :Q_SUB * (width - 1)]
    toep = flat.reshape(HEADS, Q_SUB, width - 1)
    return toep[:, :, Q_SUB - 1:Q_SUB - 1 + 2 * Q_SUB]


def kernel(x_prompt, x_sample, state_pool, cache_kv_w128, cache_kv_w512, cache_kv_w2048,
           norm_mix, pool_w, pool_scale, norm_mlp, mlp_in, mlp_out, norm_kv, w_kv, w_q, w_o,
           rel_bias, norm_final):
    B, T, D = x_prompt.shape
    Bs = x_sample.shape[0]
    n_a = pool_w.shape[0]
    depth = norm_mix.shape[0]
    n_br = len(BRANCHES)
    qk_scale = HEAD_DIM ** -0.5

    pool_w_b = pool_w.astype(BF16)
    mlp_in_b = mlp_in.astype(BF16)
    mlp_out_b = mlp_out.astype(BF16)
    w_kv_b = w_kv.astype(BF16)[None]
    w_q_b = w_q.astype(BF16)
    w_o_b = w_o.astype(BF16)
    row = lambda v: v.reshape(1, D)
    g_mix = norm_mix.reshape(depth, 1, D)
    g_mlp = norm_mlp.reshape(depth, 1, D)
    g_kv = norm_kv.reshape(1, 1, D)

    tabs = _bias_tables(rel_bias)
    no_prev = np.arange(2 * Q_SUB)[None, None, :] < Q_SUB
    blk_bias = []
    for t in tabs:
        bb = _block_bias(t * LOG2E)
        blk_bias.append(jnp.concatenate([bb, jnp.where(no_prev, MASKED, bb)], axis=0))
    smp_bias = jnp.stack([jnp.broadcast_to(t[::-1][:, :, None], (N_KEYS, HEADS, HEAD_DIM))
                          for t in tabs])

    caches = [c.reshape(Bs, N_KEYS - 1, d, 2, HEADS, HEAD_DIM)
              for c, (w, d) in zip((cache_kv_w128, cache_kv_w512, cache_kv_w2048), BRANCHES)]

    xp = x_prompt
    xs = x_sample.reshape(Bs, D)
    pool_p, pool_s, kv_p = [], [], []
    kvs = kvn = None
    for l in range(depth):
        last = l == depth - 1
        if l < n_a:
            xp, nbuf = _pool_layer(xp, row(norm_mix[l]), pool_w_b[l], row(pool_scale[l]))
            pool_p.append(nbuf)
            xs, hs = _pool_sample(xs, state_pool[l], row(norm_mix[l]), pool_w_b[l], row(pool_scale[l]))
            pool_s.append(jnp.concatenate([state_pool[l][:, 1:], hs[:, None, :]], axis=1))
        else:
            lb = l - n_a
            if l == n_a:
                tails = tuple(min(w, T) for (w, _) in BRANCHES)
                res = _proj(xp, g_kv, 0, w_kv_b, 0, 2 * D, 512, tails=tails)
                kvs = res[:n_br]
                kv_p = [t.reshape(B, rows, 2, HEADS, HEAD_DIM) for t, rows in zip(res[n_br:], tails)]
                kvn = _proj_sample(xs, g_kv, 0, w_kv_b, 0)
            qs_p = _proj(xp, g_mix, l, w_q_b, lb, D, 1024, scale=qk_scale * LOG2E)
            os_, lses = [], []
            for g, (w, d) in enumerate(BRANCHES):
                L = T // d
                tq, rb = (L, min(d, ATTN_ROWS // L)) if L < ATTN_ROWS else (ATTN_ROWS, 1)
                o_g, ml_g = _attn_branch(qs_p[g], kvs[g], blk_bias[g], tq, rb)
                os_.append(o_g)
                lses.append(ml_g)
            xp = _merge_out(xp, os_, lses, w_o_b, lb)
            qs = _proj_sample(xs, g_mix, l, w_q_b, lb, scale=qk_scale)
            o_s = _attn_sample(qs.reshape(Bs, n_br, HEADS, HEAD_DIM),
                               kvn.reshape(Bs, n_br, 2, HEADS, HEAD_DIM), caches, smp_bias)
            xs = _out_sample(xs, o_s.reshape(Bs, D), w_o_b[lb])
        xp = _mlp(xp.reshape(B * T, D), g_mlp, mlp_in_b, mlp_out_b, row(norm_final), l, last,
                  tm=1024).reshape(B, T, D)
        xs = _mlp(xs, g_mlp, mlp_in_b, mlp_out_b, row(norm_final), l, last, tm=Bs)

    kv_s = [kvn[:, g * 2 * D:(g + 1) * 2 * D].reshape(Bs, 1, 2, HEADS, HEAD_DIM) for g in range(n_br)]
    return (xp, xs.reshape(Bs, 1, D), jnp.stack(pool_p), jnp.stack(pool_s),
            kv_p[0], kv_s[0], kv_p[1], kv_s[1], kv_p[2], kv_s[2])
```

```python
import functools
import math

import jax
import jax.numpy as jnp
import numpy as np
from jax import lax
from jax.experimental import pallas as pl
from jax.experimental.pallas import tpu as pltpu

F32 = jnp.float32
BF16 = jnp.bfloat16

D_MODEL = 1024
HEADS = 8
HEAD_DIM = 128
D_FF = 4 * D_MODEL
POOL_WINDOWS = (2, 4, 8, 16)
POOL_GROUP = D_MODEL // len(POOL_WINDOWS)
POOL_STATE = max(POOL_WINDOWS) - 1
BRANCHES = ((128, 1), (512, 4), (2048, 16))
N_KEYS = 129
NUM_BUCKETS = 32
MAX_DISTANCE = 2048
SAMPLE_POS0 = 8192
EPS = 1e-6
MASKED = -1e30
LOG2E = math.log2(math.e)

LANES = 128
N_SLABS = D_MODEL // LANES
Q_SUB = 128
ATTN_ROWS = 2048
VMEM_LIMIT = 56 * 1024 * 1024


def _cparams(*sem):
    return pltpu.CompilerParams(dimension_semantics=sem, vmem_limit_bytes=VMEM_LIMIT)


def _rms(x, g):
    ms = jnp.mean(x * x, axis=-1, keepdims=True)
    return x * lax.rsqrt(ms + EPS) * g


POOL_BASE = 24
POOL_LO = 8


def _pool_pieces(x_ref, g_ref, w_ref, sc_ref, nb_ref, hbuf, sbuf, x1_ref, t, tm):
    end = POOL_BASE + tm

    def norm():
        hbuf[POOL_BASE:end, :] = _rms(x_ref[0], g_ref[...])

    def group(gi, w):
        sl = slice(gi * POOL_GROUP, (gi + 1) * POOL_GROUP)
        n_stage = w.bit_length() - 1
        s = None
        for si in range(n_stage):
            sh = 1 << si
            lo = POOL_BASE if si == n_stage - 1 else POOL_LO
            if si == 0:
                val = hbuf[lo:end, sl] + hbuf[lo - sh:end - sh, sl]
            else:
                val = sbuf[si - 1, lo:end, :] + sbuf[si - 1, lo - sh:end - sh, :]
            if si == n_stage - 1:
                s = val
            else:
                sbuf[si, lo:end, :] = val
        pos = t * tm + lax.broadcasted_iota(jnp.int32, (tm, 1), 0)
        cnt = jnp.minimum(pos + 1, w).astype(F32)
        pooled = s / cnt - hbuf[POOL_BASE:end, sl]
        y = jnp.dot(pooled.astype(BF16), w_ref[gi], preferred_element_type=F32)
        x1_ref[:, sl] = x_ref[0, :, sl] + y * sc_ref[:, sl]

    def carry():
        nb_ref[0] = hbuf[end - POOL_STATE:end, :]
        hbuf[POOL_BASE - 16:POOL_BASE, :] = hbuf[end - 16:end, :]

    pieces = [norm]
    pieces += [functools.partial(group, gi, w) for gi, w in enumerate(POOL_WINDOWS)]
    return pieces + [carry]


def _mlp_tile(x1_ref, hs_ref, w1_ref, w2_ref, gf_ref, o_ref, us, tc, final_norm, between=()):
    between = list(between)
    n_chunks = D_FF // tc
    assert len(between) <= n_chunks
    for c in range(n_chunks):
        if c < len(between):
            between[c]()
        u = jnp.dot(hs_ref[...], w1_ref[:, c * tc:(c + 1) * tc], preferred_element_type=F32)
        us[:, c * tc:(c + 1) * tc] = jnp.square(jnp.maximum(u, 0.0)).astype(BF16)
    y = x1_ref[...] + jnp.dot(us[...], w2_ref[...], preferred_element_type=F32)
    if final_norm:
        y = _rms(y, gf_ref[...])
    o_ref[...] = y


def _pool_mlp_kernel(x_ref, gmix_ref, wp_ref, sc_ref, gmlp_ref, w1_ref, w2_ref, gf_ref,
                     o_ref, nb_ref, x1a, x1b, hsa, hsb, us, hbuf, sbuf, *, tm, nt, n_tiles, tc):
    j = pl.program_id(0)
    t = jnp.minimum(j, n_tiles - 1) % nt

    @pl.when(j == 0)
    def _():
        hbuf[0:POOL_BASE, :] = jnp.zeros((POOL_BASE, D_MODEL), F32)
        sbuf[...] = jnp.zeros(sbuf.shape, F32)

    @pl.when(t == 0)
    def _():
        hbuf[POOL_BASE - 16:POOL_BASE, :] = jnp.zeros((16, D_MODEL), F32)

    def run(x1_w, hs_w, x1_r, hs_r):
        def next_norm():
            hs_w[...] = _rms(x1_w[...], gmlp_ref[...]).astype(BF16)

        pieces = _pool_pieces(x_ref, gmix_ref, wp_ref, sc_ref, nb_ref, hbuf, sbuf, x1_w, t, tm)
        _mlp_tile(x1_r, hs_r, w1_ref, w2_ref, gf_ref, o_ref, us, tc, False,
                  between=pieces + [next_norm])

    @pl.when(j == 0)
    def _():
        x1b[...] = jnp.zeros(x1b.shape, F32)
        hsb[...] = jnp.zeros(hsb.shape, BF16)

    @pl.when(j % 2 == 0)
    def _():
        run(x1a, hsa, x1b, hsb)

    @pl.when(j % 2 == 1)
    def _():
        run(x1b, hsb, x1a, hsa)


def _pool_mlp(x, gmix, li, w_pool, scale, gmlp, w1, w2, gain_final, tm=512, tc=512):
    B, T, D = x.shape
    nt = T // tm
    n_tiles = B * nt

    def tile(j):
        jj = jnp.minimum(j, n_tiles - 1)
        return jj // nt, jj % nt

    one = pl.Buffered(1)
    out, nbuf = pl.pallas_call(
        functools.partial(_pool_mlp_kernel, tm=tm, nt=nt, n_tiles=n_tiles, tc=tc),
        grid=(n_tiles + 1,),
        in_specs=[
            pl.BlockSpec((1, tm, D), lambda j: (*tile(j), 0)),
            pl.BlockSpec((None, 1, D), lambda j: (li, 0, 0)),
            pl.BlockSpec((None,) + w_pool.shape[1:], lambda j: (li, 0, 0, 0)),
            pl.BlockSpec((None, 1, D), lambda j: (li, 0, 0)),
            pl.BlockSpec((None, 1, D), lambda j: (li, 0, 0)),
            pl.BlockSpec((None, D, D_FF), lambda j: (li, 0, 0), pipeline_mode=one),
            pl.BlockSpec((None, D_FF, D), lambda j: (li, 0, 0), pipeline_mode=one),
            pl.BlockSpec((1, D), lambda j: (0, 0)),
        ],
        out_specs=[
            pl.BlockSpec((tm, D), lambda j: (jnp.maximum(j - 1, 0), 0)),
            pl.BlockSpec((1, POOL_STATE, D), lambda j: (tile(j)[0], 0, 0)),
        ],
        out_shape=[
            jax.ShapeDtypeStruct((B * T, D), F32),
            jax.ShapeDtypeStruct((B, POOL_STATE, D), F32),
        ],
        scratch_shapes=[
            pltpu.VMEM((tm, D), F32), pltpu.VMEM((tm, D), F32),
            pltpu.VMEM((tm, D), BF16), pltpu.VMEM((tm, D), BF16),
            pltpu.VMEM((tm, D_FF), BF16),
            pltpu.VMEM((POOL_BASE + tm, D), F32),
            pltpu.VMEM((len(POOL_WINDOWS) - 1, POOL_BASE + tm, POOL_GROUP), F32),
        ],
        compiler_params=_cparams("arbitrary"),
        name="pool_mlp",
    )(x, gmix, w_pool, scale, gmlp, w1, w2, gain_final)
    return out.reshape(B, T, D), nbuf


def _pool_sample_kernel(x_ref, st_ref, g_ref, w_ref, sc_ref, o_ref, h_ref):
    x = x_ref[...]
    h = _rms(x, g_ref[...])
    h_ref[...] = h
    for gi, w in enumerate(POOL_WINDOWS):
        sl = slice(gi * POOL_GROUP, (gi + 1) * POOL_GROUP)
        hg = h[:, sl]
        s = hg
        for j in range(1, w):
            s = s + st_ref[:, POOL_STATE - j, sl]
        pooled = s / float(min(SAMPLE_POS0 + 1, w)) - hg
        y = jnp.dot(pooled.astype(BF16), w_ref[gi], preferred_element_type=F32)
        o_ref[:, sl] = x[:, sl] + y * sc_ref[:, sl]


def _pool_sample(x, state, gain, w_pool, scale):
    Bs, D = x.shape
    return pl.pallas_call(
        _pool_sample_kernel,
        out_shape=[jax.ShapeDtypeStruct((Bs, D), F32), jax.ShapeDtypeStruct((Bs, D), F32)],
        compiler_params=pltpu.CompilerParams(vmem_limit_bytes=VMEM_LIMIT),
        name="pool_sample",
    )(x, state, gain, w_pool, scale)


def _mlp_kernel(x_ref, g_ref, w1_ref, w2_ref, gf_ref, o_ref, hs, us, *, tc, final_norm):
    x = x_ref[...]
    hs[...] = _rms(x, g_ref[...]).astype(BF16)
    for c in range(D_FF // tc):
        u = jnp.dot(hs[...], w1_ref[:, c * tc:(c + 1) * tc], preferred_element_type=F32)
        us[:, c * tc:(c + 1) * tc] = jnp.square(jnp.maximum(u, 0.0)).astype(BF16)
    y = x + jnp.dot(us[...], w2_ref[...], preferred_element_type=F32)
    if final_norm:
        y = _rms(y, gf_ref[...])
    o_ref[...] = y


def _mlp(x, gains, w1, w2, gain_final, layer, final_norm, tm, tc=512):
    M, D = x.shape
    return pl.pallas_call(
        functools.partial(_mlp_kernel, tc=tc, final_norm=final_norm),
        grid=(M // tm,),
        in_specs=[
            pl.BlockSpec((tm, D), lambda i: (i, 0)),
            pl.BlockSpec((None, 1, D), lambda i: (layer, 0, 0)),
            pl.BlockSpec((None, D, D_FF), lambda i: (layer, 0, 0), pipeline_mode=pl.Buffered(1)),
            pl.BlockSpec((None, D_FF, D), lambda i: (layer, 0, 0), pipeline_mode=pl.Buffered(1)),
            pl.BlockSpec((1, D), lambda i: (0, 0)),
        ],
        out_specs=pl.BlockSpec((tm, D), lambda i: (i, 0)),
        out_shape=jax.ShapeDtypeStruct((M, D), F32),
        scratch_shapes=[pltpu.VMEM((tm, D), BF16), pltpu.VMEM((tm, D_FF), BF16)],
        compiler_params=_cparams("arbitrary"),
        name="mlp",
    )(x, gains, w1, w2, gain_final)


def _proj_kernel(x_ref, g_ref, w_ref, *refs, tm, nc, scale, tails):
    n_br = len(BRANCHES)
    o_refs = refs[:n_br]
    if tails:
        tail_refs = refs[n_br:2 * n_br]
        hsl, tsl = refs[2 * n_br], refs[2 * n_br + 1]
        lhs = refs[2 * n_br + 2:]
        i = pl.program_id(1)
        nt = pl.num_programs(1)
    else:
        hsl = refs[n_br]
        lhs = refs[n_br + 1:]
    h = _rms(x_ref[0], g_ref[...])
    for c in range(N_SLABS):
        hsl[c] = h[:, c * LANES:(c + 1) * LANES]
    for g, (_, dil) in enumerate(BRANCHES):
        n = tm // dil
        if dil == 1:
            lhs[g][...] = h.astype(BF16)
        else:
            for r in range(dil):
                for c in range(N_SLABS):
                    lhs[g][r * n:(r + 1) * n, c * LANES:(c + 1) * LANES] = (
                        hsl[c, pl.ds(r, n, stride=dil), :].astype(BF16))
        res = jnp.dot(lhs[g][...], w_ref[:, g * nc:(g + 1) * nc], preferred_element_type=F32)
        if scale != 1.0:
            res = res * scale
        for r in range(dil):
            o_refs[g][0, r] = res[r * n:(r + 1) * n].astype(o_refs[g].dtype)

        if tails:
            tail_ref, rows = tail_refs[g], tails[g]
            if rows < tm:
                @pl.when(i == nt - 1)
                def _(tail_ref=tail_ref, rows=rows, res=res):
                    tail_ref[0] = res[tm - rows:tm]
            else:
                @pl.when(i >= nt - rows // tm)
                def _(tail_ref=tail_ref, res=res, dil=dil, n=n):
                    if dil == 1:
                        tail_ref[0] = res
                    else:
                        for c in range(nc // LANES):
                            for r in range(dil):
                                tsl[c, pl.ds(r, n, stride=dil), :] = (
                                    res[r * n:(r + 1) * n, c * LANES:(c + 1) * LANES])
                        for c in range(nc // LANES):
                            tail_ref[0, :, c * LANES:(c + 1) * LANES] = tsl[c]


def _proj(x, gains, gi, w, wi, nc, tm, scale=1.0, tails=()):
    B, T, D = x.shape
    nt = T // tm
    out_specs, out_shape = [], []
    for (_, dil) in BRANCHES:
        out_specs.append(pl.BlockSpec((1, dil, tm // dil, nc), lambda b, i: (b, 0, i, 0)))
        out_shape.append(jax.ShapeDtypeStruct((B, dil, T // dil, nc), BF16))
    scratch = [pltpu.VMEM((N_SLABS, tm, LANES), F32)]
    if tails:
        for rows, (_, dil) in zip(tails, BRANCHES):
            assert rows >= tm or dil == 1
            if rows < tm:
                out_specs.append(pl.BlockSpec((1, rows, nc), lambda b, i: (b, 0, 0)))
            else:
                first = nt - rows // tm
                out_specs.append(pl.BlockSpec(
                    (1, tm, nc), lambda b, i, first=first: (b, jnp.maximum(i - first, 0), 0)))
            out_shape.append(jax.ShapeDtypeStruct((B, rows, nc), F32))
        scratch.append(pltpu.VMEM((nc // LANES, tm, LANES), F32))
    scratch += [pltpu.VMEM((tm, D), BF16) for _ in BRANCHES]
    return pl.pallas_call(
        functools.partial(_proj_kernel, tm=tm, nc=nc, scale=scale, tails=tuple(tails)),
        grid=(B, nt),
        in_specs=[
            pl.BlockSpec((1, tm, D), lambda b, i: (b, i, 0)),
            pl.BlockSpec((None, 1, D), lambda b, i: (gi, 0, 0)),
            pl.BlockSpec((None, D, len(BRANCHES) * nc), lambda b, i: (wi, 0, 0),
                         pipeline_mode=pl.Buffered(1)),
        ],
        out_specs=out_specs,
        out_shape=out_shape,
        scratch_shapes=scratch,
        compiler_params=_cparams("arbitrary", "arbitrary"),
        name="proj_kv" if tails else "proj_q",
    )(x, gains, w)


def _proj_sample_kernel(x_ref, g_ref, w_ref, o_ref, *, scale):
    h = _rms(x_ref[...], g_ref[...]).astype(BF16)
    res = jnp.dot(h, w_ref[...], preferred_element_type=F32)
    if scale != 1.0:
        res = res * scale
    o_ref[...] = res


def _proj_sample(x, gains, gi, w, wi, scale=1.0, tn=1024):
    Bs, D = x.shape
    nc = w.shape[2]
    return pl.pallas_call(
        functools.partial(_proj_sample_kernel, scale=scale),
        grid=(nc // tn,),
        in_specs=[
            pl.BlockSpec((Bs, D), lambda j: (0, 0)),
            pl.BlockSpec((None, 1, D), lambda j: (gi, 0, 0)),
            pl.BlockSpec((None, D, tn), lambda j: (wi, 0, j)),
        ],
        out_specs=pl.BlockSpec((Bs, tn), lambda j: (0, j)),
        out_shape=jax.ShapeDtypeStruct((Bs, nc), F32),
        compiler_params=_cparams("arbitrary"),
        name="proj_sample",
    )(x, gains, w)


def _attn_kernel(q_ref, kv_ref, bias_ref, o_ref, ml_ref, kprev, vprev, *, tq, rb):
    first = pl.program_id(2) == 0
    lane = lax.broadcasted_iota(jnp.int32, (Q_SUB, LANES), 1)
    ones = jnp.ones((2 * Q_SUB, HEAD_DIM), BF16)
    if rb == 1:
        @pl.when(first)
        def _():
            kprev[...] = jnp.zeros(kprev.shape, BF16)
            vprev[...] = jnp.zeros(vprev.shape, BF16)
    for rr in range(rb):
        for s in range(tq // Q_SUB):
            r0 = s * Q_SUB

            def keys(h, col0, s=s, r0=r0, rr=rr):
                cs = slice(col0 + h * HEAD_DIM, col0 + (h + 1) * HEAD_DIM)
                if s > 0:
                    return kv_ref[0, rr, r0 - Q_SUB:r0 + Q_SUB, cs]
                cur = kv_ref[0, rr, 0:Q_SUB, cs]
                if rb == 1:
                    prev = (kprev if col0 == 0 else vprev)[:, h * HEAD_DIM:(h + 1) * HEAD_DIM]
                else:
                    prev = jnp.zeros((Q_SUB, HEAD_DIM), BF16)
                return jnp.concatenate([prev, cur], axis=0)

            if s > 0:
                boff = 0
            elif rb > 1:
                boff = HEADS
            else:
                boff = jnp.where(first, HEADS, 0)
            ml_tile = jnp.zeros((Q_SUB, LANES), F32)
            for h in range(HEADS):
                cs = slice(h * HEAD_DIM, (h + 1) * HEAD_DIM)
                q = q_ref[0, rr, r0:r0 + Q_SUB, cs]
                sc = lax.dot_general(q, keys(h, 0), (((1,), (1,)), ((), ())),
                                     preferred_element_type=F32)
                sc = sc + bias_ref[boff + h]
                m = jnp.max(sc, axis=-1, keepdims=True)
                p = jnp.exp2(sc - m).astype(BF16)
                v_ext = jnp.concatenate([keys(h, D_MODEL), ones], axis=1)
                o = jnp.dot(p, v_ext, preferred_element_type=F32)
                o_ref[0, rr, r0:r0 + Q_SUB, cs] = o[:, :HEAD_DIM].astype(o_ref.dtype)
                ml_tile = jnp.where(lane == h, m, ml_tile)
                ml_tile = jnp.where(lane == HEADS + h, o[:, HEAD_DIM:], ml_tile)
            ml_ref[0, rr, r0:r0 + Q_SUB, :] = ml_tile
    if rb == 1:
        kprev[...] = kv_ref[0, 0, tq - Q_SUB:tq, 0:D_MODEL]
        vprev[...] = kv_ref[0, 0, tq - Q_SUB:tq, D_MODEL:2 * D_MODEL]


def _attn_branch(q, kv, bias, tq, rb):
    B, dil, L, D = q.shape
    assert rb == 1 or tq == L
    return pl.pallas_call(
        functools.partial(_attn_kernel, tq=tq, rb=rb),
        grid=(B, dil // rb, L // tq),
        in_specs=[
            pl.BlockSpec((1, rb, tq, D), lambda b, r, i: (b, r, i, 0)),
            pl.BlockSpec((1, rb, tq, 2 * D), lambda b, r, i: (b, r, i, 0)),
            pl.BlockSpec(bias.shape, lambda b, r, i: (0, 0, 0)),
        ],
        out_specs=[
            pl.BlockSpec((1, rb, tq, D), lambda b, r, i: (b, r, i, 0)),
            pl.BlockSpec((1, rb, tq, LANES), lambda b, r, i: (b, r, i, 0)),
        ],
        out_shape=[
            jax.ShapeDtypeStruct((B, dil, L, D), BF16),
            jax.ShapeDtypeStruct((B, dil, L, LANES), F32),
        ],
        scratch_shapes=[pltpu.VMEM((Q_SUB, D), BF16), pltpu.VMEM((Q_SUB, D), BF16)],
        compiler_params=_cparams("arbitrary", "arbitrary", "arbitrary"),
        name=f"attn_d{dil}",
    )(q, kv, bias)


def _merge_pieces(x_ref, o_refs, ml_refs, wo_ref, osls, wsl, lhs, x1_ref, tm):
    def weights():
        tiles = []
        for g, (_, dil) in enumerate(BRANCHES):
            if dil == 1:
                tiles.append(ml_refs[g][0, 0])
            else:
                n = tm // dil
                for r in range(dil):
                    wsl[g, pl.ds(r, n, stride=dil), :] = ml_refs[g][0, r]
                tiles.append(wsl[g])
        m = jnp.maximum(jnp.maximum(tiles[0], tiles[1]), tiles[2])
        us = [jnp.exp2(tl - m) for tl in tiles]
        ls = [pltpu.roll(tl, LANES - HEADS, axis=1) for tl in tiles]
        den = us[0] * ls[0] + us[1] * ls[1] + us[2] * ls[2]
        for g in range(len(BRANCHES)):
            wsl[g] = us[g] / den

    def interleave(g, r_lo, r_hi):
        dil = BRANCHES[g][1]
        n = tm // dil
        for r in range(r_lo, r_hi):
            for c in range(N_SLABS):
                osls[g][c, pl.ds(r, n, stride=dil), :] = (
                    o_refs[g][0, r, :, c * LANES:(c + 1) * LANES].astype(F32))

    def combine(c_lo, c_hi):
        for c in range(c_lo, c_hi):
            cs = slice(c * LANES, (c + 1) * LANES)
            merged = wsl[0, :, c:c + 1] * o_refs[0][0, 0, :, cs].astype(F32)
            for g in range(1, len(BRANCHES)):
                merged = merged + wsl[g, :, c:c + 1] * osls[g][c]
            lhs[:, cs] = merged.astype(BF16)

    def project():
        x1_ref[...] = x_ref[0] + jnp.dot(lhs[...], wo_ref[...], preferred_element_type=F32)

    pieces = [weights]
    for g, (_, dil) in enumerate(BRANCHES):
        if dil > 1:
            half = max(dil // 2, 1) if dil >= 8 else dil
            pieces += [functools.partial(interleave, g, lo, min(lo + half, dil))
                       for lo in range(0, dil, half)]
    pieces += [functools.partial(combine, 0, N_SLABS // 2),
               functools.partial(combine, N_SLABS // 2, N_SLABS), project]
    return pieces


def _merge_mlp_kernel(x_ref, o0_ref, o1_ref, o2_ref, ml0_ref, ml1_ref, ml2_ref, wo_ref,
                      gmlp_ref, w1_ref, w2_ref, gf_ref, out_ref,
                      x1a, x1b, hsa, hsb, us, osl1, osl2, wsl, lhs, *, tm, tc, final_norm):
    j = pl.program_id(0)

    def run(x1_w, hs_w, x1_r, hs_r):
        def next_norm():
            hs_w[...] = _rms(x1_w[...], gmlp_ref[...]).astype(BF16)

        pieces = _merge_pieces(x_ref, (o0_ref, o1_ref, o2_ref), (ml0_ref, ml1_ref, ml2_ref),
                               wo_ref, (None, osl1, osl2), wsl, lhs, x1_w, tm)
        _mlp_tile(x1_r, hs_r, w1_ref, w2_ref, gf_ref, out_ref, us, tc, final_norm,
                  between=pieces + [next_norm])

    @pl.when(j == 0)
    def _():
        x1b[...] = jnp.zeros(x1b.shape, F32)
        hsb[...] = jnp.zeros(hsb.shape, BF16)

    @pl.when(j % 2 == 0)
    def _():
        run(x1a, hsa, x1b, hsb)

    @pl.when(j % 2 == 1)
    def _():
        run(x1b, hsb, x1a, hsa)


def _merge_mlp(x, os_, mls, wo, wi, gmlp, w1, w2, gain_final, li, final_norm, tm=512, tc=512):
    B, T, D = x.shape
    nt = T // tm
    n_tiles = B * nt

    def tile(j):
        jj = jnp.minimum(j, n_tiles - 1)
        return jj // nt, jj % nt

    one = pl.Buffered(1)
    in_specs = [pl.BlockSpec((1, tm, D), lambda j: (*tile(j), 0))]
    for (_, dil) in BRANCHES:
        in_specs.append(pl.BlockSpec((1, dil, tm // dil, D), lambda j: (tile(j)[0], 0, tile(j)[1], 0)))
    for (_, dil) in BRANCHES:
        in_specs.append(pl.BlockSpec((1, dil, tm // dil, LANES),
                                     lambda j: (tile(j)[0], 0, tile(j)[1], 0)))
    in_specs += [
        pl.BlockSpec((None, D, D), lambda j: (wi, 0, 0), pipeline_mode=one),
        pl.BlockSpec((None, 1, D), lambda j: (li, 0, 0)),
        pl.BlockSpec((None, D, D_FF), lambda j: (li, 0, 0), pipeline_mode=one),
        pl.BlockSpec((None, D_FF, D), lambda j: (li, 0, 0), pipeline_mode=one),
        pl.BlockSpec((1, D), lambda j: (0, 0)),
    ]
    out = pl.pallas_call(
        functools.partial(_merge_mlp_kernel, tm=tm, tc=tc, final_norm=final_norm),
        grid=(n_tiles + 1,),
        in_specs=in_specs,
        out_specs=pl.BlockSpec((tm, D), lambda j: (jnp.maximum(j - 1, 0), 0)),
        out_shape=jax.ShapeDtypeStruct((B * T, D), F32),
        scratch_shapes=[
            pltpu.VMEM((tm, D), F32), pltpu.VMEM((tm, D), F32),
            pltpu.VMEM((tm, D), BF16), pltpu.VMEM((tm, D), BF16),
            pltpu.VMEM((tm, D_FF), BF16),
            pltpu.VMEM((N_SLABS, tm, LANES), F32), pltpu.VMEM((N_SLABS, tm, LANES), F32),
            pltpu.VMEM((len(BRANCHES), tm, LANES), F32),
            pltpu.VMEM((tm, D), BF16),
        ],
        compiler_params=_cparams("arbitrary"),
        name="merge_mlp",
    )(x, *os_, *mls, wo, gmlp, w1, w2, gain_final)
    return out.reshape(B, T, D)


def _attn_sample_kernel(q_ref, kvn_ref, c0_ref, c1_ref, c2_ref, bias_ref, o_ref, *, nb):
    caches = (c0_ref, c1_ref, c2_ref)
    for i in range(nb):
        sc_c, sc_n = [], []
        for g in range(len(BRANCHES)):
            q = q_ref[i, g]
            kc = caches[g][i, :, 0]
            sc_c.append(jnp.sum(kc * q[None], axis=-1, keepdims=True) + bias_ref[g, 0:N_KEYS - 1])
            sc_n.append(jnp.sum(kvn_ref[i, g, 0] * q, axis=-1, keepdims=True)
                        + bias_ref[g, N_KEYS - 1])
        m = sc_n[0]
        for g in range(len(BRANCHES)):
            m = jnp.maximum(m, jnp.maximum(jnp.max(sc_c[g], axis=0), sc_n[g]))
        num = jnp.zeros((HEADS, HEAD_DIM), F32)
        den = jnp.zeros((HEADS, HEAD_DIM), F32)
        for g in range(len(BRANCHES)):
            p_c = jnp.exp(sc_c[g] - m[None])
            p_n = jnp.exp(sc_n[g] - m)
            den = den + jnp.sum(p_c, axis=0) + p_n
            num = num + jnp.sum(p_c * caches[g][i, :, 1], axis=0) + p_n * kvn_ref[i, g, 1]
        o_ref[i] = num / den


def _attn_sample(q, kvn, caches, bias, nb=4):
    Bs = q.shape[0]
    in_specs = [
        pl.BlockSpec((nb,) + q.shape[1:], lambda b: (b, 0, 0, 0)),
        pl.BlockSpec((nb,) + kvn.shape[1:], lambda b: (b, 0, 0, 0, 0)),
    ]
    for c in caches:
        in_specs.append(pl.BlockSpec((nb, N_KEYS - 1, None, 2, HEADS, HEAD_DIM),
                                     lambda b: (b, 0, 0, 0, 0, 0)))
    in_specs.append(pl.BlockSpec(bias.shape, lambda b: (0, 0, 0, 0)))
    return pl.pallas_call(
        functools.partial(_attn_sample_kernel, nb=nb),
        grid=(Bs // nb,),
        in_specs=in_specs,
        out_specs=pl.BlockSpec((nb, HEADS, HEAD_DIM), lambda b: (b, 0, 0)),
        out_shape=jax.ShapeDtypeStruct((Bs, HEADS, HEAD_DIM), F32),
        compiler_params=_cparams("arbitrary"),
        name="attn_sample",
    )(q, kvn, *caches, bias)


def _out_sample_kernel(x_ref, o_ref, wo_ref, out_ref):
    out_ref[...] = x_ref[...] + jnp.dot(o_ref[...].astype(BF16), wo_ref[...],
                                        preferred_element_type=F32)


def _out_sample(x, o, wo):
    return pl.pallas_call(
        _out_sample_kernel,
        out_shape=jax.ShapeDtypeStruct(x.shape, F32),
        compiler_params=pltpu.CompilerParams(vmem_limit_bytes=VMEM_LIMIT),
        name="out_sample",
    )(x, o, wo)


def _t5_bucket(dist):
    max_exact = NUM_BUCKETS // 2
    df = jnp.maximum(dist, 1).astype(F32)
    large = max_exact + (jnp.log(df / max_exact) / math.log(MAX_DISTANCE / max_exact)
                         * (NUM_BUCKETS - max_exact)).astype(jnp.int32)
    large = jnp.minimum(large, NUM_BUCKETS - 1)
    return jnp.where(dist < max_exact, dist, large)


def _bias_tables(rel_bias):
    out = []
    for g, (w, d) in enumerate(BRANCHES):
        dist = jnp.arange(N_KEYS, dtype=jnp.int32) * d
        out.append(rel_bias[_t5_bucket(dist)][:, g * HEADS:(g + 1) * HEADS])
    return out


def _block_bias(tab):
    pad_lo = jnp.full((HEADS, Q_SUB - 1), MASKED, F32)
    pad_hi = jnp.full((HEADS, Q_SUB), MASKED, F32)
    f = jnp.concatenate([pad_lo, tab[::-1].T, pad_hi], axis=1)
    width = f.shape[1]
    flat = jnp.tile(f, (1, Q_SUB))[:, :Q_SUB * (width - 1)]
    toep = flat.reshape(HEADS, Q_SUB, width - 1)
    return toep[:, :, Q_SUB - 1:Q_SUB - 1 + 2 * Q_SUB]


def kernel(x_prompt, x_sample, state_pool, cache_kv_w128, cache_kv_w512, cache_kv_w2048,
           norm_mix, pool_w, pool_scale, norm_mlp, mlp_in, mlp_out, norm_kv, w_kv, w_q, w_o,
           rel_bias, norm_final):
    B, T, D = x_prompt.shape
    Bs = x_sample.shape[0]
    n_a = pool_w.shape[0]
    depth = norm_mix.shape[0]
    n_br = len(BRANCHES)
    qk_scale = HEAD_DIM ** -0.5

    pool_w_b = pool_w.astype(BF16)
    mlp_in_b = mlp_in.astype(BF16)
    mlp_out_b = mlp_out.astype(BF16)
    w_kv_b = w_kv.astype(BF16)[None]
    w_q_b = w_q.astype(BF16)
    w_o_b = w_o.astype(BF16)
    row = lambda v: v.reshape(1, D)
    g_mix = norm_mix.reshape(depth, 1, D)
    g_mlp = norm_mlp.reshape(depth, 1, D)
    g_kv = norm_kv.reshape(1, 1, D)
    g_scale = pool_scale.reshape(n_a, 1, D)

    tabs = _bias_tables(rel_bias)
    no_prev = np.arange(2 * Q_SUB)[None, None, :] < Q_SUB
    blk_bias = []
    for t in tabs:
        bb = _block_bias(t * LOG2E)
        blk_bias.append(jnp.concatenate([bb, jnp.where(no_prev, MASKED, bb)], axis=0))
    smp_bias = jnp.stack([jnp.broadcast_to(t[::-1][:, :, None], (N_KEYS, HEADS, HEAD_DIM))
                          for t in tabs])

    caches = [c.reshape(Bs, N_KEYS - 1, d, 2, HEADS, HEAD_DIM)
              for c, (w, d) in zip((cache_kv_w128, cache_kv_w512, cache_kv_w2048), BRANCHES)]

    xp = x_prompt
    xs = x_sample.reshape(Bs, D)
    pool_p, pool_s, kv_p = [], [], []
    kvs = kvn = None
    for l in range(depth):
        last = l == depth - 1
        if l < n_a:
            xp, nbuf = _pool_mlp(xp, g_mix, l, pool_w_b, g_scale, g_mlp, mlp_in_b, mlp_out_b,
                                 row(norm_final))
            pool_p.append(nbuf)
            xs, hs = _pool_sample(xs, state_pool[l], row(norm_mix[l]), pool_w_b[l], row(pool_scale[l]))
            pool_s.append(jnp.concatenate([state_pool[l][:, 1:], hs[:, None, :]], axis=1))
        else:
            lb = l - n_a
            if l == n_a:
                tails = tuple(min(w, T) for (w, _) in BRANCHES)
                res = _proj(xp, g_kv, 0, w_kv_b, 0, 2 * D, 256, tails=tails)
                kvs = res[:n_br]
                kv_p = [t.reshape(B, rows, 2, HEADS, HEAD_DIM) for t, rows in zip(res[n_br:], tails)]
                kvn = _proj_sample(xs, g_kv, 0, w_kv_b, 0)
            qs_p = _proj(xp, g_mix, l, w_q_b, lb, D, 1024, scale=qk_scale * LOG2E)
            os_, lses = [], []
            for g, (w, d) in enumerate(BRANCHES):
                L = T // d
                tq, rb = (L, min(d, ATTN_ROWS // L)) if L < ATTN_ROWS else (ATTN_ROWS, 1)
                o_g, ml_g = _attn_branch(qs_p[g], kvs[g], blk_bias[g], tq, rb)
                os_.append(o_g)
                lses.append(ml_g)
            xp = _merge_mlp(xp, os_, lses, w_o_b, lb, g_mlp, mlp_in_b, mlp_out_b, row(norm_final),
                            l, last)
            qs = _proj_sample(xs, g_mix, l, w_q_b, lb, scale=qk_scale)
            o_s = _attn_sample(qs.reshape(Bs, n_br, HEADS, HEAD_DIM),
                               kvn.reshape(Bs, n_br, 2, HEADS, HEAD_DIM), caches, smp_bias)
            xs = _out_sample(xs, o_s.reshape(Bs, D), w_o_b[lb])
        xs = _mlp(xs, g_mlp, mlp_in_b, mlp_out_b, row(norm_final), l, last, tm=Bs)

    kv_s = [kvn[:, g * 2 * D:(g + 1) * 2 * D].reshape(Bs, 1, 2, HEADS, HEAD_DIM) for g in range(n_br)]
    return (xp, xs.reshape(Bs, 1, D), jnp.stack(pool_p), jnp.stack(pool_s),
            kv_p[0], kv_s[0], kv_p[1], kv_s[1], kv_p[2], kv_s[2])
```

```python
import functools
import math

import jax
import jax.numpy as jnp
import numpy as np
from jax import lax
from jax.experimental import pallas as pl
from jax.experimental.pallas import tpu as pltpu

F32 = jnp.float32
BF16 = jnp.bfloat16

D_MODEL = 1024
HEADS = 8
HEAD_DIM = 128
D_FF = 4 * D_MODEL
POOL_WINDOWS = (2, 4, 8, 16)
POOL_GROUP = D_MODEL // len(POOL_WINDOWS)
POOL_STATE = max(POOL_WINDOWS) - 1
BRANCHES = ((128, 1), (512, 4), (2048, 16))
N_KEYS = 129
NUM_BUCKETS = 32
MAX_DISTANCE = 2048
SAMPLE_POS0 = 8192
EPS = 1e-6
MASKED = -1e30
LOG2E = math.log2(math.e)

LANES = 128
N_SLABS = D_MODEL // LANES
Q_SUB = 128
ATTN_ROWS = 2048
VMEM_LIMIT = 56 * 1024 * 1024


def _cparams(*sem):
    return pltpu.CompilerParams(dimension_semantics=sem, vmem_limit_bytes=VMEM_LIMIT)


def _rms(x, g):
    ms = jnp.mean(x * x, axis=-1, keepdims=True)
    return x * lax.rsqrt(ms + EPS) * g


POOL_BASE = 24
POOL_LO = 8


def _pool_pieces(x_ref, g_ref, w_ref, sc_ref, nb_ref, hbuf, sbuf, x1_ref, t, tm):
    end = POOL_BASE + tm

    def norm():
        hbuf[POOL_BASE:end, :] = _rms(x_ref[0], g_ref[...])

    def group(gi, w):
        sl = slice(gi * POOL_GROUP, (gi + 1) * POOL_GROUP)
        n_stage = w.bit_length() - 1
        s = None
        for si in range(n_stage):
            sh = 1 << si
            lo = POOL_BASE if si == n_stage - 1 else POOL_LO
            if si == 0:
                val = hbuf[lo:end, sl] + hbuf[lo - sh:end - sh, sl]
            else:
                val = sbuf[si - 1, lo:end, :] + sbuf[si - 1, lo - sh:end - sh, :]
            if si == n_stage - 1:
                s = val
            else:
                sbuf[si, lo:end, :] = val
        pos = t * tm + lax.broadcasted_iota(jnp.int32, (tm, 1), 0)
        cnt = jnp.minimum(pos + 1, w).astype(F32)
        pooled = s / cnt - hbuf[POOL_BASE:end, sl]
        y = jnp.dot(pooled.astype(BF16), w_ref[gi], preferred_element_type=F32)
        x1_ref[:, sl] = x_ref[0, :, sl] + y * sc_ref[:, sl]

    def carry():
        nb_ref[0] = hbuf[end - POOL_STATE:end, :]
        hbuf[POOL_BASE - 16:POOL_BASE, :] = hbuf[end - 16:end, :]

    pieces = [norm]
    pieces += [functools.partial(group, gi, w) for gi, w in enumerate(POOL_WINDOWS)]
    return pieces + [carry]


def _mlp_tile(x1_ref, hs_ref, w1_ref, w2_ref, gf_ref, o_ref, us, tc, final_norm, between=()):
    between = list(between)
    n_chunks = D_FF // tc
    assert len(between) <= n_chunks
    for c in range(n_chunks):
        if c < len(between):
            between[c]()
        u = jnp.dot(hs_ref[...], w1_ref[:, c * tc:(c + 1) * tc], preferred_element_type=F32)
        us[:, c * tc:(c + 1) * tc] = jnp.square(jnp.maximum(u, 0.0)).astype(BF16)
    y = x1_ref[...] + jnp.dot(us[...], w2_ref[...], preferred_element_type=F32)
    if final_norm:
        y = _rms(y, gf_ref[...])
    o_ref[...] = y


def _run_pipelined(slots, xs1_ref, gmlp_ref, o_ref, os_ref, run):
    (x1a, hsa), (x1b, hsb) = slots
    j = pl.program_id(0)
    bs = xs1_ref.shape[0]

    @pl.when(j == 0)
    def _():
        x1b[...] = jnp.zeros(x1b.shape, F32)
        hsb[...] = jnp.zeros(hsb.shape, BF16)
        xs1 = xs1_ref[...]
        x1b[0:bs, :] = xs1
        hsb[0:bs, :] = _rms(xs1, gmlp_ref[...]).astype(BF16)

    @pl.when(j % 2 == 0)
    def _():
        run(x1a, hsa, x1b, hsb)

    @pl.when(j % 2 == 1)
    def _():
        run(x1b, hsb, x1a, hsa)

    @pl.when(j == 0)
    def _():
        os_ref[...] = o_ref[0:bs, :]


def _pool_mlp_kernel(x_ref, xs1_ref, gmix_ref, wp_ref, sc_ref, gmlp_ref, w1_ref, w2_ref, gf_ref,
                     o_ref, os_ref, nb_ref, x1a, x1b, hsa, hsb, us, hbuf, sbuf,
                     *, tm, nt, n_tiles, tc):
    j = pl.program_id(0)
    t = jnp.minimum(j, n_tiles - 1) % nt

    @pl.when(j == 0)
    def _():
        hbuf[0:POOL_BASE, :] = jnp.zeros((POOL_BASE, D_MODEL), F32)
        sbuf[...] = jnp.zeros(sbuf.shape, F32)

    @pl.when(t == 0)
    def _():
        hbuf[POOL_BASE - 16:POOL_BASE, :] = jnp.zeros((16, D_MODEL), F32)

    def run(x1_w, hs_w, x1_r, hs_r):
        def next_norm():
            hs_w[...] = _rms(x1_w[...], gmlp_ref[...]).astype(BF16)

        pieces = _pool_pieces(x_ref, gmix_ref, wp_ref, sc_ref, nb_ref, hbuf, sbuf, x1_w, t, tm)
        _mlp_tile(x1_r, hs_r, w1_ref, w2_ref, gf_ref, o_ref, us, tc, False,
                  between=pieces + [next_norm])

    _run_pipelined(((x1a, hsa), (x1b, hsb)), xs1_ref, gmlp_ref, o_ref, os_ref, run)


def _pool_mlp(x, xs1, gmix, li, w_pool, scale, gmlp, w1, w2, gain_final, tm=512, tc=512):
    B, T, D = x.shape
    Bs = xs1.shape[0]
    nt = T // tm
    n_tiles = B * nt

    def tile(j):
        jj = jnp.minimum(j, n_tiles - 1)
        return jj // nt, jj % nt

    one = pl.Buffered(1)
    out, out_s, nbuf = pl.pallas_call(
        functools.partial(_pool_mlp_kernel, tm=tm, nt=nt, n_tiles=n_tiles, tc=tc),
        grid=(n_tiles + 1,),
        in_specs=[
            pl.BlockSpec((1, tm, D), lambda j: (*tile(j), 0)),
            pl.BlockSpec((Bs, D), lambda j: (0, 0)),
            pl.BlockSpec((None, 1, D), lambda j: (li, 0, 0)),
            pl.BlockSpec((None,) + w_pool.shape[1:], lambda j: (li, 0, 0, 0)),
            pl.BlockSpec((None, 1, D), lambda j: (li, 0, 0)),
            pl.BlockSpec((None, 1, D), lambda j: (li, 0, 0)),
            pl.BlockSpec((None, D, D_FF), lambda j: (li, 0, 0), pipeline_mode=one),
            pl.BlockSpec((None, D_FF, D), lambda j: (li, 0, 0), pipeline_mode=one),
            pl.BlockSpec((1, D), lambda j: (0, 0)),
        ],
        out_specs=[
            pl.BlockSpec((tm, D), lambda j: (jnp.maximum(j - 1, 0), 0)),
            pl.BlockSpec((Bs, D), lambda j: (0, 0)),
            pl.BlockSpec((1, POOL_STATE, D), lambda j: (tile(j)[0], 0, 0)),
        ],
        out_shape=[
            jax.ShapeDtypeStruct((B * T, D), F32),
            jax.ShapeDtypeStruct((Bs, D), F32),
            jax.ShapeDtypeStruct((B, POOL_STATE, D), F32),
        ],
        scratch_shapes=[
            pltpu.VMEM((tm, D), F32), pltpu.VMEM((tm, D), F32),
            pltpu.VMEM((tm, D), BF16), pltpu.VMEM((tm, D), BF16),
            pltpu.VMEM((tm, D_FF), BF16),
            pltpu.VMEM((POOL_BASE + tm, D), F32),
            pltpu.VMEM((len(POOL_WINDOWS) - 1, POOL_BASE + tm, POOL_GROUP), F32),
        ],
        compiler_params=_cparams("arbitrary"),
        name="pool_mlp",
    )(x, xs1, gmix, w_pool, scale, gmlp, w1, w2, gain_final)
    return out.reshape(B, T, D), out_s, nbuf


def _pool_sample_kernel(x_ref, st_ref, g_ref, w_ref, sc_ref, o_ref, h_ref):
    x = x_ref[...]
    h = _rms(x, g_ref[...])
    h_ref[...] = h
    for gi, w in enumerate(POOL_WINDOWS):
        sl = slice(gi * POOL_GROUP, (gi + 1) * POOL_GROUP)
        hg = h[:, sl]
        s = hg
        for j in range(1, w):
            s = s + st_ref[:, POOL_STATE - j, sl]
        pooled = s / float(min(SAMPLE_POS0 + 1, w)) - hg
        y = jnp.dot(pooled.astype(BF16), w_ref[gi], preferred_element_type=F32)
        o_ref[:, sl] = x[:, sl] + y * sc_ref[:, sl]


def _pool_sample(x, state, gain, w_pool, scale):
    Bs, D = x.shape
    return pl.pallas_call(
        _pool_sample_kernel,
        out_shape=[jax.ShapeDtypeStruct((Bs, D), F32), jax.ShapeDtypeStruct((Bs, D), F32)],
        compiler_params=pltpu.CompilerParams(vmem_limit_bytes=VMEM_LIMIT),
        name="pool_sample",
    )(x, state, gain, w_pool, scale)


def _proj_kernel(x_ref, g_ref, w_ref, *refs, tm, nc, scale, tails):
    n_br = len(BRANCHES)
    o_refs = refs[:n_br]
    if tails:
        tail_refs = refs[n_br:2 * n_br]
        hsl, tsl = refs[2 * n_br], refs[2 * n_br + 1]
        lhs = refs[2 * n_br + 2:]
        i = pl.program_id(1)
        nt = pl.num_programs(1)
    else:
        hsl = refs[n_br]
        lhs = refs[n_br + 1:]
    h = _rms(x_ref[0], g_ref[...])
    for c in range(N_SLABS):
        hsl[c] = h[:, c * LANES:(c + 1) * LANES]
    for g, (_, dil) in enumerate(BRANCHES):
        n = tm // dil
        if dil == 1:
            lhs[g][...] = h.astype(BF16)
        else:
            for r in range(dil):
                for c in range(N_SLABS):
                    lhs[g][r * n:(r + 1) * n, c * LANES:(c + 1) * LANES] = (
                        hsl[c, pl.ds(r, n, stride=dil), :].astype(BF16))
        res = jnp.dot(lhs[g][...], w_ref[:, g * nc:(g + 1) * nc], preferred_element_type=F32)
        if scale != 1.0:
            res = res * scale
        for r in range(dil):
            o_refs[g][0, r] = res[r * n:(r + 1) * n].astype(o_refs[g].dtype)

        if tails:
            tail_ref, rows = tail_refs[g], tails[g]
            if rows < tm:
                @pl.when(i == nt - 1)
                def _(tail_ref=tail_ref, rows=rows, res=res):
                    tail_ref[0] = res[tm - rows:tm]
            else:
                @pl.when(i >= nt - rows // tm)
                def _(tail_ref=tail_ref, res=res, dil=dil, n=n):
                    if dil == 1:
                        tail_ref[0] = res
                    else:
                        for c in range(nc // LANES):
                            for r in range(dil):
                                tsl[c, pl.ds(r, n, stride=dil), :] = (
                                    res[r * n:(r + 1) * n, c * LANES:(c + 1) * LANES])
                        for c in range(nc // LANES):
                            tail_ref[0, :, c * LANES:(c + 1) * LANES] = tsl[c]


def _proj(x, gains, gi, w, wi, nc, tm, scale=1.0, tails=()):
    B, T, D = x.shape
    nt = T // tm
    out_specs, out_shape = [], []
    for (_, dil) in BRANCHES:
        out_specs.append(pl.BlockSpec((1, dil, tm // dil, nc), lambda b, i: (b, 0, i, 0)))
        out_shape.append(jax.ShapeDtypeStruct((B, dil, T // dil, nc), BF16))
    scratch = [pltpu.VMEM((N_SLABS, tm, LANES), F32)]
    if tails:
        for rows, (_, dil) in zip(tails, BRANCHES):
            assert rows >= tm or dil == 1
            if rows < tm:
                out_specs.append(pl.BlockSpec((1, rows, nc), lambda b, i: (b, 0, 0)))
            else:
                first = nt - rows // tm
                out_specs.append(pl.BlockSpec(
                    (1, tm, nc), lambda b, i, first=first: (b, jnp.maximum(i - first, 0), 0)))
            out_shape.append(jax.ShapeDtypeStruct((B, rows, nc), F32))
        scratch.append(pltpu.VMEM((nc // LANES, tm, LANES), F32))
    scratch += [pltpu.VMEM((tm, D), BF16) for _ in BRANCHES]
    return pl.pallas_call(
        functools.partial(_proj_kernel, tm=tm, nc=nc, scale=scale, tails=tuple(tails)),
        grid=(B, nt),
        in_specs=[
            pl.BlockSpec((1, tm, D), lambda b, i: (b, i, 0)),
            pl.BlockSpec((None, 1, D), lambda b, i: (gi, 0, 0)),
            pl.BlockSpec((None, D, len(BRANCHES) * nc), lambda b, i: (wi, 0, 0),
                         pipeline_mode=pl.Buffered(1)),
        ],
        out_specs=out_specs,
        out_shape=out_shape,
        scratch_shapes=scratch,
        compiler_params=_cparams("arbitrary", "arbitrary"),
        name="proj_kv" if tails else "proj_q",
    )(x, gains, w)


def _proj_sample_kernel(x_ref, g_ref, w_ref, o_ref, *, scale):
    h = _rms(x_ref[...], g_ref[...]).astype(BF16)
    res = jnp.dot(h, w_ref[...], preferred_element_type=F32)
    if scale != 1.0:
        res = res * scale
    o_ref[...] = res


def _proj_sample(x, gains, gi, w, wi, scale=1.0, tn=1024):
    Bs, D = x.shape
    nc = w.shape[2]
    return pl.pallas_call(
        functools.partial(_proj_sample_kernel, scale=scale),
        grid=(nc // tn,),
        in_specs=[
            pl.BlockSpec((Bs, D), lambda j: (0, 0)),
            pl.BlockSpec((None, 1, D), lambda j: (gi, 0, 0)),
            pl.BlockSpec((None, D, tn), lambda j: (wi, 0, j)),
        ],
        out_specs=pl.BlockSpec((Bs, tn), lambda j: (0, j)),
        out_shape=jax.ShapeDtypeStruct((Bs, nc), F32),
        compiler_params=_cparams("arbitrary"),
        name="proj_sample",
    )(x, gains, w)


def _attn_kernel(q_ref, kv_ref, bias_ref, o_ref, ml_ref, kprev, vprev, *, tq, rb):
    first = pl.program_id(2) == 0
    lane = lax.broadcasted_iota(jnp.int32, (Q_SUB, LANES), 1)
    ones = jnp.ones((2 * Q_SUB, HEAD_DIM), BF16)
    if rb == 1:
        @pl.when(first)
        def _():
            kprev[...] = jnp.zeros(kprev.shape, BF16)
            vprev[...] = jnp.zeros(vprev.shape, BF16)
    for rr in range(rb):
        for s in range(tq // Q_SUB):
            r0 = s * Q_SUB

            def keys(h, col0, s=s, r0=r0, rr=rr):
                cs = slice(col0 + h * HEAD_DIM, col0 + (h + 1) * HEAD_DIM)
                if s > 0:
                    return kv_ref[0, rr, r0 - Q_SUB:r0 + Q_SUB, cs]
                cur = kv_ref[0, rr, 0:Q_SUB, cs]
                if rb == 1:
                    prev = (kprev if col0 == 0 else vprev)[:, h * HEAD_DIM:(h + 1) * HEAD_DIM]
                else:
                    prev = jnp.zeros((Q_SUB, HEAD_DIM), BF16)
                return jnp.concatenate([prev, cur], axis=0)

            if s > 0:
                boff = 0
            elif rb > 1:
                boff = HEADS
            else:
                boff = jnp.where(first, HEADS, 0)
            ml_tile = jnp.zeros((Q_SUB, LANES), F32)
            for h in range(HEADS):
                cs = slice(h * HEAD_DIM, (h + 1) * HEAD_DIM)
                q = q_ref[0, rr, r0:r0 + Q_SUB, cs]
                sc = lax.dot_general(q, keys(h, 0), (((1,), (1,)), ((), ())),
                                     preferred_element_type=F32)
                sc = sc + bias_ref[boff + h]
                m = jnp.max(sc, axis=-1, keepdims=True)
                p = jnp.exp2(sc - m).astype(BF16)
                v_ext = jnp.concatenate([keys(h, D_MODEL), ones], axis=1)
                o = jnp.dot(p, v_ext, preferred_element_type=F32)
                o_ref[0, rr, r0:r0 + Q_SUB, cs] = o[:, :HEAD_DIM].astype(o_ref.dtype)
                ml_tile = jnp.where(lane == h, m, ml_tile)
                ml_tile = jnp.where(lane == HEADS + h, o[:, HEAD_DIM:], ml_tile)
            ml_ref[0, rr, r0:r0 + Q_SUB, :] = ml_tile
    if rb == 1:
        kprev[...] = kv_ref[0, 0, tq - Q_SUB:tq, 0:D_MODEL]
        vprev[...] = kv_ref[0, 0, tq - Q_SUB:tq, D_MODEL:2 * D_MODEL]


def _attn_branch(q, kv, bias, tq, rb):
    B, dil, L, D = q.shape
    assert rb == 1 or tq == L
    return pl.pallas_call(
        functools.partial(_attn_kernel, tq=tq, rb=rb),
        grid=(B, dil // rb, L // tq),
        in_specs=[
            pl.BlockSpec((1, rb, tq, D), lambda b, r, i: (b, r, i, 0)),
            pl.BlockSpec((1, rb, tq, 2 * D), lambda b, r, i: (b, r, i, 0)),
            pl.BlockSpec(bias.shape, lambda b, r, i: (0, 0, 0)),
        ],
        out_specs=[
            pl.BlockSpec((1, rb, tq, D), lambda b, r, i: (b, r, i, 0)),
            pl.BlockSpec((1, rb, tq, LANES), lambda b, r, i: (b, r, i, 0)),
        ],
        out_shape=[
            jax.ShapeDtypeStruct((B, dil, L, D), BF16),
            jax.ShapeDtypeStruct((B, dil, L, LANES), F32),
        ],
        scratch_shapes=[pltpu.VMEM((Q_SUB, D), BF16), pltpu.VMEM((Q_SUB, D), BF16)],
        compiler_params=_cparams("arbitrary", "arbitrary", "arbitrary"),
        name=f"attn_d{dil}",
    )(q, kv, bias)


def _merge_pieces(x_ref, o_refs, ml_refs, wo_ref, osls, wsl, lhs, x1_ref, tm):
    def weights():
        tiles = []
        for g, (_, dil) in enumerate(BRANCHES):
            if dil == 1:
                tiles.append(ml_refs[g][0, 0])
            else:
                n = tm // dil
                for r in range(dil):
                    wsl[g, pl.ds(r, n, stride=dil), :] = ml_refs[g][0, r]
                tiles.append(wsl[g])
        m = jnp.maximum(jnp.maximum(tiles[0], tiles[1]), tiles[2])
        us = [jnp.exp2(tl - m) for tl in tiles]
        ls = [pltpu.roll(tl, LANES - HEADS, axis=1) for tl in tiles]
        den = us[0] * ls[0] + us[1] * ls[1] + us[2] * ls[2]
        for g in range(len(BRANCHES)):
            wsl[g] = us[g] / den

    def interleave(g, r_lo, r_hi):
        dil = BRANCHES[g][1]
        n = tm // dil
        for r in range(r_lo, r_hi):
            for c in range(N_SLABS):
                osls[g][c, pl.ds(r, n, stride=dil), :] = (
                    o_refs[g][0, r, :, c * LANES:(c + 1) * LANES].astype(F32))

    def combine(c_lo, c_hi):
        for c in range(c_lo, c_hi):
            cs = slice(c * LANES, (c + 1) * LANES)
            merged = wsl[0, :, c:c + 1] * o_refs[0][0, 0, :, cs].astype(F32)
            for g in range(1, len(BRANCHES)):
                merged = merged + wsl[g, :, c:c + 1] * osls[g][c]
            lhs[:, cs] = merged.astype(BF16)

    def project():
        x1_ref[...] = x_ref[0] + jnp.dot(lhs[...], wo_ref[...], preferred_element_type=F32)

    pieces = [weights]
    for g, (_, dil) in enumerate(BRANCHES):
        if dil > 1:
            half = max(dil // 2, 1) if dil >= 8 else dil
            pieces += [functools.partial(interleave, g, lo, min(lo + half, dil))
                       for lo in range(0, dil, half)]
    pieces += [functools.partial(combine, 0, N_SLABS // 2),
               functools.partial(combine, N_SLABS // 2, N_SLABS), project]
    return pieces


def _merge_mlp_kernel(x_ref, xs1_ref, o0_ref, o1_ref, o2_ref, ml0_ref, ml1_ref, ml2_ref, wo_ref,
                      gmlp_ref, w1_ref, w2_ref, gf_ref, out_ref, os_ref,
                      x1a, x1b, hsa, hsb, us, osl1, osl2, wsl, lhs, *, tm, tc, final_norm):
    def run(x1_w, hs_w, x1_r, hs_r):
        def next_norm():
            hs_w[...] = _rms(x1_w[...], gmlp_ref[...]).astype(BF16)

        pieces = _merge_pieces(x_ref, (o0_ref, o1_ref, o2_ref), (ml0_ref, ml1_ref, ml2_ref),
                               wo_ref, (None, osl1, osl2), wsl, lhs, x1_w, tm)
        _mlp_tile(x1_r, hs_r, w1_ref, w2_ref, gf_ref, out_ref, us, tc, final_norm,
                  between=pieces + [next_norm])

    _run_pipelined(((x1a, hsa), (x1b, hsb)), xs1_ref, gmlp_ref, out_ref, os_ref, run)


def _merge_mlp(x, xs1, os_, mls, wo, wi, gmlp, w1, w2, gain_final, li, final_norm, tm=512, tc=512):
    B, T, D = x.shape
    Bs = xs1.shape[0]
    nt = T // tm
    n_tiles = B * nt

    def tile(j):
        jj = jnp.minimum(j, n_tiles - 1)
        return jj // nt, jj % nt

    one = pl.Buffered(1)
    in_specs = [pl.BlockSpec((1, tm, D), lambda j: (*tile(j), 0)),
                pl.BlockSpec((Bs, D), lambda j: (0, 0))]
    for (_, dil) in BRANCHES:
        in_specs.append(pl.BlockSpec((1, dil, tm // dil, D), lambda j: (tile(j)[0], 0, tile(j)[1], 0)))
    for (_, dil) in BRANCHES:
        in_specs.append(pl.BlockSpec((1, dil, tm // dil, LANES),
                                     lambda j: (tile(j)[0], 0, tile(j)[1], 0)))
    in_specs += [
        pl.BlockSpec((None, D, D), lambda j: (wi, 0, 0), pipeline_mode=one),
        pl.BlockSpec((None, 1, D), lambda j: (li, 0, 0)),
        pl.BlockSpec((None, D, D_FF), lambda j: (li, 0, 0), pipeline_mode=one),
        pl.BlockSpec((None, D_FF, D), lambda j: (li, 0, 0), pipeline_mode=one),
        pl.BlockSpec((1, D), lambda j: (0, 0)),
    ]
    out, out_s = pl.pallas_call(
        functools.partial(_merge_mlp_kernel, tm=tm, tc=tc, final_norm=final_norm),
        grid=(n_tiles + 1,),
        in_specs=in_specs,
        out_specs=[pl.BlockSpec((tm, D), lambda j: (jnp.maximum(j - 1, 0), 0)),
                   pl.BlockSpec((Bs, D), lambda j: (0, 0))],
        out_shape=[jax.ShapeDtypeStruct((B * T, D), F32), jax.ShapeDtypeStruct((Bs, D), F32)],
        scratch_shapes=[
            pltpu.VMEM((tm, D), F32), pltpu.VMEM((tm, D), F32),
            pltpu.VMEM((tm, D), BF16), pltpu.VMEM((tm, D), BF16),
            pltpu.VMEM((tm, D_FF), BF16),
            pltpu.VMEM((N_SLABS, tm, LANES), F32), pltpu.VMEM((N_SLABS, tm, LANES), F32),
            pltpu.VMEM((len(BRANCHES), tm, LANES), F32),
            pltpu.VMEM((tm, D), BF16),
        ],
        compiler_params=_cparams("arbitrary"),
        name="merge_mlp",
    )(x, xs1, *os_, *mls, wo, gmlp, w1, w2, gain_final)
    return out.reshape(B, T, D), out_s


def _attn_sample_kernel(q_ref, kvn_ref, c0_ref, c1_ref, c2_ref, bias_ref, o_ref, *, nb):
    caches = (c0_ref, c1_ref, c2_ref)
    for i in range(nb):
        sc_c, sc_n = [], []
        for g in range(len(BRANCHES)):
            q = q_ref[i, g]
            kc = caches[g][i, :, 0]
            sc_c.append(jnp.sum(kc * q[None], axis=-1, keepdims=True) + bias_ref[g, 0:N_KEYS - 1])
            sc_n.append(jnp.sum(kvn_ref[i, g, 0] * q, axis=-1, keepdims=True)
                        + bias_ref[g, N_KEYS - 1])
        m = sc_n[0]
        for g in range(len(BRANCHES)):
            m = jnp.maximum(m, jnp.maximum(jnp.max(sc_c[g], axis=0), sc_n[g]))
        num = jnp.zeros((HEADS, HEAD_DIM), F32)
        den = jnp.zeros((HEADS, HEAD_DIM), F32)
        for g in range(len(BRANCHES)):
            p_c = jnp.exp(sc_c[g] - m[None])
            p_n = jnp.exp(sc_n[g] - m)
            den = den + jnp.sum(p_c, axis=0) + p_n
            num = num + jnp.sum(p_c * caches[g][i, :, 1], axis=0) + p_n * kvn_ref[i, g, 1]
        o_ref[i] = num / den


def _attn_sample(q, kvn, caches, bias, nb=4):
    Bs = q.shape[0]
    in_specs = [
        pl.BlockSpec((nb,) + q.shape[1:], lambda b: (b, 0, 0, 0)),
        pl.BlockSpec((nb,) + kvn.shape[1:], lambda b: (b, 0, 0, 0, 0)),
    ]
    for c in caches:
        in_specs.append(pl.BlockSpec((nb, N_KEYS - 1, None, 2, HEADS, HEAD_DIM),
                                     lambda b: (b, 0, 0, 0, 0, 0)))
    in_specs.append(pl.BlockSpec(bias.shape, lambda b: (0, 0, 0, 0)))
    return pl.pallas_call(
        functools.partial(_attn_sample_kernel, nb=nb),
        grid=(Bs // nb,),
        in_specs=in_specs,
        out_specs=pl.BlockSpec((nb, HEADS, HEAD_DIM), lambda b: (b, 0, 0)),
        out_shape=jax.ShapeDtypeStruct((Bs, HEADS, HEAD_DIM), F32),
        compiler_params=_cparams("arbitrary"),
        name="attn_sample",
    )(q, kvn, *caches, bias)


def _out_sample_kernel(x_ref, o_ref, wo_ref, out_ref):
    out_ref[...] = x_ref[...] + jnp.dot(o_ref[...].astype(BF16), wo_ref[...],
                                        preferred_element_type=F32)


def _out_sample(x, o, wo):
    return pl.pallas_call(
        _out_sample_kernel,
        out_shape=jax.ShapeDtypeStruct(x.shape, F32),
        compiler_params=pltpu.CompilerParams(vmem_limit_bytes=VMEM_LIMIT),
        name="out_sample",
    )(x, o, wo)


def _t5_bucket(dist):
    max_exact = NUM_BUCKETS // 2
    df = jnp.maximum(dist, 1).astype(F32)
    large = max_exact + (jnp.log(df / max_exact) / math.log(MAX_DISTANCE / max_exact)
                         * (NUM_BUCKETS - max_exact)).astype(jnp.int32)
    large = jnp.minimum(large, NUM_BUCKETS - 1)
    return jnp.where(dist < max_exact, dist, large)


def _bias_tables(rel_bias):
    out = []
    for g, (w, d) in enumerate(BRANCHES):
        dist = jnp.arange(N_KEYS, dtype=jnp.int32) * d
        out.append(rel_bias[_t5_bucket(dist)][:, g * HEADS:(g + 1) * HEADS])
    return out


def _block_bias(tab):
    pad_lo = jnp.full((HEADS, Q_SUB - 1), MASKED, F32)
    pad_hi = jnp.full((HEADS, Q_SUB), MASKED, F32)
    f = jnp.concatenate([pad_lo, tab[::-1].T, pad_hi], axis=1)
    width = f.shape[1]
    flat = jnp.tile(f, (1, Q_SUB))[:, :Q_SUB * (width - 1)]
    toep = flat.reshape(HEADS, Q_SUB, width - 1)
    return toep[:, :, Q_SUB - 1:Q_SUB - 1 + 2 * Q_SUB]


def kernel(x_prompt, x_sample, state_pool, cache_kv_w128, cache_kv_w512, cache_kv_w2048,
           norm_mix, pool_w, pool_scale, norm_mlp, mlp_in, mlp_out, norm_kv, w_kv, w_q, w_o,
           rel_bias, norm_final):
    B, T, D = x_prompt.shape
    Bs = x_sample.shape[0]
    n_a = pool_w.shape[0]
    depth = norm_mix.shape[0]
    n_br = len(BRANCHES)
    qk_scale = HEAD_DIM ** -0.5

    pool_w_b = pool_w.astype(BF16)
    mlp_in_b = mlp_in.astype(BF16)
    mlp_out_b = mlp_out.astype(BF16)
    w_kv_b = w_kv.astype(BF16)[None]
    w_q_b = w_q.astype(BF16)
    w_o_b = w_o.astype(BF16)
    row = lambda v: v.reshape(1, D)
    g_mix = norm_mix.reshape(depth, 1, D)
    g_mlp = norm_mlp.reshape(depth, 1, D)
    g_kv = norm_kv.reshape(1, 1, D)
    g_scale = pool_scale.reshape(n_a, 1, D)

    tabs = _bias_tables(rel_bias)
    no_prev = np.arange(2 * Q_SUB)[None, None, :] < Q_SUB
    blk_bias = []
    for t in tabs:
        bb = _block_bias(t * LOG2E)
        blk_bias.append(jnp.concatenate([bb, jnp.where(no_prev, MASKED, bb)], axis=0))
    smp_bias = jnp.stack([jnp.broadcast_to(t[::-1][:, :, None], (N_KEYS, HEADS, HEAD_DIM))
                          for t in tabs])

    caches = [c.reshape(Bs, N_KEYS - 1, d, 2, HEADS, HEAD_DIM)
              for c, (w, d) in zip((cache_kv_w128, cache_kv_w512, cache_kv_w2048), BRANCHES)]

    xp = x_prompt
    xs = x_sample.reshape(Bs, D)
    pool_p, pool_s, kv_p = [], [], []
    kvs = kvn = None
    for l in range(depth):
        last = l == depth - 1
        if l < n_a:
            xs1, hs = _pool_sample(xs, state_pool[l], row(norm_mix[l]), pool_w_b[l], row(pool_scale[l]))
            pool_s.append(jnp.concatenate([state_pool[l][:, 1:], hs[:, None, :]], axis=1))
            xp, xs, nbuf = _pool_mlp(xp, xs1, g_mix, l, pool_w_b, g_scale, g_mlp, mlp_in_b,
                                     mlp_out_b, row(norm_final))
            pool_p.append(nbuf)
        else:
            lb = l - n_a
            if l == n_a:
                tails = tuple(min(w, T) for (w, _) in BRANCHES)
                res = _proj(xp, g_kv, 0, w_kv_b, 0, 2 * D, 256, tails=tails)
                kvs = res[:n_br]
                kv_p = [t.reshape(B, rows, 2, HEADS, HEAD_DIM) for t, rows in zip(res[n_br:], tails)]
                kvn = _proj_sample(xs, g_kv, 0, w_kv_b, 0)
            qs_p = _proj(xp, g_mix, l, w_q_b, lb, D, 1024, scale=qk_scale * LOG2E)
            os_, lses = [], []
            for g, (w, d) in enumerate(BRANCHES):
                L = T // d
                tq, rb = (L, min(d, ATTN_ROWS // L)) if L < ATTN_ROWS else (ATTN_ROWS, 1)
                o_g, ml_g = _attn_branch(qs_p[g], kvs[g], blk_bias[g], tq, rb)
                os_.append(o_g)
                lses.append(ml_g)
            qs = _proj_sample(xs, g_mix, l, w_q_b, lb, scale=qk_scale)
            o_s = _attn_sample(qs.reshape(Bs, n_br, HEADS, HEAD_DIM),
                               kvn.reshape(Bs, n_br, 2, HEADS, HEAD_DIM), caches, smp_bias)
            xs1 = _out_sample(xs, o_s.reshape(Bs, D), w_o_b[lb])
            xp, xs = _merge_mlp(xp, xs1, os_, lses, w_o_b, lb, g_mlp, mlp_in_b, mlp_out_b,
                                row(norm_final), l, last)

    kv_s = [kvn[:, g * 2 * D:(g + 1) * 2 * D].reshape(Bs, 1, 2, HEADS, HEAD_DIM) for g in range(n_br)]
    return (xp, xs.reshape(Bs, 1, D), jnp.stack(pool_p), jnp.stack(pool_s),
            kv_p[0], kv_s[0], kv_p[1], kv_s[1], kv_p[2], kv_s[2])
```

```python
import functools
import math

import jax
import jax.numpy as jnp
import numpy as np
from jax import lax
from jax.experimental import pallas as pl
from jax.experimental.pallas import tpu as pltpu

F32 = jnp.float32
BF16 = jnp.bfloat16

D_MODEL = 1024
HEADS = 8
HEAD_DIM = 128
D_FF = 4 * D_MODEL
POOL_WINDOWS = (2, 4, 8, 16)
POOL_GROUP = D_MODEL // len(POOL_WINDOWS)
POOL_STATE = max(POOL_WINDOWS) - 1
BRANCHES = ((128, 1), (512, 4), (2048, 16))
N_KEYS = 129
NUM_BUCKETS = 32
MAX_DISTANCE = 2048
SAMPLE_POS0 = 8192
EPS = 1e-6
MASKED = -1e30
LOG2E = math.log2(math.e)

LANES = 128
N_SLABS = D_MODEL // LANES
Q_SUB = 128
ATTN_ROWS = 2048
VMEM_LIMIT = 60 * 1024 * 1024


def _cparams(*sem):
    return pltpu.CompilerParams(dimension_semantics=sem, vmem_limit_bytes=VMEM_LIMIT)


def _rms(x, g):
    ms = jnp.mean(x * x, axis=-1, keepdims=True)
    return x * lax.rsqrt(ms + EPS) * g


POOL_BASE = 24
POOL_LO = 8


def _pool_pieces(x_ref, g_ref, w_ref, sc_ref, nb_ref, hbuf, sbuf, x1_ref, t, tm):
    end = POOL_BASE + tm

    def norm():
        hbuf[POOL_BASE:end, :] = _rms(x_ref[0], g_ref[...])

    def group(gi, w):
        sl = slice(gi * POOL_GROUP, (gi + 1) * POOL_GROUP)
        n_stage = w.bit_length() - 1
        s = None
        for si in range(n_stage):
            sh = 1 << si
            lo = POOL_BASE if si == n_stage - 1 else POOL_LO
            if si == 0:
                val = hbuf[lo:end, sl] + hbuf[lo - sh:end - sh, sl]
            else:
                val = sbuf[si - 1, lo:end, :] + sbuf[si - 1, lo - sh:end - sh, :]
            if si == n_stage - 1:
                s = val
            else:
                sbuf[si, lo:end, :] = val
        pos = t * tm + lax.broadcasted_iota(jnp.int32, (tm, 1), 0)
        cnt = jnp.minimum(pos + 1, w).astype(F32)
        pooled = s / cnt - hbuf[POOL_BASE:end, sl]
        y = jnp.dot(pooled.astype(BF16), w_ref[gi], preferred_element_type=F32)
        x1_ref[:, sl] = x_ref[0, :, sl] + y * sc_ref[:, sl]

    def carry():
        nb_ref[0] = hbuf[end - POOL_STATE:end, :]
        hbuf[POOL_BASE - 16:POOL_BASE, :] = hbuf[end - 16:end, :]

    pieces = [norm]
    pieces += [functools.partial(group, gi, w) for gi, w in enumerate(POOL_WINDOWS)]
    return pieces + [carry]


def _mlp_tile(x1_ref, hs_ref, w1_ref, w2_ref, gf_ref, o_ref, us, tc, final_norm, between=()):
    between = list(between)
    n_chunks = D_FF // tc
    assert len(between) <= n_chunks
    for c in range(n_chunks):
        if c < len(between):
            between[c]()
        u = jnp.dot(hs_ref[...], w1_ref[:, c * tc:(c + 1) * tc], preferred_element_type=F32)
        us[:, c * tc:(c + 1) * tc] = jnp.square(jnp.maximum(u, 0.0)).astype(BF16)
    y = x1_ref[...] + jnp.dot(us[...], w2_ref[...], preferred_element_type=F32)
    if final_norm:
        y = _rms(y, gf_ref[...])
    o_ref[...] = y


def _run_pipelined(slots, xs1_ref, gmlp_ref, o_ref, os_ref, run):
    (x1a, hsa), (x1b, hsb) = slots
    j = pl.program_id(0)
    bs = xs1_ref.shape[0]

    @pl.when(j == 0)
    def _():
        x1b[...] = jnp.zeros(x1b.shape, F32)
        hsb[...] = jnp.zeros(hsb.shape, BF16)
        xs1 = xs1_ref[...]
        x1b[0:bs, :] = xs1
        hsb[0:bs, :] = _rms(xs1, gmlp_ref[...]).astype(BF16)

    @pl.when(j % 2 == 0)
    def _():
        run(x1a, hsa, x1b, hsb)

    @pl.when(j % 2 == 1)
    def _():
        run(x1b, hsb, x1a, hsa)

    @pl.when(j == 0)
    def _():
        os_ref[...] = o_ref[0:bs, :]


def _cast_rider(refs):
    def cast():
        if refs:
            w1f_ref, w2f_ref, w1n_ref, w2n_ref = refs
            w1n_ref[...] = w1f_ref[...].astype(BF16)
            w2n_ref[...] = w2f_ref[...].astype(BF16)
    return cast


def _rider_specs(nxt, n_tiles):
    if nxt is None:
        return [], [], [], []
    w1f, w2f, ln = nxt
    r1, r2 = w1f.shape[1] // n_tiles, w2f.shape[1] // n_tiles
    chunk = lambda j: jnp.minimum(j, n_tiles - 1)
    in_specs = [pl.BlockSpec((None, r1, w1f.shape[2]), lambda j: (ln, chunk(j), 0)),
                pl.BlockSpec((None, r2, w2f.shape[2]), lambda j: (ln, chunk(j), 0))]
    out_specs = [pl.BlockSpec((r1, w1f.shape[2]), lambda j: (chunk(j), 0)),
                 pl.BlockSpec((r2, w2f.shape[2]), lambda j: (chunk(j), 0))]
    out_shape = [jax.ShapeDtypeStruct(w1f.shape[1:], BF16), jax.ShapeDtypeStruct(w2f.shape[1:], BF16)]
    return [w1f, w2f], in_specs, out_specs, out_shape


def _pool_mlp_kernel(x_ref, xs1_ref, gmix_ref, wp_ref, sc_ref, gmlp_ref, w1_ref, w2_ref, gf_ref,
                     *refs, tm, nt, n_tiles, tc, ride):
    if ride:
        w1f_ref, w2f_ref, o_ref, os_ref, nb_ref, w1n_ref, w2n_ref = refs[:7]
        x1a, x1b, hsa, hsb, us, hbuf, sbuf = refs[7:]
        cast = _cast_rider((w1f_ref, w2f_ref, w1n_ref, w2n_ref))
    else:
        o_ref, os_ref, nb_ref, x1a, x1b, hsa, hsb, us, hbuf, sbuf = refs
        cast = _cast_rider(())
    j = pl.program_id(0)
    t = jnp.minimum(j, n_tiles - 1) % nt

    @pl.when(j == 0)
    def _():
        hbuf[0:POOL_BASE, :] = jnp.zeros((POOL_BASE, D_MODEL), F32)
        sbuf[...] = jnp.zeros(sbuf.shape, F32)

    @pl.when(t == 0)
    def _():
        hbuf[POOL_BASE - 16:POOL_BASE, :] = jnp.zeros((16, D_MODEL), F32)

    def run(x1_w, hs_w, x1_r, hs_r):
        def next_norm():
            hs_w[...] = _rms(x1_w[...], gmlp_ref[...]).astype(BF16)

        pieces = _pool_pieces(x_ref, gmix_ref, wp_ref, sc_ref, nb_ref, hbuf, sbuf, x1_w, t, tm)
        _mlp_tile(x1_r, hs_r, w1_ref, w2_ref, gf_ref, o_ref, us, tc, False,
                  between=pieces + [next_norm, cast])

    _run_pipelined(((x1a, hsa), (x1b, hsb)), xs1_ref, gmlp_ref, o_ref, os_ref, run)


def _pool_mlp(x, xs1, gmix, li, w_pool, scale, gmlp, w1, w2, gain_final, nxt, tm=512, tc=512):
    B, T, D = x.shape
    Bs = xs1.shape[0]
    nt = T // tm
    n_tiles = B * nt

    def tile(j):
        jj = jnp.minimum(j, n_tiles - 1)
        return jj // nt, jj % nt

    one = pl.Buffered(1)
    ride_args, ride_in, ride_out, ride_shape = _rider_specs(nxt, n_tiles)
    res = pl.pallas_call(
        functools.partial(_pool_mlp_kernel, tm=tm, nt=nt, n_tiles=n_tiles, tc=tc,
                          ride=nxt is not None),
        grid=(n_tiles + 1,),
        in_specs=[
            pl.BlockSpec((1, tm, D), lambda j: (*tile(j), 0)),
            pl.BlockSpec((Bs, D), lambda j: (0, 0)),
            pl.BlockSpec((None, 1, D), lambda j: (li, 0, 0)),
            pl.BlockSpec((None,) + w_pool.shape[1:], lambda j: (li, 0, 0, 0)),
            pl.BlockSpec((None, 1, D), lambda j: (li, 0, 0)),
            pl.BlockSpec((None, 1, D), lambda j: (li, 0, 0)),
            pl.BlockSpec((D, D_FF), lambda j: (0, 0), pipeline_mode=one),
            pl.BlockSpec((D_FF, D), lambda j: (0, 0), pipeline_mode=one),
            pl.BlockSpec((1, D), lambda j: (0, 0)),
        ] + ride_in,
        out_specs=[
            pl.BlockSpec((tm, D), lambda j: (jnp.maximum(j - 1, 0), 0)),
            pl.BlockSpec((Bs, D), lambda j: (0, 0)),
            pl.BlockSpec((1, POOL_STATE, D), lambda j: (tile(j)[0], 0, 0)),
        ] + ride_out,
        out_shape=[
            jax.ShapeDtypeStruct((B * T, D), F32),
            jax.ShapeDtypeStruct((Bs, D), F32),
            jax.ShapeDtypeStruct((B, POOL_STATE, D), F32),
        ] + ride_shape,
        scratch_shapes=[
            pltpu.VMEM((tm, D), F32), pltpu.VMEM((tm, D), F32),
            pltpu.VMEM((tm, D), BF16), pltpu.VMEM((tm, D), BF16),
            pltpu.VMEM((tm, D_FF), BF16),
            pltpu.VMEM((POOL_BASE + tm, D), F32),
            pltpu.VMEM((len(POOL_WINDOWS) - 1, POOL_BASE + tm, POOL_GROUP), F32),
        ],
        compiler_params=_cparams("arbitrary"),
        name="pool_mlp",
    )(x, xs1, gmix, w_pool, scale, gmlp, w1, w2, gain_final, *ride_args)
    return (res[0].reshape(B, T, D),) + tuple(res[1:])


def _pool_sample_kernel(x_ref, st_ref, g_ref, w_ref, sc_ref, o_ref, h_ref):
    x = x_ref[...]
    h = _rms(x, g_ref[...])
    h_ref[...] = h
    for gi, w in enumerate(POOL_WINDOWS):
        sl = slice(gi * POOL_GROUP, (gi + 1) * POOL_GROUP)
        hg = h[:, sl]
        s = hg
        for j in range(1, w):
            s = s + st_ref[:, POOL_STATE - j, sl]
        pooled = s / float(min(SAMPLE_POS0 + 1, w)) - hg
        y = jnp.dot(pooled.astype(BF16), w_ref[gi], preferred_element_type=F32)
        o_ref[:, sl] = x[:, sl] + y * sc_ref[:, sl]


def _pool_sample(x, state, gain, w_pool, scale):
    Bs, D = x.shape
    return pl.pallas_call(
        _pool_sample_kernel,
        out_shape=[jax.ShapeDtypeStruct((Bs, D), F32), jax.ShapeDtypeStruct((Bs, D), F32)],
        compiler_params=pltpu.CompilerParams(vmem_limit_bytes=VMEM_LIMIT),
        name="pool_sample",
    )(x, state, gain, w_pool, scale)


def _proj_kernel(x_ref, g_ref, w_ref, *refs, tm, nc, scale, tails):
    n_br = len(BRANCHES)
    o_refs = refs[:n_br]
    if tails:
        tail_refs = refs[n_br:2 * n_br]
        hsl, tsl = refs[2 * n_br], refs[2 * n_br + 1]
        lhs = refs[2 * n_br + 2:]
        i = pl.program_id(1)
        nt = pl.num_programs(1)
    else:
        hsl = refs[n_br]
        lhs = refs[n_br + 1:]
    h = _rms(x_ref[0], g_ref[...])
    for c in range(N_SLABS):
        hsl[c] = h[:, c * LANES:(c + 1) * LANES]
    for g, (_, dil) in enumerate(BRANCHES):
        n = tm // dil
        if dil == 1:
            lhs[g][...] = h.astype(BF16)
        else:
            for r in range(dil):
                for c in range(N_SLABS):
                    lhs[g][r * n:(r + 1) * n, c * LANES:(c + 1) * LANES] = (
                        hsl[c, pl.ds(r, n, stride=dil), :].astype(BF16))
        res = jnp.dot(lhs[g][...], w_ref[:, g * nc:(g + 1) * nc], preferred_element_type=F32)
        if scale != 1.0:
            res = res * scale
        for r in range(dil):
            o_refs[g][0, r] = res[r * n:(r + 1) * n].astype(o_refs[g].dtype)

        if tails:
            tail_ref, rows = tail_refs[g], tails[g]
            if rows < tm:
                @pl.when(i == nt - 1)
                def _(tail_ref=tail_ref, rows=rows, res=res):
                    tail_ref[0] = res[tm - rows:tm]
            else:
                @pl.when(i >= nt - rows // tm)
                def _(tail_ref=tail_ref, res=res, dil=dil, n=n):
                    if dil == 1:
                        tail_ref[0] = res
                    else:
                        for c in range(nc // LANES):
                            for r in range(dil):
                                tsl[c, pl.ds(r, n, stride=dil), :] = (
                                    res[r * n:(r + 1) * n, c * LANES:(c + 1) * LANES])
                        for c in range(nc // LANES):
                            tail_ref[0, :, c * LANES:(c + 1) * LANES] = tsl[c]


def _proj(x, gains, gi, w, wi, nc, tm, scale=1.0, tails=()):
    B, T, D = x.shape
    nt = T // tm
    out_specs, out_shape = [], []
    for (_, dil) in BRANCHES:
        out_specs.append(pl.BlockSpec((1, dil, tm // dil, nc), lambda b, i: (b, 0, i, 0)))
        out_shape.append(jax.ShapeDtypeStruct((B, dil, T // dil, nc), BF16))
    scratch = [pltpu.VMEM((N_SLABS, tm, LANES), F32)]
    if tails:
        for rows, (_, dil) in zip(tails, BRANCHES):
            assert rows >= tm or dil == 1
            if rows < tm:
                out_specs.append(pl.BlockSpec((1, rows, nc), lambda b, i: (b, 0, 0)))
            else:
                first = nt - rows // tm
                out_specs.append(pl.BlockSpec(
                    (1, tm, nc), lambda b, i, first=first: (b, jnp.maximum(i - first, 0), 0)))
            out_shape.append(jax.ShapeDtypeStruct((B, rows, nc), F32))
        scratch.append(pltpu.VMEM((nc // LANES, tm, LANES), F32))
    scratch += [pltpu.VMEM((tm, D), BF16) for _ in BRANCHES]
    return pl.pallas_call(
        functools.partial(_proj_kernel, tm=tm, nc=nc, scale=scale, tails=tuple(tails)),
        grid=(B, nt),
        in_specs=[
            pl.BlockSpec((1, tm, D), lambda b, i: (b, i, 0)),
            pl.BlockSpec((None, 1, D), lambda b, i: (gi, 0, 0)),
            pl.BlockSpec((None, D, len(BRANCHES) * nc), lambda b, i: (wi, 0, 0),
                         pipeline_mode=pl.Buffered(1)),
        ],
        out_specs=out_specs,
        out_shape=out_shape,
        scratch_shapes=scratch,
        compiler_params=_cparams("arbitrary", "arbitrary"),
        name="proj_kv" if tails else "proj_q",
    )(x, gains, w)


def _proj_sample_kernel(x_ref, g_ref, w_ref, o_ref, *, scale):
    h = _rms(x_ref[...], g_ref[...]).astype(BF16)
    res = jnp.dot(h, w_ref[...], preferred_element_type=F32)
    if scale != 1.0:
        res = res * scale
    o_ref[...] = res


def _proj_sample(x, gains, gi, w, wi, scale=1.0, tn=3072):
    Bs, D = x.shape
    nc = w.shape[2]
    return pl.pallas_call(
        functools.partial(_proj_sample_kernel, scale=scale),
        grid=(nc // tn,),
        in_specs=[
            pl.BlockSpec((Bs, D), lambda j: (0, 0)),
            pl.BlockSpec((None, 1, D), lambda j: (gi, 0, 0)),
            pl.BlockSpec((None, D, tn), lambda j: (wi, 0, j)),
        ],
        out_specs=pl.BlockSpec((Bs, tn), lambda j: (0, j)),
        out_shape=jax.ShapeDtypeStruct((Bs, nc), F32),
        compiler_params=_cparams("arbitrary"),
        name="proj_sample",
    )(x, gains, w)


def _attn_kernel(q_ref, kv_ref, bias_ref, o_ref, ml_ref, kprev, vprev, *, tq, rb):
    first = pl.program_id(2) == 0
    lane = lax.broadcasted_iota(jnp.int32, (Q_SUB, LANES), 1)
    ones = jnp.ones((2 * Q_SUB, HEAD_DIM), BF16)
    if rb == 1:
        @pl.when(first)
        def _():
            kprev[...] = jnp.zeros(kprev.shape, BF16)
            vprev[...] = jnp.zeros(vprev.shape, BF16)
    for rr in range(rb):
        for s in range(tq // Q_SUB):
            r0 = s * Q_SUB

            def keys(h, col0, s=s, r0=r0, rr=rr):
                cs = slice(col0 + h * HEAD_DIM, col0 + (h + 1) * HEAD_DIM)
                if s > 0:
                    return kv_ref[0, rr, r0 - Q_SUB:r0 + Q_SUB, cs]
                cur = kv_ref[0, rr, 0:Q_SUB, cs]
                if rb == 1:
                    prev = (kprev if col0 == 0 else vprev)[:, h * HEAD_DIM:(h + 1) * HEAD_DIM]
                else:
                    prev = jnp.zeros((Q_SUB, HEAD_DIM), BF16)
                return jnp.concatenate([prev, cur], axis=0)

            if s > 0:
                boff = 0
            elif rb > 1:
                boff = HEADS
            else:
                boff = jnp.where(first, HEADS, 0)
            ml_tile = jnp.zeros((Q_SUB, LANES), F32)
            for h in range(HEADS):
                cs = slice(h * HEAD_DIM, (h + 1) * HEAD_DIM)
                q = q_ref[0, rr, r0:r0 + Q_SUB, cs]
                sc = lax.dot_general(q, keys(h, 0), (((1,), (1,)), ((), ())),
                                     preferred_element_type=F32)
                sc = sc + bias_ref[boff + h]
                m = jnp.max(sc, axis=-1, keepdims=True)
                p = jnp.exp2(sc - m).astype(BF16)
                v_ext = jnp.concatenate([keys(h, D_MODEL), ones], axis=1)
                o = jnp.dot(p, v_ext, preferred_element_type=F32)
                o_ref[0, rr, r0:r0 + Q_SUB, cs] = o[:, :HEAD_DIM].astype(o_ref.dtype)
                ml_tile = jnp.where(lane == h, m, ml_tile)
                ml_tile = jnp.where(lane == HEADS + h, o[:, HEAD_DIM:], ml_tile)
            ml_ref[0, rr, r0:r0 + Q_SUB, :] = ml_tile
    if rb == 1:
        kprev[...] = kv_ref[0, 0, tq - Q_SUB:tq, 0:D_MODEL]
        vprev[...] = kv_ref[0, 0, tq - Q_SUB:tq, D_MODEL:2 * D_MODEL]


def _attn_branch(q, kv, bias, tq, rb):
    B, dil, L, D = q.shape
    assert rb == 1 or tq == L
    return pl.pallas_call(
        functools.partial(_attn_kernel, tq=tq, rb=rb),
        grid=(B, dil // rb, L // tq),
        in_specs=[
            pl.BlockSpec((1, rb, tq, D), lambda b, r, i: (b, r, i, 0)),
            pl.BlockSpec((1, rb, tq, 2 * D), lambda b, r, i: (b, r, i, 0)),
            pl.BlockSpec(bias.shape, lambda b, r, i: (0, 0, 0)),
        ],
        out_specs=[
            pl.BlockSpec((1, rb, tq, D), lambda b, r, i: (b, r, i, 0)),
            pl.BlockSpec((1, rb, tq, LANES), lambda b, r, i: (b, r, i, 0)),
        ],
        out_shape=[
            jax.ShapeDtypeStruct((B, dil, L, D), BF16),
            jax.ShapeDtypeStruct((B, dil, L, LANES), F32),
        ],
        scratch_shapes=[pltpu.VMEM((Q_SUB, D), BF16), pltpu.VMEM((Q_SUB, D), BF16)],
        compiler_params=_cparams("arbitrary", "arbitrary", "arbitrary"),
        name=f"attn_d{dil}",
    )(q, kv, bias)


def _merge_pieces(x_ref, o_refs, ml_refs, wo_ref, osls, wsl, lhs, x1_ref, tm):
    def weights():
        tiles = []
        for g, (_, dil) in enumerate(BRANCHES):
            if dil == 1:
                tiles.append(ml_refs[g][0, 0])
            else:
                n = tm // dil
                for r in range(dil):
                    wsl[g, pl.ds(r, n, stride=dil), :] = ml_refs[g][0, r]
                tiles.append(wsl[g])
        m = jnp.maximum(jnp.maximum(tiles[0], tiles[1]), tiles[2])
        us = [jnp.exp2(tl - m) for tl in tiles]
        ls = [pltpu.roll(tl, LANES - HEADS, axis=1) for tl in tiles]
        den = us[0] * ls[0] + us[1] * ls[1] + us[2] * ls[2]
        for g in range(len(BRANCHES)):
            wsl[g] = us[g] / den

    def interleave(g, r_lo, r_hi):
        dil = BRANCHES[g][1]
        n = tm // dil
        for r in range(r_lo, r_hi):
            for c in range(N_SLABS):
                osls[g][c, pl.ds(r, n, stride=dil), :] = (
                    o_refs[g][0, r, :, c * LANES:(c + 1) * LANES].astype(F32))

    def combine(c_lo, c_hi):
        for c in range(c_lo, c_hi):
            cs = slice(c * LANES, (c + 1) * LANES)
            merged = wsl[0, :, c:c + 1] * o_refs[0][0, 0, :, cs].astype(F32)
            for g in range(1, len(BRANCHES)):
                merged = merged + wsl[g, :, c:c + 1] * osls[g][c]
            lhs[:, cs] = merged.astype(BF16)

    def project():
        x1_ref[...] = x_ref[0] + jnp.dot(lhs[...], wo_ref[...], preferred_element_type=F32)

    pieces = [weights]
    for g, (_, dil) in enumerate(BRANCHES):
        if dil > 1:
            half = max(dil // 2, 1) if dil >= 8 else dil
            pieces += [functools.partial(interleave, g, lo, min(lo + half, dil))
                       for lo in range(0, dil, half)]
    pieces += [functools.partial(combine, 0, N_SLABS // 2),
               functools.partial(combine, N_SLABS // 2, N_SLABS), project]
    return pieces


def _merge_mlp_kernel(x_ref, xs1_ref, o0_ref, o1_ref, o2_ref, ml0_ref, ml1_ref, ml2_ref, wo_ref,
                      gmlp_ref, w1_ref, w2_ref, gf_ref, *refs, tm, tc, final_norm, ride):
    if ride:
        w1f_ref, w2f_ref, out_ref, os_ref, w1n_ref, w2n_ref = refs[:6]
        x1a, x1b, hsa, hsb, us, osl1, osl2, wsl, lhs = refs[6:]
        cast = _cast_rider((w1f_ref, w2f_ref, w1n_ref, w2n_ref))
    else:
        out_ref, os_ref, x1a, x1b, hsa, hsb, us, osl1, osl2, wsl, lhs = refs
        cast = _cast_rider(())

    def run(x1_w, hs_w, x1_r, hs_r):
        def next_norm():
            hs_w[...] = _rms(x1_w[...], gmlp_ref[...]).astype(BF16)
            cast()

        pieces = _merge_pieces(x_ref, (o0_ref, o1_ref, o2_ref), (ml0_ref, ml1_ref, ml2_ref),
                               wo_ref, (None, osl1, osl2), wsl, lhs, x1_w, tm)
        _mlp_tile(x1_r, hs_r, w1_ref, w2_ref, gf_ref, out_ref, us, tc, final_norm,
                  between=pieces + [next_norm])

    _run_pipelined(((x1a, hsa), (x1b, hsb)), xs1_ref, gmlp_ref, out_ref, os_ref, run)


def _merge_mlp(x, xs1, os_, mls, wo, wi, gmlp, w1, w2, gain_final, li, final_norm, nxt,
               tm=512, tc=512):
    B, T, D = x.shape
    Bs = xs1.shape[0]
    nt = T // tm
    n_tiles = B * nt

    def tile(j):
        jj = jnp.minimum(j, n_tiles - 1)
        return jj // nt, jj % nt

    one = pl.Buffered(1)
    in_specs = [pl.BlockSpec((1, tm, D), lambda j: (*tile(j), 0)),
                pl.BlockSpec((Bs, D), lambda j: (0, 0))]
    for (_, dil) in BRANCHES:
        in_specs.append(pl.BlockSpec((1, dil, tm // dil, D), lambda j: (tile(j)[0], 0, tile(j)[1], 0)))
    for (_, dil) in BRANCHES:
        in_specs.append(pl.BlockSpec((1, dil, tm // dil, LANES),
                                     lambda j: (tile(j)[0], 0, tile(j)[1], 0)))
    in_specs += [
        pl.BlockSpec((None, D, D), lambda j: (wi, 0, 0), pipeline_mode=one),
        pl.BlockSpec((None, 1, D), lambda j: (li, 0, 0)),
        pl.BlockSpec((D, D_FF), lambda j: (0, 0), pipeline_mode=one),
        pl.BlockSpec((D_FF, D), lambda j: (0, 0), pipeline_mode=one),
        pl.BlockSpec((1, D), lambda j: (0, 0)),
    ]
    ride_args, ride_in, ride_out, ride_shape = _rider_specs(nxt, n_tiles)
    res = pl.pallas_call(
        functools.partial(_merge_mlp_kernel, tm=tm, tc=tc, final_norm=final_norm,
                          ride=nxt is not None),
        grid=(n_tiles + 1,),
        in_specs=in_specs + ride_in,
        out_specs=[pl.BlockSpec((tm, D), lambda j: (jnp.maximum(j - 1, 0), 0)),
                   pl.BlockSpec((Bs, D), lambda j: (0, 0))] + ride_out,
        out_shape=[jax.ShapeDtypeStruct((B * T, D), F32),
                   jax.ShapeDtypeStruct((Bs, D), F32)] + ride_shape,
        scratch_shapes=[
            pltpu.VMEM((tm, D), F32), pltpu.VMEM((tm, D), F32),
            pltpu.VMEM((tm, D), BF16), pltpu.VMEM((tm, D), BF16),
            pltpu.VMEM((tm, D_FF), BF16),
            pltpu.VMEM((N_SLABS, tm, LANES), F32), pltpu.VMEM((N_SLABS, tm, LANES), F32),
            pltpu.VMEM((len(BRANCHES), tm, LANES), F32),
            pltpu.VMEM((tm, D), BF16),
        ],
        compiler_params=_cparams("arbitrary"),
        name="merge_mlp",
    )(x, xs1, *os_, *mls, wo, gmlp, w1, w2, gain_final, *ride_args)
    return (res[0].reshape(B, T, D),) + tuple(res[1:])


def _attn_sample_kernel(q_ref, kvn_ref, c0_ref, c1_ref, c2_ref, bias_ref, o_ref, *, nb):
    caches = (c0_ref, c1_ref, c2_ref)
    for i in range(nb):
        sc_c, sc_n = [], []
        for g in range(len(BRANCHES)):
            q = q_ref[i, g]
            kc = caches[g][i, :, 0]
            sc_c.append(jnp.sum(kc * q[None], axis=-1, keepdims=True) + bias_ref[g, 0:N_KEYS - 1])
            sc_n.append(jnp.sum(kvn_ref[i, g, 0] * q, axis=-1, keepdims=True)
                        + bias_ref[g, N_KEYS - 1])
        m = sc_n[0]
        for g in range(len(BRANCHES)):
            m = jnp.maximum(m, jnp.maximum(jnp.max(sc_c[g], axis=0), sc_n[g]))
        num = jnp.zeros((HEADS, HEAD_DIM), F32)
        den = jnp.zeros((HEADS, HEAD_DIM), F32)
        for g in range(len(BRANCHES)):
            p_c = jnp.exp2(sc_c[g] - m[None])
            p_n = jnp.exp2(sc_n[g] - m)
            den = den + jnp.sum(p_c, axis=0) + p_n
            num = num + jnp.sum(p_c * caches[g][i, :, 1], axis=0) + p_n * kvn_ref[i, g, 1]
        o_ref[i] = num / den


def _attn_sample(q, kvn, caches, bias, nb=4):
    Bs = q.shape[0]
    in_specs = [
        pl.BlockSpec((nb,) + q.shape[1:], lambda b: (b, 0, 0, 0)),
        pl.BlockSpec((nb,) + kvn.shape[1:], lambda b: (b, 0, 0, 0, 0)),
    ]
    for c in caches:
        in_specs.append(pl.BlockSpec((nb, N_KEYS - 1, None, 2, HEADS, HEAD_DIM),
                                     lambda b: (b, 0, 0, 0, 0, 0)))
    in_specs.append(pl.BlockSpec(bias.shape, lambda b: (0, 0, 0, 0)))
    return pl.pallas_call(
        functools.partial(_attn_sample_kernel, nb=nb),
        grid=(Bs // nb,),
        in_specs=in_specs,
        out_specs=pl.BlockSpec((nb, HEADS, HEAD_DIM), lambda b: (b, 0, 0)),
        out_shape=jax.ShapeDtypeStruct((Bs, HEADS, HEAD_DIM), F32),
        compiler_params=_cparams("arbitrary"),
        name="attn_sample",
    )(q, kvn, *caches, bias)


def _out_sample_kernel(x_ref, o_ref, wo_ref, out_ref):
    out_ref[...] = x_ref[...] + jnp.dot(o_ref[...].astype(BF16), wo_ref[...],
                                        preferred_element_type=F32)


def _out_sample(x, o, wo):
    return pl.pallas_call(
        _out_sample_kernel,
        out_shape=jax.ShapeDtypeStruct(x.shape, F32),
        compiler_params=pltpu.CompilerParams(vmem_limit_bytes=VMEM_LIMIT),
        name="out_sample",
    )(x, o, wo)


def _t5_bucket(dist):
    max_exact = NUM_BUCKETS // 2
    df = jnp.maximum(dist, 1).astype(F32)
    large = max_exact + (jnp.log(df / max_exact) / math.log(MAX_DISTANCE / max_exact)
                         * (NUM_BUCKETS - max_exact)).astype(jnp.int32)
    large = jnp.minimum(large, NUM_BUCKETS - 1)
    return jnp.where(dist < max_exact, dist, large)


def _bias_tables(rel_bias):
    out = []
    for g, (w, d) in enumerate(BRANCHES):
        dist = jnp.arange(N_KEYS, dtype=jnp.int32) * d
        out.append(rel_bias[_t5_bucket(dist)][:, g * HEADS:(g + 1) * HEADS])
    return out


def _block_bias(tab):
    pad_lo = jnp.full((HEADS, Q_SUB - 1), MASKED, F32)
    pad_hi = jnp.full((HEADS, Q_SUB), MASKED, F32)
    f = jnp.concatenate([pad_lo, tab[::-1].T, pad_hi], axis=1)
    width = f.shape[1]
    flat = jnp.tile(f, (1, Q_SUB))[:, :Q_SUB * (width - 1)]
    toep = flat.reshape(HEADS, Q_SUB, width - 1)
    return toep[:, :, Q_SUB - 1:Q_SUB - 1 + 2 * Q_SUB]


def kernel(x_prompt, x_sample, state_pool, cache_kv_w128, cache_kv_w512, cache_kv_w2048,
           norm_mix, pool_w, pool_scale, norm_mlp, mlp_in, mlp_out, norm_kv, w_kv, w_q, w_o,
           rel_bias, norm_final):
    B, T, D = x_prompt.shape
    Bs = x_sample.shape[0]
    n_a = pool_w.shape[0]
    depth = norm_mix.shape[0]
    n_br = len(BRANCHES)
    qk_scale = HEAD_DIM ** -0.5

    pool_w_b = pool_w.astype(BF16)
    w1_b, w2_b = mlp_in[0].astype(BF16), mlp_out[0].astype(BF16)
    w_kv_b = w_kv.astype(BF16)[None]
    w_q_b = w_q.astype(BF16)
    w_o_b = w_o.astype(BF16)
    row = lambda v: v.reshape(1, D)
    g_mix = norm_mix.reshape(depth, 1, D)
    g_mlp = norm_mlp.reshape(depth, 1, D)
    g_kv = norm_kv.reshape(1, 1, D)
    g_scale = pool_scale.reshape(n_a, 1, D)

    tabs = _bias_tables(rel_bias)
    no_prev = np.arange(2 * Q_SUB)[None, None, :] < Q_SUB
    blk_bias = []
    for t in tabs:
        bb = _block_bias(t * LOG2E)
        blk_bias.append(jnp.concatenate([bb, jnp.where(no_prev, MASKED, bb)], axis=0))
    smp_bias = jnp.stack([jnp.broadcast_to((t * LOG2E)[::-1][:, :, None], (N_KEYS, HEADS, HEAD_DIM))
                          for t in tabs])

    caches = [c.reshape(Bs, N_KEYS - 1, d, 2, HEADS, HEAD_DIM)
              for c, (w, d) in zip((cache_kv_w128, cache_kv_w512, cache_kv_w2048), BRANCHES)]

    xp = x_prompt
    xs = x_sample.reshape(Bs, D)
    pool_p, pool_s, kv_p = [], [], []
    kvs = kvn = None
    for l in range(depth):
        last = l == depth - 1
        nxt = None if last else (mlp_in, mlp_out, l + 1)
        if l < n_a:
            xs1, hs = _pool_sample(xs, state_pool[l], row(norm_mix[l]), pool_w_b[l], row(pool_scale[l]))
            pool_s.append(jnp.concatenate([state_pool[l][:, 1:], hs[:, None, :]], axis=1))
            res = _pool_mlp(xp, xs1, g_mix, l, pool_w_b, g_scale, g_mlp, w1_b, w2_b,
                            row(norm_final), nxt)
            xp, xs, nbuf = res[:3]
            pool_p.append(nbuf)
        else:
            lb = l - n_a
            if l == n_a:
                tails = tuple(min(w, T) for (w, _) in BRANCHES)
                res = _proj(xp, g_kv, 0, w_kv_b, 0, 2 * D, 256, tails=tails)
                kvs = res[:n_br]
                kv_p = [t.reshape(B, rows, 2, HEADS, HEAD_DIM) for t, rows in zip(res[n_br:], tails)]
                kvn = _proj_sample(xs, g_kv, 0, w_kv_b, 0)
            qs_p = _proj(xp, g_mix, l, w_q_b, lb, D, 1024, scale=qk_scale * LOG2E)
            qs = _proj_sample(xs, g_mix, l, w_q_b, lb, scale=qk_scale * LOG2E)
            os_, lses = [], []
            for g, (w, d) in enumerate(BRANCHES):
                L = T // d
                tq, rb = (L, min(d, ATTN_ROWS // L)) if L < ATTN_ROWS else (ATTN_ROWS, 1)
                o_g, ml_g = _attn_branch(qs_p[g], kvs[g], blk_bias[g], tq, rb)
                os_.append(o_g)
                lses.append(ml_g)
            o_s = _attn_sample(qs.reshape(Bs, n_br, HEADS, HEAD_DIM),
                               kvn.reshape(Bs, n_br, 2, HEADS, HEAD_DIM), caches, smp_bias)
            xs1 = _out_sample(xs, o_s.reshape(Bs, D), w_o_b[lb])
            res = _merge_mlp(xp, xs1, os_, lses, w_o_b, lb, g_mlp, w1_b, w2_b, row(norm_final),
                             l, last, nxt)
            xp, xs = res[:2]
        if nxt is not None:
            w1_b, w2_b = res[-2:]

    kv_s = [kvn[:, g * 2 * D:(g + 1) * 2 * D].reshape(Bs, 1, 2, HEADS, HEAD_DIM) for g in range(n_br)]
    return (xp, xs.reshape(Bs, 1, D), jnp.stack(pool_p), jnp.stack(pool_s),
            kv_p[0], kv_s[0], kv_p[1], kv_s[1], kv_p[2], kv_s[2])
```

```python
import functools
import math

import jax
import jax.numpy as jnp
import numpy as np
from jax import lax
from jax.experimental import pallas as pl
from jax.experimental.pallas import tpu as pltpu

F32 = jnp.float32
BF16 = jnp.bfloat16

D_MODEL = 1024
HEADS = 8
HEAD_DIM = 128
D_FF = 4 * D_MODEL
POOL_WINDOWS = (2, 4, 8, 16)
POOL_GROUP = D_MODEL // len(POOL_WINDOWS)
POOL_STATE = max(POOL_WINDOWS) - 1
BRANCHES = ((128, 1), (512, 4), (2048, 16))
N_KEYS = 129
NUM_BUCKETS = 32
MAX_DISTANCE = 2048
SAMPLE_POS0 = 8192
EPS = 1e-6
MASKED = -1e30
LOG2E = math.log2(math.e)

LANES = 128
N_SLABS = D_MODEL // LANES
Q_SUB = 128
ATTN_ROWS = 2048
VMEM_LIMIT = 60 * 1024 * 1024


def _cparams(*sem):
    return pltpu.CompilerParams(dimension_semantics=sem, vmem_limit_bytes=VMEM_LIMIT)


def _rms(x, g):
    ms = jnp.mean(x * x, axis=-1, keepdims=True)
    return x * lax.rsqrt(ms + EPS) * g


POOL_BASE = 24
POOL_LO = 8


def _pool_pieces(x_ref, g_ref, w_ref, sc_ref, nb_ref, hbuf, sbuf, x1_ref, t, tm):
    end = POOL_BASE + tm

    def norm():
        hbuf[POOL_BASE:end, :] = _rms(x_ref[0], g_ref[...])

    def group(gi, w):
        sl = slice(gi * POOL_GROUP, (gi + 1) * POOL_GROUP)
        n_stage = w.bit_length() - 1
        s = None
        for si in range(n_stage):
            sh = 1 << si
            lo = POOL_BASE if si == n_stage - 1 else POOL_LO
            if si == 0:
                val = hbuf[lo:end, sl] + hbuf[lo - sh:end - sh, sl]
            else:
                val = sbuf[si - 1, lo:end, :] + sbuf[si - 1, lo - sh:end - sh, :]
            if si == n_stage - 1:
                s = val
            else:
                sbuf[si, lo:end, :] = val
        pos = t * tm + lax.broadcasted_iota(jnp.int32, (tm, 1), 0)
        cnt = jnp.minimum(pos + 1, w).astype(F32)
        pooled = s / cnt - hbuf[POOL_BASE:end, sl]
        y = jnp.dot(pooled.astype(BF16), w_ref[gi], preferred_element_type=F32)
        x1_ref[:, sl] = x_ref[0, :, sl] + y * sc_ref[:, sl]

    def carry():
        nb_ref[0] = hbuf[end - POOL_STATE:end, :]
        hbuf[POOL_BASE - 16:POOL_BASE, :] = hbuf[end - 16:end, :]

    pieces = [norm]
    pieces += [functools.partial(group, gi, w) for gi, w in enumerate(POOL_WINDOWS)]
    return pieces + [carry]


def _mlp_tile(x1_ref, hs_ref, w1_ref, w2_ref, gf_ref, o_ref, us, tc, final_norm, between=()):
    between = list(between)
    n_chunks = D_FF // tc
    assert len(between) <= n_chunks
    for c in range(n_chunks):
        if c < len(between):
            between[c]()
        u = jnp.dot(hs_ref[...], w1_ref[:, c * tc:(c + 1) * tc], preferred_element_type=F32)
        us[:, c * tc:(c + 1) * tc] = jnp.square(jnp.maximum(u, 0.0)).astype(BF16)
    y = x1_ref[...] + jnp.dot(us[...], w2_ref[...], preferred_element_type=F32)
    if final_norm:
        y = _rms(y, gf_ref[...])
    o_ref[...] = y


def _run_pipelined(slots, xs1_ref, gmlp_ref, o_ref, os_ref, run):
    (x1a, hsa), (x1b, hsb) = slots
    j = pl.program_id(0)
    bs = xs1_ref.shape[0]

    @pl.when(j == 0)
    def _():
        x1b[...] = jnp.zeros(x1b.shape, F32)
        hsb[...] = jnp.zeros(hsb.shape, BF16)
        xs1 = xs1_ref[...]
        x1b[0:bs, :] = xs1
        hsb[0:bs, :] = _rms(xs1, gmlp_ref[...]).astype(BF16)

    @pl.when(j % 2 == 0)
    def _():
        run(x1a, hsa, x1b, hsb)

    @pl.when(j % 2 == 1)
    def _():
        run(x1b, hsb, x1a, hsa)

    @pl.when(j == 0)
    def _():
        os_ref[...] = o_ref[0:bs, :]


def _cast_rider(f32_refs, bf16_refs):
    def cast():
        for src, dst in zip(f32_refs, bf16_refs):
            dst[...] = src[...].astype(BF16)
    return cast


def _rider_specs(riders, n_tiles):
    chunk = lambda j: jnp.minimum(j, n_tiles - 1)
    args, in_specs, out_specs, out_shape = [], [], [], []
    for w, ln in riders:
        rows = w.shape[1] // n_tiles
        args.append(w)
        in_specs.append(pl.BlockSpec((None, rows, w.shape[2]), lambda j, ln=ln: (ln, chunk(j), 0)))
        out_specs.append(pl.BlockSpec((rows, w.shape[2]), lambda j: (chunk(j), 0)))
        out_shape.append(jax.ShapeDtypeStruct(w.shape[1:], BF16))
    return args, in_specs, out_specs, out_shape


def _pool_mlp_kernel(x_ref, xs1_ref, gmix_ref, wp_ref, sc_ref, gmlp_ref, w1_ref, w2_ref, gf_ref,
                     *refs, tm, nt, n_tiles, tc, ride):
    f32_refs, refs = refs[:ride], refs[ride:]
    o_ref, os_ref, nb_ref = refs[:3]
    bf16_refs, refs = refs[3:3 + ride], refs[3 + ride:]
    x1a, x1b, hsa, hsb, us, hbuf, sbuf = refs
    cast = _cast_rider(f32_refs, bf16_refs)
    j = pl.program_id(0)
    t = jnp.minimum(j, n_tiles - 1) % nt

    @pl.when(j == 0)
    def _():
        hbuf[0:POOL_BASE, :] = jnp.zeros((POOL_BASE, D_MODEL), F32)
        sbuf[...] = jnp.zeros(sbuf.shape, F32)

    @pl.when(t == 0)
    def _():
        hbuf[POOL_BASE - 16:POOL_BASE, :] = jnp.zeros((16, D_MODEL), F32)

    def run(x1_w, hs_w, x1_r, hs_r):
        def next_norm():
            hs_w[...] = _rms(x1_w[...], gmlp_ref[...]).astype(BF16)

        pieces = _pool_pieces(x_ref, gmix_ref, wp_ref, sc_ref, nb_ref, hbuf, sbuf, x1_w, t, tm)
        _mlp_tile(x1_r, hs_r, w1_ref, w2_ref, gf_ref, o_ref, us, tc, False,
                  between=pieces + [next_norm, cast])

    _run_pipelined(((x1a, hsa), (x1b, hsb)), xs1_ref, gmlp_ref, o_ref, os_ref, run)


def _pool_mlp(x, xs1, gmix, li, w_pool, scale, gmlp, w1, w2, gain_final, riders, tm=512, tc=512):
    B, T, D = x.shape
    Bs = xs1.shape[0]
    nt = T // tm
    n_tiles = B * nt

    def tile(j):
        jj = jnp.minimum(j, n_tiles - 1)
        return jj // nt, jj % nt

    one = pl.Buffered(1)
    ride_args, ride_in, ride_out, ride_shape = _rider_specs(riders, n_tiles)
    res = pl.pallas_call(
        functools.partial(_pool_mlp_kernel, tm=tm, nt=nt, n_tiles=n_tiles, tc=tc,
                          ride=len(riders)),
        grid=(n_tiles + 1,),
        in_specs=[
            pl.BlockSpec((1, tm, D), lambda j: (*tile(j), 0)),
            pl.BlockSpec((Bs, D), lambda j: (0, 0)),
            pl.BlockSpec((None, 1, D), lambda j: (li, 0, 0)),
            pl.BlockSpec((None,) + w_pool.shape[1:], lambda j: (li, 0, 0, 0)),
            pl.BlockSpec((None, 1, D), lambda j: (li, 0, 0)),
            pl.BlockSpec((None, 1, D), lambda j: (li, 0, 0)),
            pl.BlockSpec((D, D_FF), lambda j: (0, 0), pipeline_mode=one),
            pl.BlockSpec((D_FF, D), lambda j: (0, 0), pipeline_mode=one),
            pl.BlockSpec((1, D), lambda j: (0, 0)),
        ] + ride_in,
        out_specs=[
            pl.BlockSpec((tm, D), lambda j: (jnp.maximum(j - 1, 0), 0)),
            pl.BlockSpec((Bs, D), lambda j: (0, 0)),
            pl.BlockSpec((1, POOL_STATE, D), lambda j: (tile(j)[0], 0, 0)),
        ] + ride_out,
        out_shape=[
            jax.ShapeDtypeStruct((B * T, D), F32),
            jax.ShapeDtypeStruct((Bs, D), F32),
            jax.ShapeDtypeStruct((B, POOL_STATE, D), F32),
        ] + ride_shape,
        scratch_shapes=[
            pltpu.VMEM((tm, D), F32), pltpu.VMEM((tm, D), F32),
            pltpu.VMEM((tm, D), BF16), pltpu.VMEM((tm, D), BF16),
            pltpu.VMEM((tm, D_FF), BF16),
            pltpu.VMEM((POOL_BASE + tm, D), F32),
            pltpu.VMEM((len(POOL_WINDOWS) - 1, POOL_BASE + tm, POOL_GROUP), F32),
        ],
        compiler_params=_cparams("arbitrary"),
        name="pool_mlp",
    )(x, xs1, gmix, w_pool, scale, gmlp, w1, w2, gain_final, *ride_args)
    return (res[0].reshape(B, T, D),) + tuple(res[1:])


def _pool_sample_kernel(x_ref, st_ref, g_ref, w_ref, sc_ref, o_ref, h_ref):
    x = x_ref[...]
    h = _rms(x, g_ref[...])
    h_ref[...] = h
    for gi, w in enumerate(POOL_WINDOWS):
        sl = slice(gi * POOL_GROUP, (gi + 1) * POOL_GROUP)
        hg = h[:, sl]
        s = hg
        for j in range(1, w):
            s = s + st_ref[:, POOL_STATE - j, sl]
        pooled = s / float(min(SAMPLE_POS0 + 1, w)) - hg
        y = jnp.dot(pooled.astype(BF16), w_ref[gi], preferred_element_type=F32)
        o_ref[:, sl] = x[:, sl] + y * sc_ref[:, sl]


def _pool_sample(x, state, gain, w_pool, scale):
    Bs, D = x.shape
    return pl.pallas_call(
        _pool_sample_kernel,
        out_shape=[jax.ShapeDtypeStruct((Bs, D), F32), jax.ShapeDtypeStruct((Bs, D), F32)],
        compiler_params=pltpu.CompilerParams(vmem_limit_bytes=VMEM_LIMIT),
        name="pool_sample",
    )(x, state, gain, w_pool, scale)


def _proj_kernel(x_ref, g_ref, w_ref, *refs, tm, nc, scale, tails):
    n_br = len(BRANCHES)
    o_refs = refs[:n_br]
    if tails:
        tail_refs = refs[n_br:2 * n_br]
        hsl, tsl = refs[2 * n_br], refs[2 * n_br + 1]
        lhs = refs[2 * n_br + 2:]
        i = pl.program_id(1)
        nt = pl.num_programs(1)
    else:
        hsl = refs[n_br]
        lhs = refs[n_br + 1:]
    h = _rms(x_ref[0], g_ref[...])
    for c in range(N_SLABS):
        hsl[c] = h[:, c * LANES:(c + 1) * LANES]
    for g, (_, dil) in enumerate(BRANCHES):
        n = tm // dil
        if dil == 1:
            lhs[g][...] = h.astype(BF16)
        else:
            for r in range(dil):
                for c in range(N_SLABS):
                    lhs[g][r * n:(r + 1) * n, c * LANES:(c + 1) * LANES] = (
                        hsl[c, pl.ds(r, n, stride=dil), :].astype(BF16))
        res = jnp.dot(lhs[g][...], w_ref[:, g * nc:(g + 1) * nc], preferred_element_type=F32)
        if scale != 1.0:
            res = res * scale
        for r in range(dil):
            o_refs[g][0, r] = res[r * n:(r + 1) * n].astype(o_refs[g].dtype)

        if tails:
            tail_ref, rows = tail_refs[g], tails[g]
            if rows < tm:
                @pl.when(i == nt - 1)
                def _(tail_ref=tail_ref, rows=rows, res=res):
                    tail_ref[0] = res[tm - rows:tm]
            else:
                @pl.when(i >= nt - rows // tm)
                def _(tail_ref=tail_ref, res=res, dil=dil, n=n):
                    if dil == 1:
                        tail_ref[0] = res
                    else:
                        for c in range(nc // LANES):
                            for r in range(dil):
                                tsl[c, pl.ds(r, n, stride=dil), :] = (
                                    res[r * n:(r + 1) * n, c * LANES:(c + 1) * LANES])
                        for c in range(nc // LANES):
                            tail_ref[0, :, c * LANES:(c + 1) * LANES] = tsl[c]


def _proj(x, gains, gi, w, wi, nc, tm, scale=1.0, tails=()):
    B, T, D = x.shape
    nt = T // tm
    out_specs, out_shape = [], []
    for (_, dil) in BRANCHES:
        out_specs.append(pl.BlockSpec((1, dil, tm // dil, nc), lambda b, i: (b, 0, i, 0)))
        out_shape.append(jax.ShapeDtypeStruct((B, dil, T // dil, nc), BF16))
    scratch = [pltpu.VMEM((N_SLABS, tm, LANES), F32)]
    if tails:
        for rows, (_, dil) in zip(tails, BRANCHES):
            assert rows >= tm or dil == 1
            if rows < tm:
                out_specs.append(pl.BlockSpec((1, rows, nc), lambda b, i: (b, 0, 0)))
            else:
                first = nt - rows // tm
                out_specs.append(pl.BlockSpec(
                    (1, tm, nc), lambda b, i, first=first: (b, jnp.maximum(i - first, 0), 0)))
            out_shape.append(jax.ShapeDtypeStruct((B, rows, nc), F32))
        scratch.append(pltpu.VMEM((nc // LANES, tm, LANES), F32))
    scratch += [pltpu.VMEM((tm, D), BF16) for _ in BRANCHES]
    return pl.pallas_call(
        functools.partial(_proj_kernel, tm=tm, nc=nc, scale=scale, tails=tuple(tails)),
        grid=(B, nt),
        in_specs=[
            pl.BlockSpec((1, tm, D), lambda b, i: (b, i, 0)),
            pl.BlockSpec((None, 1, D), lambda b, i: (gi, 0, 0)),
            pl.BlockSpec((None, D, len(BRANCHES) * nc), lambda b, i: (wi, 0, 0),
                         pipeline_mode=pl.Buffered(1)),
        ],
        out_specs=out_specs,
        out_shape=out_shape,
        scratch_shapes=scratch,
        compiler_params=_cparams("arbitrary", "arbitrary"),
        name="proj_kv" if tails else "proj_q",
    )(x, gains, w)


def _proj_sample_kernel(x_ref, g_ref, w_ref, o_ref, *, scale):
    h = _rms(x_ref[...], g_ref[...]).astype(BF16)
    res = jnp.dot(h, w_ref[...], preferred_element_type=F32)
    if scale != 1.0:
        res = res * scale
    o_ref[...] = res


def _proj_sample(x, gains, gi, w, wi, scale=1.0, tn=3072):
    Bs, D = x.shape
    nc = w.shape[2]
    return pl.pallas_call(
        functools.partial(_proj_sample_kernel, scale=scale),
        grid=(nc // tn,),
        in_specs=[
            pl.BlockSpec((Bs, D), lambda j: (0, 0)),
            pl.BlockSpec((None, 1, D), lambda j: (gi, 0, 0)),
            pl.BlockSpec((None, D, tn), lambda j: (wi, 0, j)),
        ],
        out_specs=pl.BlockSpec((Bs, tn), lambda j: (0, j)),
        out_shape=jax.ShapeDtypeStruct((Bs, nc), F32),
        compiler_params=_cparams("arbitrary"),
        name="proj_sample",
    )(x, gains, w)


def _attn_kernel(q_ref, kv_ref, bias_ref, o_ref, ml_ref, kprev, vprev, *, tq, rb):
    first = pl.program_id(2) == 0
    lane = lax.broadcasted_iota(jnp.int32, (Q_SUB, LANES), 1)
    ones = jnp.ones((2 * Q_SUB, HEAD_DIM), BF16)
    if rb == 1:
        @pl.when(first)
        def _():
            kprev[...] = jnp.zeros(kprev.shape, BF16)
            vprev[...] = jnp.zeros(vprev.shape, BF16)
    for rr in range(rb):
        for s in range(tq // Q_SUB):
            r0 = s * Q_SUB

            def keys(h, col0, s=s, r0=r0, rr=rr):
                cs = slice(col0 + h * HEAD_DIM, col0 + (h + 1) * HEAD_DIM)
                if s > 0:
                    return kv_ref[0, rr, r0 - Q_SUB:r0 + Q_SUB, cs]
                cur = kv_ref[0, rr, 0:Q_SUB, cs]
                if rb == 1:
                    prev = (kprev if col0 == 0 else vprev)[:, h * HEAD_DIM:(h + 1) * HEAD_DIM]
                else:
                    prev = jnp.zeros((Q_SUB, HEAD_DIM), BF16)
                return jnp.concatenate([prev, cur], axis=0)

            if s > 0:
                boff = 0
            elif rb > 1:
                boff = HEADS
            else:
                boff = jnp.where(first, HEADS, 0)
            ml_tile = jnp.zeros((Q_SUB, LANES), F32)
            for h in range(HEADS):
                cs = slice(h * HEAD_DIM, (h + 1) * HEAD_DIM)
                q = q_ref[0, rr, r0:r0 + Q_SUB, cs]
                sc = lax.dot_general(q, keys(h, 0), (((1,), (1,)), ((), ())),
                                     preferred_element_type=F32)
                sc = sc + bias_ref[boff + h]
                m = jnp.max(sc, axis=-1, keepdims=True)
                p = jnp.exp2(sc - m).astype(BF16)
                v_ext = jnp.concatenate([keys(h, D_MODEL), ones], axis=1)
                o = jnp.dot(p, v_ext, preferred_element_type=F32)
                o_ref[0, rr, r0:r0 + Q_SUB, cs] = o[:, :HEAD_DIM].astype(o_ref.dtype)
                ml_tile = jnp.where(lane == h, m, ml_tile)
                ml_tile = jnp.where(lane == HEADS + h, o[:, HEAD_DIM:], ml_tile)
            ml_ref[0, rr, r0:r0 + Q_SUB, :] = ml_tile
    if rb == 1:
        kprev[...] = kv_ref[0, 0, tq - Q_SUB:tq, 0:D_MODEL]
        vprev[...] = kv_ref[0, 0, tq - Q_SUB:tq, D_MODEL:2 * D_MODEL]


def _attn_branch(q, kv, bias, tq, rb):
    B, dil, L, D = q.shape
    assert rb == 1 or tq == L
    return pl.pallas_call(
        functools.partial(_attn_kernel, tq=tq, rb=rb),
        grid=(B, dil // rb, L // tq),
        in_specs=[
            pl.BlockSpec((1, rb, tq, D), lambda b, r, i: (b, r, i, 0)),
            pl.BlockSpec((1, rb, tq, 2 * D), lambda b, r, i: (b, r, i, 0)),
            pl.BlockSpec(bias.shape, lambda b, r, i: (0, 0, 0)),
        ],
        out_specs=[
            pl.BlockSpec((1, rb, tq, D), lambda b, r, i: (b, r, i, 0)),
            pl.BlockSpec((1, rb, tq, LANES), lambda b, r, i: (b, r, i, 0)),
        ],
        out_shape=[
            jax.ShapeDtypeStruct((B, dil, L, D), BF16),
            jax.ShapeDtypeStruct((B, dil, L, LANES), F32),
        ],
        scratch_shapes=[pltpu.VMEM((Q_SUB, D), BF16), pltpu.VMEM((Q_SUB, D), BF16)],
        compiler_params=_cparams("arbitrary", "arbitrary", "arbitrary"),
        name=f"attn_d{dil}",
    )(q, kv, bias)


def _merge_pieces(x_ref, o_refs, ml_refs, wo_ref, osls, wsl, lhs, x1_ref, tm):
    def weights():
        tiles = []
        for g, (_, dil) in enumerate(BRANCHES):
            if dil == 1:
                tiles.append(ml_refs[g][0, 0])
            else:
                n = tm // dil
                for r in range(dil):
                    wsl[g, pl.ds(r, n, stride=dil), :] = ml_refs[g][0, r]
                tiles.append(wsl[g])
        m = jnp.maximum(jnp.maximum(tiles[0], tiles[1]), tiles[2])
        us = [jnp.exp2(tl - m) for tl in tiles]
        ls = [pltpu.roll(tl, LANES - HEADS, axis=1) for tl in tiles]
        den = us[0] * ls[0] + us[1] * ls[1] + us[2] * ls[2]
        for g in range(len(BRANCHES)):
            wsl[g] = us[g] / den

    def interleave(g, r_lo, r_hi):
        dil = BRANCHES[g][1]
        n = tm // dil
        for r in range(r_lo, r_hi):
            for c in range(N_SLABS):
                osls[g][c, pl.ds(r, n, stride=dil), :] = (
                    o_refs[g][0, r, :, c * LANES:(c + 1) * LANES].astype(F32))

    def combine(c_lo, c_hi):
        for c in range(c_lo, c_hi):
            cs = slice(c * LANES, (c + 1) * LANES)
            merged = wsl[0, :, c:c + 1] * o_refs[0][0, 0, :, cs].astype(F32)
            for g in range(1, len(BRANCHES)):
                merged = merged + wsl[g, :, c:c + 1] * osls[g][c]
            lhs[:, cs] = merged.astype(BF16)

    def project():
        x1_ref[...] = x_ref[0] + jnp.dot(lhs[...], wo_ref[...], preferred_element_type=F32)

    pieces = [weights]
    for g, (_, dil) in enumerate(BRANCHES):
        if dil > 1:
            half = max(dil // 2, 1) if dil >= 8 else dil
            pieces += [functools.partial(interleave, g, lo, min(lo + half, dil))
                       for lo in range(0, dil, half)]
    pieces += [functools.partial(combine, 0, N_SLABS // 2),
               functools.partial(combine, N_SLABS // 2, N_SLABS), project]
    return pieces


def _merge_mlp_kernel(x_ref, xs1_ref, o0_ref, o1_ref, o2_ref, ml0_ref, ml1_ref, ml2_ref, wo_ref,
                      gmlp_ref, w1_ref, w2_ref, gf_ref, *refs, tm, tc, final_norm, ride):
    f32_refs, refs = refs[:ride], refs[ride:]
    out_ref, os_ref = refs[:2]
    bf16_refs, refs = refs[2:2 + ride], refs[2 + ride:]
    x1a, x1b, hsa, hsb, us, osl1, osl2, wsl, lhs = refs
    cast = _cast_rider(f32_refs, bf16_refs)

    def run(x1_w, hs_w, x1_r, hs_r):
        def next_norm():
            hs_w[...] = _rms(x1_w[...], gmlp_ref[...]).astype(BF16)
            cast()

        pieces = _merge_pieces(x_ref, (o0_ref, o1_ref, o2_ref), (ml0_ref, ml1_ref, ml2_ref),
                               wo_ref, (None, osl1, osl2), wsl, lhs, x1_w, tm)
        _mlp_tile(x1_r, hs_r, w1_ref, w2_ref, gf_ref, out_ref, us, tc, final_norm,
                  between=pieces + [next_norm])

    _run_pipelined(((x1a, hsa), (x1b, hsb)), xs1_ref, gmlp_ref, out_ref, os_ref, run)


def _merge_mlp(x, xs1, os_, mls, wo, wi, gmlp, w1, w2, gain_final, li, final_norm, riders,
               tm=512, tc=512):
    B, T, D = x.shape
    Bs = xs1.shape[0]
    nt = T // tm
    n_tiles = B * nt

    def tile(j):
        jj = jnp.minimum(j, n_tiles - 1)
        return jj // nt, jj % nt

    one = pl.Buffered(1)
    in_specs = [pl.BlockSpec((1, tm, D), lambda j: (*tile(j), 0)),
                pl.BlockSpec((Bs, D), lambda j: (0, 0))]
    for (_, dil) in BRANCHES:
        in_specs.append(pl.BlockSpec((1, dil, tm // dil, D), lambda j: (tile(j)[0], 0, tile(j)[1], 0)))
    for (_, dil) in BRANCHES:
        in_specs.append(pl.BlockSpec((1, dil, tm // dil, LANES),
                                     lambda j: (tile(j)[0], 0, tile(j)[1], 0)))
    in_specs += [
        pl.BlockSpec((None, D, D), lambda j: (wi, 0, 0), pipeline_mode=one),
        pl.BlockSpec((None, 1, D), lambda j: (li, 0, 0)),
        pl.BlockSpec((D, D_FF), lambda j: (0, 0), pipeline_mode=one),
        pl.BlockSpec((D_FF, D), lambda j: (0, 0), pipeline_mode=one),
        pl.BlockSpec((1, D), lambda j: (0, 0)),
    ]
    ride_args, ride_in, ride_out, ride_shape = _rider_specs(riders, n_tiles)
    res = pl.pallas_call(
        functools.partial(_merge_mlp_kernel, tm=tm, tc=tc, final_norm=final_norm,
                          ride=len(riders)),
        grid=(n_tiles + 1,),
        in_specs=in_specs + ride_in,
        out_specs=[pl.BlockSpec((tm, D), lambda j: (jnp.maximum(j - 1, 0), 0)),
                   pl.BlockSpec((Bs, D), lambda j: (0, 0))] + ride_out,
        out_shape=[jax.ShapeDtypeStruct((B * T, D), F32),
                   jax.ShapeDtypeStruct((Bs, D), F32)] + ride_shape,
        scratch_shapes=[
            pltpu.VMEM((tm, D), F32), pltpu.VMEM((tm, D), F32),
            pltpu.VMEM((tm, D), BF16), pltpu.VMEM((tm, D), BF16),
            pltpu.VMEM((tm, D_FF), BF16),
            pltpu.VMEM((N_SLABS, tm, LANES), F32), pltpu.VMEM((N_SLABS, tm, LANES), F32),
            pltpu.VMEM((len(BRANCHES), tm, LANES), F32),
            pltpu.VMEM((tm, D), BF16),
        ],
        compiler_params=_cparams("arbitrary"),
        name="merge_mlp",
    )(x, xs1, *os_, *mls, wo, gmlp, w1, w2, gain_final, *ride_args)
    return (res[0].reshape(B, T, D),) + tuple(res[1:])


def _attn_sample_kernel(q_ref, kvn_ref, c0_ref, c1_ref, c2_ref, bias_ref, o_ref, *, nb):
    caches = (c0_ref, c1_ref, c2_ref)
    for i in range(nb):
        sc_c, sc_n = [], []
        for g in range(len(BRANCHES)):
            q = q_ref[i, g]
            kc = caches[g][i, :, 0]
            sc_c.append(jnp.sum(kc * q[None], axis=-1, keepdims=True) + bias_ref[g, 0:N_KEYS - 1])
            sc_n.append(jnp.sum(kvn_ref[i, g, 0] * q, axis=-1, keepdims=True)
                        + bias_ref[g, N_KEYS - 1])
        m = sc_n[0]
        for g in range(len(BRANCHES)):
            m = jnp.maximum(m, jnp.maximum(jnp.max(sc_c[g], axis=0), sc_n[g]))
        num = jnp.zeros((HEADS, HEAD_DIM), F32)
        den = jnp.zeros((HEADS, HEAD_DIM), F32)
        for g in range(len(BRANCHES)):
            p_c = jnp.exp2(sc_c[g] - m[None])
            p_n = jnp.exp2(sc_n[g] - m)
            den = den + jnp.sum(p_c, axis=0) + p_n
            num = num + jnp.sum(p_c * caches[g][i, :, 1], axis=0) + p_n * kvn_ref[i, g, 1]
        o_ref[i] = num / den


def _attn_sample(q, kvn, caches, bias, nb=4):
    Bs = q.shape[0]
    in_specs = [
        pl.BlockSpec((nb,) + q.shape[1:], lambda b: (b, 0, 0, 0)),
        pl.BlockSpec((nb,) + kvn.shape[1:], lambda b: (b, 0, 0, 0, 0)),
    ]
    for c in caches:
        in_specs.append(pl.BlockSpec((nb, N_KEYS - 1, None, 2, HEADS, HEAD_DIM),
                                     lambda b: (b, 0, 0, 0, 0, 0)))
    in_specs.append(pl.BlockSpec(bias.shape, lambda b: (0, 0, 0, 0)))
    return pl.pallas_call(
        functools.partial(_attn_sample_kernel, nb=nb),
        grid=(Bs // nb,),
        in_specs=in_specs,
        out_specs=pl.BlockSpec((nb, HEADS, HEAD_DIM), lambda b: (b, 0, 0)),
        out_shape=jax.ShapeDtypeStruct((Bs, HEADS, HEAD_DIM), F32),
        compiler_params=_cparams("arbitrary"),
        name="attn_sample",
    )(q, kvn, *caches, bias)


def _out_sample_kernel(x_ref, o_ref, wo_ref, out_ref):
    out_ref[...] = x_ref[...] + jnp.dot(o_ref[...].astype(BF16), wo_ref[...],
                                        preferred_element_type=F32)


def _out_sample(x, o, wo):
    return pl.pallas_call(
        _out_sample_kernel,
        out_shape=jax.ShapeDtypeStruct(x.shape, F32),
        compiler_params=pltpu.CompilerParams(vmem_limit_bytes=VMEM_LIMIT),
        name="out_sample",
    )(x, o, wo)


def _t5_bucket(dist):
    max_exact = NUM_BUCKETS // 2
    df = jnp.maximum(dist, 1).astype(F32)
    large = max_exact + (jnp.log(df / max_exact) / math.log(MAX_DISTANCE / max_exact)
                         * (NUM_BUCKETS - max_exact)).astype(jnp.int32)
    large = jnp.minimum(large, NUM_BUCKETS - 1)
    return jnp.where(dist < max_exact, dist, large)


def _bias_tables(rel_bias):
    out = []
    for g, (w, d) in enumerate(BRANCHES):
        dist = jnp.arange(N_KEYS, dtype=jnp.int32) * d
        out.append(rel_bias[_t5_bucket(dist)][:, g * HEADS:(g + 1) * HEADS])
    return out


def _block_bias(tab):
    pad_lo = jnp.full((HEADS, Q_SUB - 1), MASKED, F32)
    pad_hi = jnp.full((HEADS, Q_SUB), MASKED, F32)
    f = jnp.concatenate([pad_lo, tab[::-1].T, pad_hi], axis=1)
    width = f.shape[1]
    flat = jnp.tile(f, (1, Q_SUB))[:, :Q_SUB * (width - 1)]
    toep = flat.reshape(HEADS, Q_SUB, width - 1)
    return toep[:, :, Q_SUB - 1:Q_SUB - 1 + 2 * Q_SUB]


def kernel(x_prompt, x_sample, state_pool, cache_kv_w128, cache_kv_w512, cache_kv_w2048,
           norm_mix, pool_w, pool_scale, norm_mlp, mlp_in, mlp_out, norm_kv, w_kv, w_q, w_o,
           rel_bias, norm_final):
    B, T, D = x_prompt.shape
    Bs = x_sample.shape[0]
    n_a = pool_w.shape[0]
    depth = norm_mix.shape[0]
    n_br = len(BRANCHES)
    qk_scale = HEAD_DIM ** -0.5

    pool_w_b = pool_w.astype(BF16)
    w1_b, w2_b = mlp_in[0].astype(BF16), mlp_out[0].astype(BF16)
    ride_proj = n_a >= 2
    extra = {0: [w_kv[None]], 1: [w_q.reshape(1, -1, w_q.shape[-1]), w_o.reshape(1, -1, D)]}
    if not ride_proj:
        w_kv_b, w_q_b, w_o_b = w_kv.astype(BF16)[None], w_q.astype(BF16), w_o.astype(BF16)
    row = lambda v: v.reshape(1, D)
    g_mix = norm_mix.reshape(depth, 1, D)
    g_mlp = norm_mlp.reshape(depth, 1, D)
    g_kv = norm_kv.reshape(1, 1, D)
    g_scale = pool_scale.reshape(n_a, 1, D)

    tabs = _bias_tables(rel_bias)
    no_prev = np.arange(2 * Q_SUB)[None, None, :] < Q_SUB
    blk_bias = []
    for t in tabs:
        bb = _block_bias(t * LOG2E)
        blk_bias.append(jnp.concatenate([bb, jnp.where(no_prev, MASKED, bb)], axis=0))
    smp_bias = jnp.stack([jnp.broadcast_to((t * LOG2E)[::-1][:, :, None], (N_KEYS, HEADS, HEAD_DIM))
                          for t in tabs])

    caches = [c.reshape(Bs, N_KEYS - 1, d, 2, HEADS, HEAD_DIM)
              for c, (w, d) in zip((cache_kv_w128, cache_kv_w512, cache_kv_w2048), BRANCHES)]

    xp = x_prompt
    xs = x_sample.reshape(Bs, D)
    pool_p, pool_s, kv_p = [], [], []
    kvs = kvn = None
    for l in range(depth):
        last = l == depth - 1
        riders = [] if last else [(mlp_in, l + 1), (mlp_out, l + 1)]
        if l < n_a:
            if ride_proj and l in extra:
                riders += [(w, 0) for w in extra[l]]
            xs1, hs = _pool_sample(xs, state_pool[l], row(norm_mix[l]), pool_w_b[l], row(pool_scale[l]))
            pool_s.append(jnp.concatenate([state_pool[l][:, 1:], hs[:, None, :]], axis=1))
            res = _pool_mlp(xp, xs1, g_mix, l, pool_w_b, g_scale, g_mlp, w1_b, w2_b,
                            row(norm_final), riders)
            xp, xs, nbuf = res[:3]
            pool_p.append(nbuf)
            cast = res[3:]
            if ride_proj and l == 0:
                w_kv_b = cast[2][None]
            if ride_proj and l == 1:
                w_q_b, w_o_b = cast[2].reshape(w_q.shape), cast[3].reshape(w_o.shape)
        else:
            lb = l - n_a
            if l == n_a:
                tails = tuple(min(w, T) for (w, _) in BRANCHES)
                res = _proj(xp, g_kv, 0, w_kv_b, 0, 2 * D, 256, tails=tails)
                kvs = res[:n_br]
                kv_p = [t.reshape(B, rows, 2, HEADS, HEAD_DIM) for t, rows in zip(res[n_br:], tails)]
                kvn = _proj_sample(xs, g_kv, 0, w_kv_b, 0)
            qs_p = _proj(xp, g_mix, l, w_q_b, lb, D, 1024, scale=qk_scale * LOG2E)
            qs = _proj_sample(xs, g_mix, l, w_q_b, lb, scale=qk_scale * LOG2E)
            os_, lses = [], []
            for g, (w, d) in enumerate(BRANCHES):
                L = T // d
                tq, rb = (L, min(d, ATTN_ROWS // L)) if L < ATTN_ROWS else (ATTN_ROWS, 1)
                o_g, ml_g = _attn_branch(qs_p[g], kvs[g], blk_bias[g], tq, rb)
                os_.append(o_g)
                lses.append(ml_g)
            o_s = _attn_sample(qs.reshape(Bs, n_br, HEADS, HEAD_DIM),
                               kvn.reshape(Bs, n_br, 2, HEADS, HEAD_DIM), caches, smp_bias)
            xs1 = _out_sample(xs, o_s.reshape(Bs, D), w_o_b[lb])
            res = _merge_mlp(xp, xs1, os_, lses, w_o_b, lb, g_mlp, w1_b, w2_b, row(norm_final),
                             l, last, riders)
            xp, xs = res[:2]
            cast = res[2:]
        if not last:
            w1_b, w2_b = cast[:2]

    kv_s = [kvn[:, g * 2 * D:(g + 1) * 2 * D].reshape(Bs, 1, 2, HEADS, HEAD_DIM) for g in range(n_br)]
    return (xp, xs.reshape(Bs, 1, D), jnp.stack(pool_p), jnp.stack(pool_s),
            kv_p[0], kv_s[0], kv_p[1], kv_s[1], kv_p[2], kv_s[2])
```

```python
import functools
import math

import jax
import jax.numpy as jnp
import numpy as np
from jax import lax
from jax.experimental import pallas as pl
from jax.experimental.pallas import tpu as pltpu

F32 = jnp.float32
BF16 = jnp.bfloat16

D_MODEL = 1024
HEADS = 8
HEAD_DIM = 128
D_FF = 4 * D_MODEL
POOL_WINDOWS = (2, 4, 8, 16)
POOL_GROUP = D_MODEL // len(POOL_WINDOWS)
POOL_STATE = max(POOL_WINDOWS) - 1
BRANCHES = ((128, 1), (512, 4), (2048, 16))
N_KEYS = 129
NUM_BUCKETS = 32
MAX_DISTANCE = 2048
SAMPLE_POS0 = 8192
EPS = 1e-6
MASKED = -1e30
LOG2E = math.log2(math.e)

LANES = 128
N_SLABS = D_MODEL // LANES
Q_SUB = 128
ATTN_ROWS = 2048
VMEM_LIMIT = 60 * 1024 * 1024


def _cparams(*sem):
    return pltpu.CompilerParams(dimension_semantics=sem, vmem_limit_bytes=VMEM_LIMIT)


def _rms(x, g):
    ms = jnp.mean(x * x, axis=-1, keepdims=True)
    return x * lax.rsqrt(ms + EPS) * g


POOL_BASE = 24
POOL_LO = 8


def _pool_pieces(x_ref, g_ref, w_ref, sc_ref, nb_ref, hbuf, sbuf, x1_ref, t, tm):
    end = POOL_BASE + tm

    def norm():
        hbuf[POOL_BASE:end, :] = _rms(x_ref[0], g_ref[...])

    def group(gi, w):
        sl = slice(gi * POOL_GROUP, (gi + 1) * POOL_GROUP)
        n_stage = w.bit_length() - 1
        s = None
        for si in range(n_stage):
            sh = 1 << si
            lo = POOL_BASE if si == n_stage - 1 else POOL_LO
            if si == 0:
                val = hbuf[lo:end, sl] + hbuf[lo - sh:end - sh, sl]
            else:
                val = sbuf[si - 1, lo:end, :] + sbuf[si - 1, lo - sh:end - sh, :]
            if si == n_stage - 1:
                s = val
            else:
                sbuf[si, lo:end, :] = val
        pos = t * tm + lax.broadcasted_iota(jnp.int32, (tm, 1), 0)
        cnt = jnp.minimum(pos + 1, w).astype(F32)
        pooled = s / cnt - hbuf[POOL_BASE:end, sl]
        y = jnp.dot(pooled.astype(BF16), w_ref[gi], preferred_element_type=F32)
        x1_ref[:, sl] = x_ref[0, :, sl] + y * sc_ref[:, sl]

    def carry():
        nb_ref[0] = hbuf[end - POOL_STATE:end, :]
        hbuf[POOL_BASE - 16:POOL_BASE, :] = hbuf[end - 16:end, :]

    pieces = [norm]
    pieces += [functools.partial(group, gi, w) for gi, w in enumerate(POOL_WINDOWS)]
    return pieces + [carry]


def _mlp_tile(x1_ref, hs_ref, w1_ref, w2_ref, gf_ref, o_ref, us, tc, final_norm, between=()):
    between = list(between)
    n_chunks = D_FF // tc
    assert len(between) <= n_chunks
    for c in range(n_chunks):
        if c < len(between):
            between[c]()
        u = jnp.dot(hs_ref[...], w1_ref[:, c * tc:(c + 1) * tc], preferred_element_type=F32)
        us[:, c * tc:(c + 1) * tc] = jnp.square(jnp.maximum(u, 0.0)).astype(BF16)
    y = x1_ref[...] + jnp.dot(us[...], w2_ref[...], preferred_element_type=F32)
    if final_norm:
        y = _rms(y, gf_ref[...])
    o_ref[...] = y


def _run_pipelined(slots, xs1_ref, gmlp_ref, o_ref, os_ref, run):
    (x1a, hsa), (x1b, hsb) = slots
    j = pl.program_id(0)
    bs = xs1_ref.shape[0]

    @pl.when(j == 0)
    def _():
        x1b[...] = jnp.zeros(x1b.shape, F32)
        hsb[...] = jnp.zeros(hsb.shape, BF16)
        xs1 = xs1_ref[...]
        x1b[0:bs, :] = xs1
        hsb[0:bs, :] = _rms(xs1, gmlp_ref[...]).astype(BF16)

    @pl.when(j % 2 == 0)
    def _():
        run(x1a, hsa, x1b, hsb)

    @pl.when(j % 2 == 1)
    def _():
        run(x1b, hsb, x1a, hsa)

    @pl.when(j == 0)
    def _():
        os_ref[...] = o_ref[0:bs, :]


def _cast_rider(f32_refs, bf16_refs):
    def cast():
        for src, dst in zip(f32_refs, bf16_refs):
            dst[...] = src[...].astype(BF16)
    return cast


def _rider_specs(riders, n_tiles):
    chunk = lambda j: jnp.minimum(j, n_tiles - 1)
    args, in_specs, out_specs, out_shape = [], [], [], []
    for w, ln in riders:
        rows = w.shape[1] // n_tiles
        args.append(w)
        in_specs.append(pl.BlockSpec((None, rows, w.shape[2]), lambda j, ln=ln: (ln, chunk(j), 0)))
        out_specs.append(pl.BlockSpec((rows, w.shape[2]), lambda j: (chunk(j), 0)))
        out_shape.append(jax.ShapeDtypeStruct(w.shape[1:], BF16))
    return args, in_specs, out_specs, out_shape


def _pool_mlp_kernel(x_ref, xs1_ref, gmix_ref, wp_ref, sc_ref, gmlp_ref, w1_ref, w2_ref, gf_ref,
                     *refs, tm, nt, n_tiles, tc, ride):
    f32_refs, refs = refs[:ride], refs[ride:]
    o_ref, os_ref, nb_ref = refs[:3]
    bf16_refs, refs = refs[3:3 + ride], refs[3 + ride:]
    x1a, x1b, hsa, hsb, us, hbuf, sbuf = refs
    cast = _cast_rider(f32_refs, bf16_refs)
    j = pl.program_id(0)
    t = jnp.minimum(j, n_tiles - 1) % nt

    @pl.when(j == 0)
    def _():
        hbuf[0:POOL_BASE, :] = jnp.zeros((POOL_BASE, D_MODEL), F32)
        sbuf[...] = jnp.zeros(sbuf.shape, F32)

    @pl.when(t == 0)
    def _():
        hbuf[POOL_BASE - 16:POOL_BASE, :] = jnp.zeros((16, D_MODEL), F32)

    def run(x1_w, hs_w, x1_r, hs_r):
        def next_norm():
            hs_w[...] = _rms(x1_w[...], gmlp_ref[...]).astype(BF16)

        pieces = _pool_pieces(x_ref, gmix_ref, wp_ref, sc_ref, nb_ref, hbuf, sbuf, x1_w, t, tm)
        _mlp_tile(x1_r, hs_r, w1_ref, w2_ref, gf_ref, o_ref, us, tc, False,
                  between=pieces + [next_norm, cast])

    _run_pipelined(((x1a, hsa), (x1b, hsb)), xs1_ref, gmlp_ref, o_ref, os_ref, run)


def _pool_mlp(x, xs1, gmix, li, w_pool, scale, gmlp, w1, w2, gain_final, riders, tm=512, tc=512):
    B, T, D = x.shape
    Bs = xs1.shape[0]
    nt = T // tm
    n_tiles = B * nt

    def tile(j):
        jj = jnp.minimum(j, n_tiles - 1)
        return jj // nt, jj % nt

    one = pl.Buffered(1)
    ride_args, ride_in, ride_out, ride_shape = _rider_specs(riders, n_tiles)
    res = pl.pallas_call(
        functools.partial(_pool_mlp_kernel, tm=tm, nt=nt, n_tiles=n_tiles, tc=tc,
                          ride=len(riders)),
        grid=(n_tiles + 1,),
        in_specs=[
            pl.BlockSpec((1, tm, D), lambda j: (*tile(j), 0)),
            pl.BlockSpec((Bs, D), lambda j: (0, 0)),
            pl.BlockSpec((None, 1, D), lambda j: (li, 0, 0)),
            pl.BlockSpec((None,) + w_pool.shape[1:], lambda j: (li, 0, 0, 0)),
            pl.BlockSpec((None, 1, D), lambda j: (li, 0, 0)),
            pl.BlockSpec((None, 1, D), lambda j: (li, 0, 0)),
            pl.BlockSpec((D, D_FF), lambda j: (0, 0), pipeline_mode=one),
            pl.BlockSpec((D_FF, D), lambda j: (0, 0), pipeline_mode=one),
            pl.BlockSpec((1, D), lambda j: (0, 0)),
        ] + ride_in,
        out_specs=[
            pl.BlockSpec((tm, D), lambda j: (jnp.maximum(j - 1, 0), 0)),
            pl.BlockSpec((Bs, D), lambda j: (0, 0)),
            pl.BlockSpec((1, POOL_STATE, D), lambda j: (tile(j)[0], 0, 0)),
        ] + ride_out,
        out_shape=[
            jax.ShapeDtypeStruct((B * T, D), F32),
            jax.ShapeDtypeStruct((Bs, D), F32),
            jax.ShapeDtypeStruct((B, POOL_STATE, D), F32),
        ] + ride_shape,
        scratch_shapes=[
            pltpu.VMEM((tm, D), F32), pltpu.VMEM((tm, D), F32),
            pltpu.VMEM((tm, D), BF16), pltpu.VMEM((tm, D), BF16),
            pltpu.VMEM((tm, D_FF), BF16),
            pltpu.VMEM((POOL_BASE + tm, D), F32),
            pltpu.VMEM((len(POOL_WINDOWS) - 1, POOL_BASE + tm, POOL_GROUP), F32),
        ],
        compiler_params=_cparams("arbitrary"),
        name="pool_mlp",
    )(x, xs1, gmix, w_pool, scale, gmlp, w1, w2, gain_final, *ride_args)
    return (res[0].reshape(B, T, D),) + tuple(res[1:])


def _pool_sample_kernel(x_ref, st_ref, g_ref, w_ref, sc_ref, o_ref, h_ref):
    x = x_ref[...]
    h = _rms(x, g_ref[...])
    h_ref[...] = h
    for gi, w in enumerate(POOL_WINDOWS):
        sl = slice(gi * POOL_GROUP, (gi + 1) * POOL_GROUP)
        hg = h[:, sl]
        s = hg
        for j in range(1, w):
            s = s + st_ref[:, POOL_STATE - j, sl]
        pooled = s / float(min(SAMPLE_POS0 + 1, w)) - hg
        y = jnp.dot(pooled.astype(BF16), w_ref[gi], preferred_element_type=F32)
        o_ref[:, sl] = x[:, sl] + y * sc_ref[:, sl]


def _pool_sample(x, state, gain, w_pool, scale):
    Bs, D = x.shape
    return pl.pallas_call(
        _pool_sample_kernel,
        out_shape=[jax.ShapeDtypeStruct((Bs, D), F32), jax.ShapeDtypeStruct((Bs, D), F32)],
        compiler_params=pltpu.CompilerParams(vmem_limit_bytes=VMEM_LIMIT),
        name="pool_sample",
    )(x, state, gain, w_pool, scale)


def _proj_kernel(x_ref, g_ref, w_ref, *refs, tm, nc, scale, tails):
    n_br = len(BRANCHES)
    o_refs = refs[:n_br]
    if tails:
        tail_refs = refs[n_br:2 * n_br]
        hsl, stage, tsl, tsl2 = refs[2 * n_br:2 * n_br + 4]
        lhs = refs[2 * n_br + 4:]
        i = pl.program_id(1)
        nt = pl.num_programs(1)
    else:
        hsl, stage = refs[n_br:n_br + 2]
        lhs = refs[n_br + 2:]
    h = _rms(x_ref[0], g_ref[...])
    for c in range(N_SLABS):
        hsl[c] = h[:, c * LANES:(c + 1) * LANES]
    prev = None
    for g, (_, dil) in enumerate(BRANCHES):
        n = tm // dil
        if dil == 1:
            lhs[g][...] = h.astype(BF16)
        else:
            src, p = prev if prev is not None and dil % prev[1] == 0 else (hsl, 1)
            s = dil // p
            keep = any(d2 > dil and d2 % dil == 0 for (_, d2) in BRANCHES[g + 1:])
            for rl in range(p):
                for rh in range(s):
                    r = p * rh + rl
                    for c in range(N_SLABS):
                        v = src[c, pl.ds(rl * (tm // p) + rh, n, stride=s), :]
                        if keep:
                            stage[c, r * n:(r + 1) * n, :] = v
                        lhs[g][r * n:(r + 1) * n, c * LANES:(c + 1) * LANES] = v.astype(BF16)
            if keep:
                prev = (stage, dil)
        res = jnp.dot(lhs[g][...], w_ref[:, g * nc:(g + 1) * nc], preferred_element_type=F32)
        if scale != 1.0:
            res = res * scale
        for r in range(dil):
            o_refs[g][0, r] = res[r * n:(r + 1) * n].astype(o_refs[g].dtype)

        if tails:
            tail_ref, rows = tail_refs[g], tails[g]
            if rows < tm:
                @pl.when(i == nt - 1)
                def _(tail_ref=tail_ref, rows=rows, res=res):
                    tail_ref[0] = res[tm - rows:tm]
            else:
                @pl.when(i >= nt - rows // tm)
                def _(tail_ref=tail_ref, res=res, dil=dil, n=n):
                    if dil == 1:
                        tail_ref[0] = res
                    else:
                        p = max([d2 for (_, d2) in BRANCHES if 1 < d2 < dil and dil % d2 == 0],
                                default=1)
                        s = dil // p
                        for c in range(nc // LANES):
                            cs = slice(c * LANES, (c + 1) * LANES)
                            if p == 1:
                                for r in range(dil):
                                    tsl[c, pl.ds(r, n, stride=dil), :] = res[r * n:(r + 1) * n, cs]
                            else:
                                for rl in range(p):
                                    for rh in range(s):
                                        r = p * rh + rl
                                        tsl2[c, pl.ds(rl * (tm // p) + rh, n, stride=s), :] = (
                                            res[r * n:(r + 1) * n, cs])
                                for rl in range(p):
                                    tsl[c, pl.ds(rl, tm // p, stride=p), :] = (
                                        tsl2[c, rl * (tm // p):(rl + 1) * (tm // p), :])
                        for c in range(nc // LANES):
                            tail_ref[0, :, c * LANES:(c + 1) * LANES] = tsl[c]


def _proj(x, gains, gi, w, wi, nc, tm, scale=1.0, tails=()):
    B, T, D = x.shape
    nt = T // tm
    out_specs, out_shape = [], []
    for (_, dil) in BRANCHES:
        out_specs.append(pl.BlockSpec((1, dil, tm // dil, nc), lambda b, i: (b, 0, i, 0)))
        out_shape.append(jax.ShapeDtypeStruct((B, dil, T // dil, nc), BF16))
    scratch = [pltpu.VMEM((N_SLABS, tm, LANES), F32), pltpu.VMEM((N_SLABS, tm, LANES), F32)]
    if tails:
        for rows, (_, dil) in zip(tails, BRANCHES):
            assert rows >= tm or dil == 1
            if rows < tm:
                out_specs.append(pl.BlockSpec((1, rows, nc), lambda b, i: (b, 0, 0)))
            else:
                first = nt - rows // tm
                out_specs.append(pl.BlockSpec(
                    (1, tm, nc), lambda b, i, first=first: (b, jnp.maximum(i - first, 0), 0)))
            out_shape.append(jax.ShapeDtypeStruct((B, rows, nc), F32))
        scratch += [pltpu.VMEM((nc // LANES, tm, LANES), F32) for _ in range(2)]
    scratch += [pltpu.VMEM((tm, D), BF16) for _ in BRANCHES]
    return pl.pallas_call(
        functools.partial(_proj_kernel, tm=tm, nc=nc, scale=scale, tails=tuple(tails)),
        grid=(B, nt),
        in_specs=[
            pl.BlockSpec((1, tm, D), lambda b, i: (b, i, 0)),
            pl.BlockSpec((None, 1, D), lambda b, i: (gi, 0, 0)),
            pl.BlockSpec((None, D, len(BRANCHES) * nc), lambda b, i: (wi, 0, 0),
                         pipeline_mode=pl.Buffered(1)),
        ],
        out_specs=out_specs,
        out_shape=out_shape,
        scratch_shapes=scratch,
        compiler_params=_cparams("arbitrary", "arbitrary"),
        name="proj_kv" if tails else "proj_q",
    )(x, gains, w)


def _proj_sample_kernel(x_ref, g_ref, w_ref, o_ref, *, scale):
    h = _rms(x_ref[...], g_ref[...]).astype(BF16)
    res = jnp.dot(h, w_ref[...], preferred_element_type=F32)
    if scale != 1.0:
        res = res * scale
    o_ref[...] = res


def _proj_sample(x, gains, gi, w, wi, scale=1.0, tn=3072):
    Bs, D = x.shape
    nc = w.shape[2]
    return pl.pallas_call(
        functools.partial(_proj_sample_kernel, scale=scale),
        grid=(nc // tn,),
        in_specs=[
            pl.BlockSpec((Bs, D), lambda j: (0, 0)),
            pl.BlockSpec((None, 1, D), lambda j: (gi, 0, 0)),
            pl.BlockSpec((None, D, tn), lambda j: (wi, 0, j)),
        ],
        out_specs=pl.BlockSpec((Bs, tn), lambda j: (0, j)),
        out_shape=jax.ShapeDtypeStruct((Bs, nc), F32),
        compiler_params=_cparams("arbitrary"),
        name="proj_sample",
    )(x, gains, w)


def _attn_kernel(q_ref, kv_ref, bias_ref, o_ref, ml_ref, kprev, vprev, *, tq, rb):
    first = pl.program_id(2) == 0
    lane = lax.broadcasted_iota(jnp.int32, (Q_SUB, LANES), 1)
    ones = jnp.ones((2 * Q_SUB, HEAD_DIM), BF16)
    if rb == 1:
        @pl.when(first)
        def _():
            kprev[...] = jnp.zeros(kprev.shape, BF16)
            vprev[...] = jnp.zeros(vprev.shape, BF16)
    for rr in range(rb):
        for s in range(tq // Q_SUB):
            r0 = s * Q_SUB

            def keys(h, col0, s=s, r0=r0, rr=rr):
                cs = slice(col0 + h * HEAD_DIM, col0 + (h + 1) * HEAD_DIM)
                if s > 0:
                    return kv_ref[0, rr, r0 - Q_SUB:r0 + Q_SUB, cs]
                cur = kv_ref[0, rr, 0:Q_SUB, cs]
                if rb == 1:
                    prev = (kprev if col0 == 0 else vprev)[:, h * HEAD_DIM:(h + 1) * HEAD_DIM]
                else:
                    prev = jnp.zeros((Q_SUB, HEAD_DIM), BF16)
                return jnp.concatenate([prev, cur], axis=0)

            if s > 0:
                boff = 0
            elif rb > 1:
                boff = HEADS
            else:
                boff = jnp.where(first, HEADS, 0)
            ml_tile = jnp.zeros((Q_SUB, LANES), F32)
            for h in range(HEADS):
                cs = slice(h * HEAD_DIM, (h + 1) * HEAD_DIM)
                q = q_ref[0, rr, r0:r0 + Q_SUB, cs]
                sc = lax.dot_general(q, keys(h, 0), (((1,), (1,)), ((), ())),
                                     preferred_element_type=F32)
                sc = sc + bias_ref[boff + h]
                m = jnp.max(sc, axis=-1, keepdims=True)
                p = jnp.exp2(sc - m).astype(BF16)
                v_ext = jnp.concatenate([keys(h, D_MODEL), ones], axis=1)
                o = jnp.dot(p, v_ext, preferred_element_type=F32)
                o_ref[0, rr, r0:r0 + Q_SUB, cs] = o[:, :HEAD_DIM].astype(o_ref.dtype)
                ml_tile = jnp.where(lane == h, m, ml_tile)
                ml_tile = jnp.where(lane == HEADS + h, o[:, HEAD_DIM:], ml_tile)
            ml_ref[0, rr, r0:r0 + Q_SUB, :] = ml_tile
    if rb == 1:
        kprev[...] = kv_ref[0, 0, tq - Q_SUB:tq, 0:D_MODEL]
        vprev[...] = kv_ref[0, 0, tq - Q_SUB:tq, D_MODEL:2 * D_MODEL]


def _attn_branch(q, kv, bias, tq, rb):
    B, dil, L, D = q.shape
    assert rb == 1 or tq == L
    return pl.pallas_call(
        functools.partial(_attn_kernel, tq=tq, rb=rb),
        grid=(B, dil // rb, L // tq),
        in_specs=[
            pl.BlockSpec((1, rb, tq, D), lambda b, r, i: (b, r, i, 0)),
            pl.BlockSpec((1, rb, tq, 2 * D), lambda b, r, i: (b, r, i, 0)),
            pl.BlockSpec(bias.shape, lambda b, r, i: (0, 0, 0)),
        ],
        out_specs=[
            pl.BlockSpec((1, rb, tq, D), lambda b, r, i: (b, r, i, 0)),
            pl.BlockSpec((1, rb, tq, LANES), lambda b, r, i: (b, r, i, 0)),
        ],
        out_shape=[
            jax.ShapeDtypeStruct((B, dil, L, D), BF16),
            jax.ShapeDtypeStruct((B, dil, L, LANES), F32),
        ],
        scratch_shapes=[pltpu.VMEM((Q_SUB, D), BF16), pltpu.VMEM((Q_SUB, D), BF16)],
        compiler_params=_cparams("arbitrary", "arbitrary", "arbitrary"),
        name=f"attn_d{dil}",
    )(q, kv, bias)


def _merge_pieces(x_ref, o_refs, ml_refs, wo_ref, osls, wsl, lhs, x1_ref, tm):
    def weights():
        tiles = []
        for g, (_, dil) in enumerate(BRANCHES):
            if dil == 1:
                tiles.append(ml_refs[g][0, 0])
            else:
                n = tm // dil
                for r in range(dil):
                    wsl[g, pl.ds(r, n, stride=dil), :] = ml_refs[g][0, r]
                tiles.append(wsl[g])
        m = jnp.maximum(jnp.maximum(tiles[0], tiles[1]), tiles[2])
        us = [jnp.exp2(tl - m) for tl in tiles]
        ls = [pltpu.roll(tl, LANES - HEADS, axis=1) for tl in tiles]
        den = us[0] * ls[0] + us[1] * ls[1] + us[2] * ls[2]
        for g in range(len(BRANCHES)):
            wsl[g] = us[g] / den

    def interleave(g, r_lo, r_hi):
        dil = BRANCHES[g][1]
        n = tm // dil
        for r in range(r_lo, r_hi):
            for c in range(N_SLABS):
                osls[g][c, pl.ds(r, n, stride=dil), :] = (
                    o_refs[g][0, r, :, c * LANES:(c + 1) * LANES].astype(F32))

    def combine(c_lo, c_hi):
        for c in range(c_lo, c_hi):
            cs = slice(c * LANES, (c + 1) * LANES)
            merged = wsl[0, :, c:c + 1] * o_refs[0][0, 0, :, cs].astype(F32)
            for g in range(1, len(BRANCHES)):
                merged = merged + wsl[g, :, c:c + 1] * osls[g][c]
            lhs[:, cs] = merged.astype(BF16)

    def project():
        x1_ref[...] = x_ref[0] + jnp.dot(lhs[...], wo_ref[...], preferred_element_type=F32)

    pieces = [weights]
    for g, (_, dil) in enumerate(BRANCHES):
        if dil > 1:
            half = max(dil // 2, 1) if dil >= 8 else dil
            pieces += [functools.partial(interleave, g, lo, min(lo + half, dil))
                       for lo in range(0, dil, half)]
    pieces += [functools.partial(combine, 0, N_SLABS // 2),
               functools.partial(combine, N_SLABS // 2, N_SLABS), project]
    return pieces


def _merge_mlp_kernel(x_ref, xs1_ref, o0_ref, o1_ref, o2_ref, ml0_ref, ml1_ref, ml2_ref, wo_ref,
                      gmlp_ref, w1_ref, w2_ref, gf_ref, *refs, tm, tc, final_norm, ride):
    f32_refs, refs = refs[:ride], refs[ride:]
    out_ref, os_ref = refs[:2]
    bf16_refs, refs = refs[2:2 + ride], refs[2 + ride:]
    x1a, x1b, hsa, hsb, us, osl1, osl2, wsl, lhs = refs
    cast = _cast_rider(f32_refs, bf16_refs)

    def run(x1_w, hs_w, x1_r, hs_r):
        def next_norm():
            hs_w[...] = _rms(x1_w[...], gmlp_ref[...]).astype(BF16)
            cast()

        pieces = _merge_pieces(x_ref, (o0_ref, o1_ref, o2_ref), (ml0_ref, ml1_ref, ml2_ref),
                               wo_ref, (None, osl1, osl2), wsl, lhs, x1_w, tm)
        _mlp_tile(x1_r, hs_r, w1_ref, w2_ref, gf_ref, out_ref, us, tc, final_norm,
                  between=pieces + [next_norm])

    _run_pipelined(((x1a, hsa), (x1b, hsb)), xs1_ref, gmlp_ref, out_ref, os_ref, run)


def _merge_mlp(x, xs1, os_, mls, wo, wi, gmlp, w1, w2, gain_final, li, final_norm, riders,
               tm=512, tc=512):
    B, T, D = x.shape
    Bs = xs1.shape[0]
    nt = T // tm
    n_tiles = B * nt

    def tile(j):
        jj = jnp.minimum(j, n_tiles - 1)
        return jj // nt, jj % nt

    one = pl.Buffered(1)
    in_specs = [pl.BlockSpec((1, tm, D), lambda j: (*tile(j), 0)),
                pl.BlockSpec((Bs, D), lambda j: (0, 0))]
    for (_, dil) in BRANCHES:
        in_specs.append(pl.BlockSpec((1, dil, tm // dil, D), lambda j: (tile(j)[0], 0, tile(j)[1], 0)))
    for (_, dil) in BRANCHES:
        in_specs.append(pl.BlockSpec((1, dil, tm // dil, LANES),
                                     lambda j: (tile(j)[0], 0, tile(j)[1], 0)))
    in_specs += [
        pl.BlockSpec((None, D, D), lambda j: (wi, 0, 0), pipeline_mode=one),
        pl.BlockSpec((None, 1, D), lambda j: (li, 0, 0)),
        pl.BlockSpec((D, D_FF), lambda j: (0, 0), pipeline_mode=one),
        pl.BlockSpec((D_FF, D), lambda j: (0, 0), pipeline_mode=one),
        pl.BlockSpec((1, D), lambda j: (0, 0)),
    ]
    ride_args, ride_in, ride_out, ride_shape = _rider_specs(riders, n_tiles)
    res = pl.pallas_call(
        functools.partial(_merge_mlp_kernel, tm=tm, tc=tc, final_norm=final_norm,
                          ride=len(riders)),
        grid=(n_tiles + 1,),
        in_specs=in_specs + ride_in,
        out_specs=[pl.BlockSpec((tm, D), lambda j: (jnp.maximum(j - 1, 0), 0)),
                   pl.BlockSpec((Bs, D), lambda j: (0, 0))] + ride_out,
        out_shape=[jax.ShapeDtypeStruct((B * T, D), F32),
                   jax.ShapeDtypeStruct((Bs, D), F32)] + ride_shape,
        scratch_shapes=[
            pltpu.VMEM((tm, D), F32), pltpu.VMEM((tm, D), F32),
            pltpu.VMEM((tm, D), BF16), pltpu.VMEM((tm, D), BF16),
            pltpu.VMEM((tm, D_FF), BF16),
            pltpu.VMEM((N_SLABS, tm, LANES), F32), pltpu.VMEM((N_SLABS, tm, LANES), F32),
            pltpu.VMEM((len(BRANCHES), tm, LANES), F32),
            pltpu.VMEM((tm, D), BF16),
        ],
        compiler_params=_cparams("arbitrary"),
        name="merge_mlp",
    )(x, xs1, *os_, *mls, wo, gmlp, w1, w2, gain_final, *ride_args)
    return (res[0].reshape(B, T, D),) + tuple(res[1:])


def _attn_sample_kernel(q_ref, kvn_ref, c0_ref, c1_ref, c2_ref, bias_ref, o_ref, *, nb):
    caches = (c0_ref, c1_ref, c2_ref)
    for i in range(nb):
        sc_c, sc_n = [], []
        for g in range(len(BRANCHES)):
            q = q_ref[i, g]
            kc = caches[g][i, :, 0]
            sc_c.append(jnp.sum(kc * q[None], axis=-1, keepdims=True) + bias_ref[g, 0:N_KEYS - 1])
            sc_n.append(jnp.sum(kvn_ref[i, g, 0] * q, axis=-1, keepdims=True)
                        + bias_ref[g, N_KEYS - 1])
        m = sc_n[0]
        for g in range(len(BRANCHES)):
            m = jnp.maximum(m, jnp.maximum(jnp.max(sc_c[g], axis=0), sc_n[g]))
        num = jnp.zeros((HEADS, HEAD_DIM), F32)
        den = jnp.zeros((HEADS, HEAD_DIM), F32)
        for g in range(len(BRANCHES)):
            p_c = jnp.exp2(sc_c[g] - m[None])
            p_n = jnp.exp2(sc_n[g] - m)
            den = den + jnp.sum(p_c, axis=0) + p_n
            num = num + jnp.sum(p_c * caches[g][i, :, 1], axis=0) + p_n * kvn_ref[i, g, 1]
        o_ref[i] = num / den


def _attn_sample(q, kvn, caches, bias, nb=4):
    Bs = q.shape[0]
    in_specs = [
        pl.BlockSpec((nb,) + q.shape[1:], lambda b: (b, 0, 0, 0)),
        pl.BlockSpec((nb,) + kvn.shape[1:], lambda b: (b, 0, 0, 0, 0)),
    ]
    for c in caches:
        in_specs.append(pl.BlockSpec((nb, N_KEYS - 1, None, 2, HEADS, HEAD_DIM),
                                     lambda b: (b, 0, 0, 0, 0, 0)))
    in_specs.append(pl.BlockSpec(bias.shape, lambda b: (0, 0, 0, 0)))
    return pl.pallas_call(
        functools.partial(_attn_sample_kernel, nb=nb),
        grid=(Bs // nb,),
        in_specs=in_specs,
        out_specs=pl.BlockSpec((nb, HEADS, HEAD_DIM), lambda b: (b, 0, 0)),
        out_shape=jax.ShapeDtypeStruct((Bs, HEADS, HEAD_DIM), F32),
        compiler_params=_cparams("arbitrary"),
        name="attn_sample",
    )(q, kvn, *caches, bias)


def _out_sample_kernel(x_ref, o_ref, wo_ref, out_ref):
    out_ref[...] = x_ref[...] + jnp.dot(o_ref[...].astype(BF16), wo_ref[...],
                                        preferred_element_type=F32)


def _out_sample(x, o, wo):
    return pl.pallas_call(
        _out_sample_kernel,
        out_shape=jax.ShapeDtypeStruct(x.shape, F32),
        compiler_params=pltpu.CompilerParams(vmem_limit_bytes=VMEM_LIMIT),
        name="out_sample",
    )(x, o, wo)


def _t5_bucket(dist):
    max_exact = NUM_BUCKETS // 2
    df = jnp.maximum(dist, 1).astype(F32)
    large = max_exact + (jnp.log(df / max_exact) / math.log(MAX_DISTANCE / max_exact)
                         * (NUM_BUCKETS - max_exact)).astype(jnp.int32)
    large = jnp.minimum(large, NUM_BUCKETS - 1)
    return jnp.where(dist < max_exact, dist, large)


def _bias_tables(rel_bias):
    out = []
    for g, (w, d) in enumerate(BRANCHES):
        dist = jnp.arange(N_KEYS, dtype=jnp.int32) * d
        out.append(rel_bias[_t5_bucket(dist)][:, g * HEADS:(g + 1) * HEADS])
    return out


def _block_bias(tab):
    pad_lo = jnp.full((HEADS, Q_SUB - 1), MASKED, F32)
    pad_hi = jnp.full((HEADS, Q_SUB), MASKED, F32)
    f = jnp.concatenate([pad_lo, tab[::-1].T, pad_hi], axis=1)
    width = f.shape[1]
    flat = jnp.tile(f, (1, Q_SUB))[:, :Q_SUB * (width - 1)]
    toep = flat.reshape(HEADS, Q_SUB, width - 1)
    return toep[:, :, Q_SUB - 1:Q_SUB - 1 + 2 * Q_SUB]


def kernel(x_prompt, x_sample, state_pool, cache_kv_w128, cache_kv_w512, cache_kv_w2048,
           norm_mix, pool_w, pool_scale, norm_mlp, mlp_in, mlp_out, norm_kv, w_kv, w_q, w_o,
           rel_bias, norm_final):
    B, T, D = x_prompt.shape
    Bs = x_sample.shape[0]
    n_a = pool_w.shape[0]
    depth = norm_mix.shape[0]
    n_br = len(BRANCHES)
    qk_scale = HEAD_DIM ** -0.5

    pool_w_b = pool_w.astype(BF16)
    w1_b, w2_b = mlp_in[0].astype(BF16), mlp_out[0].astype(BF16)
    ride_proj = n_a >= 2
    extra = {0: [w_kv[None]], 1: [w_q.reshape(1, -1, w_q.shape[-1]), w_o.reshape(1, -1, D)]}
    if not ride_proj:
        w_kv_b, w_q_b, w_o_b = w_kv.astype(BF16)[None], w_q.astype(BF16), w_o.astype(BF16)
    row = lambda v: v.reshape(1, D)
    g_mix = norm_mix.reshape(depth, 1, D)
    g_mlp = norm_mlp.reshape(depth, 1, D)
    g_kv = norm_kv.reshape(1, 1, D)
    g_scale = pool_scale.reshape(n_a, 1, D)

    tabs = _bias_tables(rel_bias)
    no_prev = np.arange(2 * Q_SUB)[None, None, :] < Q_SUB
    blk_bias = []
    for t in tabs:
        bb = _block_bias(t * LOG2E)
        blk_bias.append(jnp.concatenate([bb, jnp.where(no_prev, MASKED, bb)], axis=0))
    smp_bias = jnp.stack([jnp.broadcast_to((t * LOG2E)[::-1][:, :, None], (N_KEYS, HEADS, HEAD_DIM))
                          for t in tabs])

    caches = [c.reshape(Bs, N_KEYS - 1, d, 2, HEADS, HEAD_DIM)
              for c, (w, d) in zip((cache_kv_w128, cache_kv_w512, cache_kv_w2048), BRANCHES)]

    xp = x_prompt
    xs = x_sample.reshape(Bs, D)
    pool_p, pool_s, kv_p = [], [], []
    kvs = kvn = None
    for l in range(depth):
        last = l == depth - 1
        riders = [] if last else [(mlp_in, l + 1), (mlp_out, l + 1)]
        if l < n_a:
            if ride_proj and l in extra:
                riders += [(w, 0) for w in extra[l]]
            xs1, hs = _pool_sample(xs, state_pool[l], row(norm_mix[l]), pool_w_b[l], row(pool_scale[l]))
            pool_s.append(jnp.concatenate([state_pool[l][:, 1:], hs[:, None, :]], axis=1))
            res = _pool_mlp(xp, xs1, g_mix, l, pool_w_b, g_scale, g_mlp, w1_b, w2_b,
                            row(norm_final), riders)
            xp, xs, nbuf = res[:3]
            pool_p.append(nbuf)
            cast = res[3:]
            if ride_proj and l == 0:
                w_kv_b = cast[2][None]
            if ride_proj and l == 1:
                w_q_b, w_o_b = cast[2].reshape(w_q.shape), cast[3].reshape(w_o.shape)
        else:
            lb = l - n_a
            if l == n_a:
                tails = tuple(min(w, T) for (w, _) in BRANCHES)
                res = _proj(xp, g_kv, 0, w_kv_b, 0, 2 * D, 256, tails=tails)
                kvs = res[:n_br]
                kv_p = [t.reshape(B, rows, 2, HEADS, HEAD_DIM) for t, rows in zip(res[n_br:], tails)]
                kvn = _proj_sample(xs, g_kv, 0, w_kv_b, 0)
            qs_p = _proj(xp, g_mix, l, w_q_b, lb, D, 1024, scale=qk_scale * LOG2E)
            qs = _proj_sample(xs, g_mix, l, w_q_b, lb, scale=qk_scale * LOG2E)
            os_, lses = [], []
            for g, (w, d) in enumerate(BRANCHES):
                L = T // d
                tq, rb = (L, min(d, ATTN_ROWS // L)) if L < ATTN_ROWS else (ATTN_ROWS, 1)
                o_g, ml_g = _attn_branch(qs_p[g], kvs[g], blk_bias[g], tq, rb)
                os_.append(o_g)
                lses.append(ml_g)
            o_s = _attn_sample(qs.reshape(Bs, n_br, HEADS, HEAD_DIM),
                               kvn.reshape(Bs, n_br, 2, HEADS, HEAD_DIM), caches, smp_bias)
            xs1 = _out_sample(xs, o_s.reshape(Bs, D), w_o_b[lb])
            res = _merge_mlp(xp, xs1, os_, lses, w_o_b, lb, g_mlp, w1_b, w2_b, row(norm_final),
                             l, last, riders)
            xp, xs = res[:2]
            cast = res[2:]
        if not last:
            w1_b, w2_b = cast[:2]

    kv_s = [kvn[:, g * 2 * D:(g + 1) * 2 * D].reshape(Bs, 1, 2, HEADS, HEAD_DIM) for g in range(n_br)]
    return (xp, xs.reshape(Bs, 1, D), jnp.stack(pool_p), jnp.stack(pool_s),
            kv_p[0], kv_s[0], kv_p[1], kv_s[1], kv_p[2], kv_s[2])
```

```python
import functools
import math

import jax
import jax.numpy as jnp
import numpy as np
from jax import lax
from jax.experimental import pallas as pl
from jax.experimental.pallas import tpu as pltpu

F32 = jnp.float32
BF16 = jnp.bfloat16

D_MODEL = 1024
HEADS = 8
HEAD_DIM = 128
D_FF = 4 * D_MODEL
POOL_WINDOWS = (2, 4, 8, 16)
POOL_GROUP = D_MODEL // len(POOL_WINDOWS)
POOL_STATE = max(POOL_WINDOWS) - 1
BRANCHES = ((128, 1), (512, 4), (2048, 16))
N_KEYS = 129
NUM_BUCKETS = 32
MAX_DISTANCE = 2048
SAMPLE_POS0 = 8192
EPS = 1e-6
MASKED = -1e30
LOG2E = math.log2(math.e)

LANES = 128
N_SLABS = D_MODEL // LANES
Q_SUB = 128
ATTN_ROWS = 2048
VMEM_LIMIT = 60 * 1024 * 1024


def _cparams(*sem):
    return pltpu.CompilerParams(dimension_semantics=sem, vmem_limit_bytes=VMEM_LIMIT)


def _rms(x, g):
    ms = jnp.mean(x * x, axis=-1, keepdims=True)
    return x * lax.rsqrt(ms + EPS) * g


POOL_BASE = 24
POOL_LO = 8


def _pool_pieces(x_ref, g_ref, w_ref, sc_ref, nb_ref, hbuf, sbuf, x1_ref, t, tm):
    end = POOL_BASE + tm

    def norm():
        hbuf[POOL_BASE:end, :] = _rms(x_ref[0], g_ref[...])

    def group(gi, w):
        sl = slice(gi * POOL_GROUP, (gi + 1) * POOL_GROUP)
        n_stage = w.bit_length() - 1
        s = None
        for si in range(n_stage):
            sh = 1 << si
            lo = POOL_BASE if si == n_stage - 1 else POOL_LO
            if si == 0:
                val = hbuf[lo:end, sl] + hbuf[lo - sh:end - sh, sl]
            else:
                val = sbuf[si - 1, lo:end, :] + sbuf[si - 1, lo - sh:end - sh, :]
            if si == n_stage - 1:
                s = val
            else:
                sbuf[si, lo:end, :] = val
        pos = t * tm + lax.broadcasted_iota(jnp.int32, (tm, 1), 0)
        cnt = jnp.minimum(pos + 1, w).astype(F32)
        pooled = s / cnt - hbuf[POOL_BASE:end, sl]
        y = jnp.dot(pooled.astype(BF16), w_ref[gi], preferred_element_type=F32)
        x1_ref[:, sl] = x_ref[0, :, sl] + y * sc_ref[:, sl]

    def carry():
        nb_ref[0] = hbuf[end - POOL_STATE:end, :]
        hbuf[POOL_BASE - 16:POOL_BASE, :] = hbuf[end - 16:end, :]

    pieces = [norm]
    pieces += [functools.partial(group, gi, w) for gi, w in enumerate(POOL_WINDOWS)]
    return pieces + [carry]


def _mlp_tile(x1_ref, hs_ref, w1_ref, w2_ref, gf_ref, o_ref, us, tc, final_norm, between=()):
    between = list(between)
    n_chunks = D_FF // tc
    assert len(between) <= n_chunks
    for c in range(n_chunks):
        if c < len(between):
            between[c]()
        u = jnp.dot(hs_ref[...], w1_ref[:, c * tc:(c + 1) * tc], preferred_element_type=F32)
        us[:, c * tc:(c + 1) * tc] = jnp.square(jnp.maximum(u, 0.0)).astype(BF16)
    y = x1_ref[...] + jnp.dot(us[...], w2_ref[...], preferred_element_type=F32)
    if final_norm:
        y = _rms(y, gf_ref[...])
    o_ref[...] = y


def _run_pipelined(slots, xs1_ref, gmlp_ref, o_ref, os_ref, run):
    (x1a, hsa), (x1b, hsb) = slots
    j = pl.program_id(0)
    bs = xs1_ref.shape[0]

    @pl.when(j == 0)
    def _():
        x1b[...] = jnp.zeros(x1b.shape, F32)
        hsb[...] = jnp.zeros(hsb.shape, BF16)
        xs1 = xs1_ref[...]
        x1b[0:bs, :] = xs1
        hsb[0:bs, :] = _rms(xs1, gmlp_ref[...]).astype(BF16)

    @pl.when(j % 2 == 0)
    def _():
        run(x1a, hsa, x1b, hsb)

    @pl.when(j % 2 == 1)
    def _():
        run(x1b, hsb, x1a, hsa)

    @pl.when(j == 0)
    def _():
        os_ref[...] = o_ref[0:bs, :]


def _cast_rider(f32_refs, bf16_refs):
    def cast():
        for src, dst in zip(f32_refs, bf16_refs):
            dst[...] = src[...].astype(BF16)
    return cast


def _rider_specs(riders, n_tiles):
    chunk = lambda j: jnp.minimum(j, n_tiles - 1)
    args, in_specs, out_specs, out_shape = [], [], [], []
    for w, ln in riders:
        rows = w.shape[1] // n_tiles
        args.append(w)
        in_specs.append(pl.BlockSpec((None, rows, w.shape[2]), lambda j, ln=ln: (ln, chunk(j), 0)))
        out_specs.append(pl.BlockSpec((rows, w.shape[2]), lambda j: (chunk(j), 0)))
        out_shape.append(jax.ShapeDtypeStruct(w.shape[1:], BF16))
    return args, in_specs, out_specs, out_shape


def _pool_mlp_kernel(x_ref, xs1_ref, gmix_ref, wp_ref, sc_ref, gmlp_ref, w1_ref, w2_ref, gf_ref,
                     *refs, tm, nt, n_tiles, tc, ride):
    f32_refs, refs = refs[:ride], refs[ride:]
    o_ref, os_ref, nb_ref = refs[:3]
    bf16_refs, refs = refs[3:3 + ride], refs[3 + ride:]
    x1a, x1b, hsa, hsb, us, hbuf, sbuf = refs
    cast = _cast_rider(f32_refs, bf16_refs)
    j = pl.program_id(0)
    t = jnp.minimum(j, n_tiles - 1) % nt

    @pl.when(j == 0)
    def _():
        hbuf[0:POOL_BASE, :] = jnp.zeros((POOL_BASE, D_MODEL), F32)
        sbuf[...] = jnp.zeros(sbuf.shape, F32)

    @pl.when(t == 0)
    def _():
        hbuf[POOL_BASE - 16:POOL_BASE, :] = jnp.zeros((16, D_MODEL), F32)

    def run(x1_w, hs_w, x1_r, hs_r):
        def next_norm():
            hs_w[...] = _rms(x1_w[...], gmlp_ref[...]).astype(BF16)

        pieces = _pool_pieces(x_ref, gmix_ref, wp_ref, sc_ref, nb_ref, hbuf, sbuf, x1_w, t, tm)
        _mlp_tile(x1_r, hs_r, w1_ref, w2_ref, gf_ref, o_ref, us, tc, False,
                  between=pieces + [next_norm, cast])

    _run_pipelined(((x1a, hsa), (x1b, hsb)), xs1_ref, gmlp_ref, o_ref, os_ref, run)


def _pool_mlp(x, xs1, gmix, li, w_pool, scale, gmlp, w1, w2, gain_final, riders, tm=512, tc=512):
    B, T, D = x.shape
    Bs = xs1.shape[0]
    nt = T // tm
    n_tiles = B * nt

    def tile(j):
        jj = jnp.minimum(j, n_tiles - 1)
        return jj // nt, jj % nt

    one = pl.Buffered(1)
    ride_args, ride_in, ride_out, ride_shape = _rider_specs(riders, n_tiles)
    res = pl.pallas_call(
        functools.partial(_pool_mlp_kernel, tm=tm, nt=nt, n_tiles=n_tiles, tc=tc,
                          ride=len(riders)),
        grid=(n_tiles + 1,),
        in_specs=[
            pl.BlockSpec((1, tm, D), lambda j: (*tile(j), 0)),
            pl.BlockSpec((Bs, D), lambda j: (0, 0)),
            pl.BlockSpec((None, 1, D), lambda j: (li, 0, 0)),
            pl.BlockSpec((None,) + w_pool.shape[1:], lambda j: (li, 0, 0, 0)),
            pl.BlockSpec((None, 1, D), lambda j: (li, 0, 0)),
            pl.BlockSpec((None, 1, D), lambda j: (li, 0, 0)),
            pl.BlockSpec((D, D_FF), lambda j: (0, 0), pipeline_mode=one),
            pl.BlockSpec((D_FF, D), lambda j: (0, 0), pipeline_mode=one),
            pl.BlockSpec((1, D), lambda j: (0, 0)),
        ] + ride_in,
        out_specs=[
            pl.BlockSpec((tm, D), lambda j: (jnp.maximum(j - 1, 0), 0)),
            pl.BlockSpec((Bs, D), lambda j: (0, 0)),
            pl.BlockSpec((1, POOL_STATE, D), lambda j: (tile(j)[0], 0, 0)),
        ] + ride_out,
        out_shape=[
            jax.ShapeDtypeStruct((B * T, D), F32),
            jax.ShapeDtypeStruct((Bs, D), F32),
            jax.ShapeDtypeStruct((B, POOL_STATE, D), F32),
        ] + ride_shape,
        scratch_shapes=[
            pltpu.VMEM((tm, D), F32), pltpu.VMEM((tm, D), F32),
            pltpu.VMEM((tm, D), BF16), pltpu.VMEM((tm, D), BF16),
            pltpu.VMEM((tm, D_FF), BF16),
            pltpu.VMEM((POOL_BASE + tm, D), F32),
            pltpu.VMEM((len(POOL_WINDOWS) - 1, POOL_BASE + tm, POOL_GROUP), F32),
        ],
        compiler_params=_cparams("arbitrary"),
        name="pool_mlp",
    )(x, xs1, gmix, w_pool, scale, gmlp, w1, w2, gain_final, *ride_args)
    return (res[0].reshape(B, T, D),) + tuple(res[1:])


def _pool_sample_kernel(x_ref, st_ref, g_ref, w_ref, sc_ref, o_ref, h_ref):
    x = x_ref[...]
    h = _rms(x, g_ref[...])
    h_ref[...] = h
    for gi, w in enumerate(POOL_WINDOWS):
        sl = slice(gi * POOL_GROUP, (gi + 1) * POOL_GROUP)
        hg = h[:, sl]
        s = hg
        for j in range(1, w):
            s = s + st_ref[:, POOL_STATE - j, sl]
        pooled = s / float(min(SAMPLE_POS0 + 1, w)) - hg
        y = jnp.dot(pooled.astype(BF16), w_ref[gi], preferred_element_type=F32)
        o_ref[:, sl] = x[:, sl] + y * sc_ref[:, sl]


def _pool_sample(x, state, gain, w_pool, scale):
    Bs, D = x.shape
    return pl.pallas_call(
        _pool_sample_kernel,
        out_shape=[jax.ShapeDtypeStruct((Bs, D), F32), jax.ShapeDtypeStruct((Bs, D), F32)],
        compiler_params=pltpu.CompilerParams(vmem_limit_bytes=VMEM_LIMIT),
        name="pool_sample",
    )(x, state, gain, w_pool, scale)


def _proj_kernel(x_ref, g_ref, w_ref, *refs, tm, nc, scale, tails):
    n_br = len(BRANCHES)
    o_refs = refs[:n_br]
    if tails:
        tail_refs = refs[n_br:2 * n_br]
        hsl, stage, tsl, tsl2 = refs[2 * n_br:2 * n_br + 4]
        lhs = refs[2 * n_br + 4:]
        i = pl.program_id(1)
        nt = pl.num_programs(1)
    else:
        hsl, stage = refs[n_br:n_br + 2]
        lhs = refs[n_br + 2:]
    h = _rms(x_ref[0], g_ref[...])
    for c in range(N_SLABS):
        hsl[c] = h[:, c * LANES:(c + 1) * LANES]
    prev = None
    for g, (_, dil) in enumerate(BRANCHES):
        n = tm // dil
        if dil == 1:
            lhs[g][...] = h.astype(BF16)
        else:
            src, p = prev if prev is not None and dil % prev[1] == 0 else (hsl, 1)
            s = dil // p
            keep = any(d2 > dil and d2 % dil == 0 for (_, d2) in BRANCHES[g + 1:])
            for rl in range(p):
                for rh in range(s):
                    r = p * rh + rl
                    for c in range(N_SLABS):
                        v = src[c, pl.ds(rl * (tm // p) + rh, n, stride=s), :]
                        if keep:
                            stage[c, r * n:(r + 1) * n, :] = v
                        lhs[g][r * n:(r + 1) * n, c * LANES:(c + 1) * LANES] = v.astype(BF16)
            if keep:
                prev = (stage, dil)
        res = jnp.dot(lhs[g][...], w_ref[:, g * nc:(g + 1) * nc], preferred_element_type=F32)
        if scale != 1.0:
            res = res * scale
        for r in range(dil):
            o_refs[g][0, r] = res[r * n:(r + 1) * n].astype(o_refs[g].dtype)

        if tails:
            tail_ref, rows = tail_refs[g], tails[g]
            if rows < tm:
                @pl.when(i == nt - 1)
                def _(tail_ref=tail_ref, rows=rows, res=res):
                    tail_ref[0] = res[tm - rows:tm]
            else:
                @pl.when(i >= nt - rows // tm)
                def _(tail_ref=tail_ref, res=res, dil=dil, n=n):
                    if dil == 1:
                        tail_ref[0] = res
                    else:
                        p = max([d2 for (_, d2) in BRANCHES if 1 < d2 < dil and dil % d2 == 0],
                                default=1)
                        s = dil // p
                        for c in range(nc // LANES):
                            cs = slice(c * LANES, (c + 1) * LANES)
                            if p == 1:
                                for r in range(dil):
                                    tsl[c, pl.ds(r, n, stride=dil), :] = res[r * n:(r + 1) * n, cs]
                            else:
                                for rl in range(p):
                                    for rh in range(s):
                                        r = p * rh + rl
                                        tsl2[c, pl.ds(rl * (tm // p) + rh, n, stride=s), :] = (
                                            res[r * n:(r + 1) * n, cs])
                                for rl in range(p):
                                    tsl[c, pl.ds(rl, tm // p, stride=p), :] = (
                                        tsl2[c, rl * (tm // p):(rl + 1) * (tm // p), :])
                        for c in range(nc // LANES):
                            tail_ref[0, :, c * LANES:(c + 1) * LANES] = tsl[c]


def _proj(x, gains, gi, w, wi, nc, tm, scale=1.0, tails=()):
    B, T, D = x.shape
    nt = T // tm
    out_specs, out_shape = [], []
    for (_, dil) in BRANCHES:
        out_specs.append(pl.BlockSpec((1, dil, tm // dil, nc), lambda b, i: (b, 0, i, 0)))
        out_shape.append(jax.ShapeDtypeStruct((B, dil, T // dil, nc), BF16))
    scratch = [pltpu.VMEM((N_SLABS, tm, LANES), F32), pltpu.VMEM((N_SLABS, tm, LANES), F32)]
    if tails:
        for rows, (_, dil) in zip(tails, BRANCHES):
            assert rows >= tm or dil == 1
            if rows < tm:
                out_specs.append(pl.BlockSpec((1, rows, nc), lambda b, i: (b, 0, 0)))
            else:
                first = nt - rows // tm
                out_specs.append(pl.BlockSpec(
                    (1, tm, nc), lambda b, i, first=first: (b, jnp.maximum(i - first, 0), 0)))
            out_shape.append(jax.ShapeDtypeStruct((B, rows, nc), F32))
        scratch += [pltpu.VMEM((nc // LANES, tm, LANES), F32) for _ in range(2)]
    scratch += [pltpu.VMEM((tm, D), BF16) for _ in BRANCHES]
    return pl.pallas_call(
        functools.partial(_proj_kernel, tm=tm, nc=nc, scale=scale, tails=tuple(tails)),
        grid=(B, nt),
        in_specs=[
            pl.BlockSpec((1, tm, D), lambda b, i: (b, i, 0)),
            pl.BlockSpec((None, 1, D), lambda b, i: (gi, 0, 0)),
            pl.BlockSpec((None, D, len(BRANCHES) * nc), lambda b, i: (wi, 0, 0),
                         pipeline_mode=pl.Buffered(1)),
        ],
        out_specs=out_specs,
        out_shape=out_shape,
        scratch_shapes=scratch,
        compiler_params=_cparams("arbitrary", "arbitrary"),
        name="proj_kv" if tails else "proj_q",
    )(x, gains, w)


def _proj_sample_kernel(x_ref, g_ref, w_ref, o_ref, *, scale):
    h = _rms(x_ref[...], g_ref[...]).astype(BF16)
    res = jnp.dot(h, w_ref[...], preferred_element_type=F32)
    if scale != 1.0:
        res = res * scale
    o_ref[...] = res


def _proj_sample(x, gains, gi, w, wi, scale=1.0, tn=3072):
    Bs, D = x.shape
    nc = w.shape[2]
    return pl.pallas_call(
        functools.partial(_proj_sample_kernel, scale=scale),
        grid=(nc // tn,),
        in_specs=[
            pl.BlockSpec((Bs, D), lambda j: (0, 0)),
            pl.BlockSpec((None, 1, D), lambda j: (gi, 0, 0)),
            pl.BlockSpec((None, D, tn), lambda j: (wi, 0, j)),
        ],
        out_specs=pl.BlockSpec((Bs, tn), lambda j: (0, j)),
        out_shape=jax.ShapeDtypeStruct((Bs, nc), F32),
        compiler_params=_cparams("arbitrary"),
        name="proj_sample",
    )(x, gains, w)


def _attn_kernel(q_ref, kv_ref, bias_ref, o_ref, ml_ref, kprev, vprev, *, tq, rb):
    first = pl.program_id(2) == 0
    lane = lax.broadcasted_iota(jnp.int32, (Q_SUB, LANES), 1)
    ones = jnp.ones((2 * Q_SUB, HEAD_DIM), BF16)
    if rb == 1:
        @pl.when(first)
        def _():
            kprev[...] = jnp.zeros(kprev.shape, BF16)
            vprev[...] = jnp.zeros(vprev.shape, BF16)
    for rr in range(rb):
        for s in range(tq // Q_SUB):
            r0 = s * Q_SUB

            def keys(h, col0, s=s, r0=r0, rr=rr):
                cs = slice(col0 + h * HEAD_DIM, col0 + (h + 1) * HEAD_DIM)
                if s > 0:
                    return kv_ref[0, rr, r0 - Q_SUB:r0 + Q_SUB, cs]
                cur = kv_ref[0, rr, 0:Q_SUB, cs]
                if rb == 1:
                    prev = (kprev if col0 == 0 else vprev)[:, h * HEAD_DIM:(h + 1) * HEAD_DIM]
                else:
                    prev = jnp.zeros((Q_SUB, HEAD_DIM), BF16)
                return jnp.concatenate([prev, cur], axis=0)

            if s > 0:
                boff = 0
            elif rb > 1:
                boff = HEADS
            else:
                boff = jnp.where(first, HEADS, 0)
            ml_tile = jnp.zeros((Q_SUB, LANES), F32)
            for h in range(HEADS):
                cs = slice(h * HEAD_DIM, (h + 1) * HEAD_DIM)
                q = q_ref[0, rr, r0:r0 + Q_SUB, cs]
                sc = lax.dot_general(q, keys(h, 0), (((1,), (1,)), ((), ())),
                                     preferred_element_type=F32)
                sc = sc + bias_ref[boff + h]
                m = jnp.max(sc, axis=-1, keepdims=True)
                p = jnp.exp2(sc - m).astype(BF16)
                v_ext = jnp.concatenate([keys(h, D_MODEL), ones], axis=1)
                o = jnp.dot(p, v_ext, preferred_element_type=F32)
                o_ref[0, rr, r0:r0 + Q_SUB, cs] = o[:, :HEAD_DIM].astype(o_ref.dtype)
                ml_tile = jnp.where(lane == h, m, ml_tile)
                ml_tile = jnp.where(lane == HEADS + h, o[:, HEAD_DIM:], ml_tile)
            ml_ref[0, rr, r0:r0 + Q_SUB, :] = ml_tile
    if rb == 1:
        kprev[...] = kv_ref[0, 0, tq - Q_SUB:tq, 0:D_MODEL]
        vprev[...] = kv_ref[0, 0, tq - Q_SUB:tq, D_MODEL:2 * D_MODEL]


def _attn_branch(q, kv, bias, tq, rb):
    B, dil, L, D = q.shape
    assert rb == 1 or tq == L
    return pl.pallas_call(
        functools.partial(_attn_kernel, tq=tq, rb=rb),
        grid=(B, dil // rb, L // tq),
        in_specs=[
            pl.BlockSpec((1, rb, tq, D), lambda b, r, i: (b, r, i, 0)),
            pl.BlockSpec((1, rb, tq, 2 * D), lambda b, r, i: (b, r, i, 0)),
            pl.BlockSpec(bias.shape, lambda b, r, i: (0, 0, 0)),
        ],
        out_specs=[
            pl.BlockSpec((1, rb, tq, D), lambda b, r, i: (b, r, i, 0)),
            pl.BlockSpec((1, rb, tq, LANES), lambda b, r, i: (b, r, i, 0)),
        ],
        out_shape=[
            jax.ShapeDtypeStruct((B, dil, L, D), BF16),
            jax.ShapeDtypeStruct((B, dil, L, LANES), F32),
        ],
        scratch_shapes=[pltpu.VMEM((Q_SUB, D), BF16), pltpu.VMEM((Q_SUB, D), BF16)],
        compiler_params=_cparams("arbitrary", "arbitrary", "arbitrary"),
        name=f"attn_d{dil}",
    )(q, kv, bias)


def _merge_pieces(x_ref, o_refs, ml_refs, wo_ref, osls, wsl, lhs, x1_ref, tm):
    def weights():
        tiles = []
        for g, (_, dil) in enumerate(BRANCHES):
            if dil == 1:
                tiles.append(ml_refs[g][0, 0])
            else:
                n = tm // dil
                for r in range(dil):
                    wsl[g, pl.ds(r, n, stride=dil), :] = ml_refs[g][0, r]
                tiles.append(wsl[g])
        m = jnp.maximum(jnp.maximum(tiles[0], tiles[1]), tiles[2])
        us = [jnp.exp2(tl - m) for tl in tiles]
        ls = [pltpu.roll(tl, LANES - HEADS, axis=1) for tl in tiles]
        den = us[0] * ls[0] + us[1] * ls[1] + us[2] * ls[2]
        for g in range(len(BRANCHES)):
            wsl[g] = us[g] / den

    def interleave(g, r_lo, r_hi):
        dil = BRANCHES[g][1]
        n = tm // dil
        for r in range(r_lo, r_hi):
            for c in range(N_SLABS):
                osls[g][c, pl.ds(r, n, stride=dil), :] = (
                    o_refs[g][0, r, :, c * LANES:(c + 1) * LANES].astype(F32))

    def interleave_by(g, gp, r_lo, r_hi):
        dil, p = BRANCHES[g][1], BRANCHES[gp][1]
        n, s = tm // dil, dil // p
        for r in range(r_lo, r_hi):
            rl, rh = r % p, r // p
            for c in range(N_SLABS):
                osls[gp][c, pl.ds(rl * (tm // p) + rh, n, stride=s), :] = (
                    o_refs[g][0, r, :, c * LANES:(c + 1) * LANES].astype(F32))

    def interleave_finish(g, gp):
        p = BRANCHES[gp][1]
        for rl in range(p):
            for c in range(N_SLABS):
                osls[g][c, pl.ds(rl, tm // p, stride=p), :] = (
                    osls[gp][c, rl * (tm // p):(rl + 1) * (tm // p), :])

    def combine(c_lo, c_hi):
        for c in range(c_lo, c_hi):
            cs = slice(c * LANES, (c + 1) * LANES)
            merged = wsl[0, :, c:c + 1] * o_refs[0][0, 0, :, cs].astype(F32)
            for g in range(1, len(BRANCHES)):
                merged = merged + wsl[g, :, c:c + 1] * osls[g][c]
            lhs[:, cs] = merged.astype(BF16)

    def project():
        x1_ref[...] = x_ref[0] + jnp.dot(lhs[...], wo_ref[...], preferred_element_type=F32)

    pieces = [weights]
    order = sorted((g for g, (_, dil) in enumerate(BRANCHES) if dil > 1),
                   key=lambda g: -BRANCHES[g][1])
    done = []
    for g in order:
        dil = BRANCHES[g][1]
        gp = next((g2 for g2 in order if g2 not in done and g2 != g
                   and 1 < BRANCHES[g2][1] < dil and dil % BRANCHES[g2][1] == 0), None)
        if gp is None:
            pieces.append(functools.partial(interleave, g, 0, dil))
        else:
            pieces += [functools.partial(interleave_by, g, gp, 0, dil // 2),
                       functools.partial(interleave_by, g, gp, dil // 2, dil),
                       functools.partial(interleave_finish, g, gp)]
        done.append(g)
    tail = pieces[-2:]
    pieces = pieces[:-2] + [lambda: [f() for f in tail]]
    pieces += [functools.partial(combine, 0, N_SLABS // 2),
               functools.partial(combine, N_SLABS // 2, N_SLABS), project]
    return pieces


def _merge_mlp_kernel(x_ref, xs1_ref, o0_ref, o1_ref, o2_ref, ml0_ref, ml1_ref, ml2_ref, wo_ref,
                      gmlp_ref, w1_ref, w2_ref, gf_ref, *refs, tm, tc, final_norm, ride):
    f32_refs, refs = refs[:ride], refs[ride:]
    out_ref, os_ref = refs[:2]
    bf16_refs, refs = refs[2:2 + ride], refs[2 + ride:]
    x1a, x1b, hsa, hsb, us, osl1, osl2, wsl, lhs = refs
    cast = _cast_rider(f32_refs, bf16_refs)

    def run(x1_w, hs_w, x1_r, hs_r):
        def next_norm():
            hs_w[...] = _rms(x1_w[...], gmlp_ref[...]).astype(BF16)
            cast()

        pieces = _merge_pieces(x_ref, (o0_ref, o1_ref, o2_ref), (ml0_ref, ml1_ref, ml2_ref),
                               wo_ref, (None, osl1, osl2), wsl, lhs, x1_w, tm)
        _mlp_tile(x1_r, hs_r, w1_ref, w2_ref, gf_ref, out_ref, us, tc, final_norm,
                  between=pieces + [next_norm])

    _run_pipelined(((x1a, hsa), (x1b, hsb)), xs1_ref, gmlp_ref, out_ref, os_ref, run)


def _merge_mlp(x, xs1, os_, mls, wo, wi, gmlp, w1, w2, gain_final, li, final_norm, riders,
               tm=512, tc=512):
    B, T, D = x.shape
    Bs = xs1.shape[0]
    nt = T // tm
    n_tiles = B * nt

    def tile(j):
        jj = jnp.minimum(j, n_tiles - 1)
        return jj // nt, jj % nt

    one = pl.Buffered(1)
    in_specs = [pl.BlockSpec((1, tm, D), lambda j: (*tile(j), 0)),
                pl.BlockSpec((Bs, D), lambda j: (0, 0))]
    for (_, dil) in BRANCHES:
        in_specs.append(pl.BlockSpec((1, dil, tm // dil, D), lambda j: (tile(j)[0], 0, tile(j)[1], 0)))
    for (_, dil) in BRANCHES:
        in_specs.append(pl.BlockSpec((1, dil, tm // dil, LANES),
                                     lambda j: (tile(j)[0], 0, tile(j)[1], 0)))
    in_specs += [
        pl.BlockSpec((None, D, D), lambda j: (wi, 0, 0), pipeline_mode=one),
        pl.BlockSpec((None, 1, D), lambda j: (li, 0, 0)),
        pl.BlockSpec((D, D_FF), lambda j: (0, 0), pipeline_mode=one),
        pl.BlockSpec((D_FF, D), lambda j: (0, 0), pipeline_mode=one),
        pl.BlockSpec((1, D), lambda j: (0, 0)),
    ]
    ride_args, ride_in, ride_out, ride_shape = _rider_specs(riders, n_tiles)
    res = pl.pallas_call(
        functools.partial(_merge_mlp_kernel, tm=tm, tc=tc, final_norm=final_norm,
                          ride=len(riders)),
        grid=(n_tiles + 1,),
        in_specs=in_specs + ride_in,
        out_specs=[pl.BlockSpec((tm, D), lambda j: (jnp.maximum(j - 1, 0), 0)),
                   pl.BlockSpec((Bs, D), lambda j: (0, 0))] + ride_out,
        out_shape=[jax.ShapeDtypeStruct((B * T, D), F32),
                   jax.ShapeDtypeStruct((Bs, D), F32)] + ride_shape,
        scratch_shapes=[
            pltpu.VMEM((tm, D), F32), pltpu.VMEM((tm, D), F32),
            pltpu.VMEM((tm, D), BF16), pltpu.VMEM((tm, D), BF16),
            pltpu.VMEM((tm, D_FF), BF16),
            pltpu.VMEM((N_SLABS, tm, LANES), F32), pltpu.VMEM((N_SLABS, tm, LANES), F32),
            pltpu.VMEM((len(BRANCHES), tm, LANES), F32),
            pltpu.VMEM((tm, D), BF16),
        ],
        compiler_params=_cparams("arbitrary"),
        name="merge_mlp",
    )(x, xs1, *os_, *mls, wo, gmlp, w1, w2, gain_final, *ride_args)
    return (res[0].reshape(B, T, D),) + tuple(res[1:])


def _attn_sample_kernel(q_ref, kvn_ref, c0_ref, c1_ref, c2_ref, bias_ref, o_ref, *, nb):
    caches = (c0_ref, c1_ref, c2_ref)
    for i in range(nb):
        sc_c, sc_n = [], []
        for g in range(len(BRANCHES)):
            q = q_ref[i, g]
            kc = caches[g][i, :, 0]
            sc_c.append(jnp.sum(kc * q[None], axis=-1, keepdims=True) + bias_ref[g, 0:N_KEYS - 1])
            sc_n.append(jnp.sum(kvn_ref[i, g, 0] * q, axis=-1, keepdims=True)
                        + bias_ref[g, N_KEYS - 1])
        m = sc_n[0]
        for g in range(len(BRANCHES)):
            m = jnp.maximum(m, jnp.maximum(jnp.max(sc_c[g], axis=0), sc_n[g]))
        num = jnp.zeros((HEADS, HEAD_DIM), F32)
        den = jnp.zeros((HEADS, HEAD_DIM), F32)
        for g in range(len(BRANCHES)):
            p_c = jnp.exp2(sc_c[g] - m[None])
            p_n = jnp.exp2(sc_n[g] - m)
            den = den + jnp.sum(p_c, axis=0) + p_n
            num = num + jnp.sum(p_c * caches[g][i, :, 1], axis=0) + p_n * kvn_ref[i, g, 1]
        o_ref[i] = num / den


def _attn_sample(q, kvn, caches, bias, nb=4):
    Bs = q.shape[0]
    in_specs = [
        pl.BlockSpec((nb,) + q.shape[1:], lambda b: (b, 0, 0, 0)),
        pl.BlockSpec((nb,) + kvn.shape[1:], lambda b: (b, 0, 0, 0, 0)),
    ]
    for c in caches:
        in_specs.append(pl.BlockSpec((nb, N_KEYS - 1, None, 2, HEADS, HEAD_DIM),
                                     lambda b: (b, 0, 0, 0, 0, 0)))
    in_specs.append(pl.BlockSpec(bias.shape, lambda b: (0, 0, 0, 0)))
    return pl.pallas_call(
        functools.partial(_attn_sample_kernel, nb=nb),
        grid=(Bs // nb,),
        in_specs=in_specs,
        out_specs=pl.BlockSpec((nb, HEADS, HEAD_DIM), lambda b: (b, 0, 0)),
        out_shape=jax.ShapeDtypeStruct((Bs, HEADS, HEAD_DIM), F32),
        compiler_params=_cparams("arbitrary"),
        name="attn_sample",
    )(q, kvn, *caches, bias)


def _out_sample_kernel(x_ref, o_ref, wo_ref, out_ref):
    out_ref[...] = x_ref[...] + jnp.dot(o_ref[...].astype(BF16), wo_ref[...],
                                        preferred_element_type=F32)


def _out_sample(x, o, wo):
    return pl.pallas_call(
        _out_sample_kernel,
        out_shape=jax.ShapeDtypeStruct(x.shape, F32),
        compiler_params=pltpu.CompilerParams(vmem_limit_bytes=VMEM_LIMIT),
        name="out_sample",
    )(x, o, wo)


def _t5_bucket(dist):
    max_exact = NUM_BUCKETS // 2
    df = jnp.maximum(dist, 1).astype(F32)
    large = max_exact + (jnp.log(df / max_exact) / math.log(MAX_DISTANCE / max_exact)
                         * (NUM_BUCKETS - max_exact)).astype(jnp.int32)
    large = jnp.minimum(large, NUM_BUCKETS - 1)
    return jnp.where(dist < max_exact, dist, large)


def _bias_tables(rel_bias):
    out = []
    for g, (w, d) in enumerate(BRANCHES):
        dist = jnp.arange(N_KEYS, dtype=jnp.int32) * d
        out.append(rel_bias[_t5_bucket(dist)][:, g * HEADS:(g + 1) * HEADS])
    return out


def _block_bias(tab):
    pad_lo = jnp.full((HEADS, Q_SUB - 1), MASKED, F32)
    pad_hi = jnp.full((HEADS, Q_SUB), MASKED, F32)
    f = jnp.concatenate([pad_lo, tab[::-1].T, pad_hi], axis=1)
    width = f.shape[1]
    flat = jnp.tile(f, (1, Q_SUB))[:, :Q_SUB * (width - 1)]
    toep = flat.reshape(HEADS, Q_SUB, width - 1)
    return toep[:, :, Q_SUB - 1:Q_SUB - 1 + 2 * Q_SUB]


def kernel(x_prompt, x_sample, state_pool, cache_kv_w128, cache_kv_w512, cache_kv_w2048,
           norm_mix, pool_w, pool_scale, norm_mlp, mlp_in, mlp_out, norm_kv, w_kv, w_q, w_o,
           rel_bias, norm_final):
    B, T, D = x_prompt.shape
    Bs = x_sample.shape[0]
    n_a = pool_w.shape[0]
    depth = norm_mix.shape[0]
    n_br = len(BRANCHES)
    qk_scale = HEAD_DIM ** -0.5

    pool_w_b = pool_w.astype(BF16)
    w1_b, w2_b = mlp_in[0].astype(BF16), mlp_out[0].astype(BF16)
    ride_proj = n_a >= 2
    extra = {0: [w_kv[None]], 1: [w_q.reshape(1, -1, w_q.shape[-1]), w_o.reshape(1, -1, D)]}
    if not ride_proj:
        w_kv_b, w_q_b, w_o_b = w_kv.astype(BF16)[None], w_q.astype(BF16), w_o.astype(BF16)
    row = lambda v: v.reshape(1, D)
    g_mix = norm_mix.reshape(depth, 1, D)
    g_mlp = norm_mlp.reshape(depth, 1, D)
    g_kv = norm_kv.reshape(1, 1, D)
    g_scale = pool_scale.reshape(n_a, 1, D)

    tabs = _bias_tables(rel_bias)
    no_prev = np.arange(2 * Q_SUB)[None, None, :] < Q_SUB
    blk_bias = []
    for t in tabs:
        bb = _block_bias(t * LOG2E)
        blk_bias.append(jnp.concatenate([bb, jnp.where(no_prev, MASKED, bb)], axis=0))
    smp_bias = jnp.stack([jnp.broadcast_to((t * LOG2E)[::-1][:, :, None], (N_KEYS, HEADS, HEAD_DIM))
                          for t in tabs])

    caches = [c.reshape(Bs, N_KEYS - 1, d, 2, HEADS, HEAD_DIM)
              for c, (w, d) in zip((cache_kv_w128, cache_kv_w512, cache_kv_w2048), BRANCHES)]

    xp = x_prompt
    xs = x_sample.reshape(Bs, D)
    pool_p, pool_s, kv_p = [], [], []
    kvs = kvn = None
    for l in range(depth):
        last = l == depth - 1
        riders = [] if last else [(mlp_in, l + 1), (mlp_out, l + 1)]
        if l < n_a:
            if ride_proj and l in extra:
                riders += [(w, 0) for w in extra[l]]
            xs1, hs = _pool_sample(xs, state_pool[l], row(norm_mix[l]), pool_w_b[l], row(pool_scale[l]))
            pool_s.append(jnp.concatenate([state_pool[l][:, 1:], hs[:, None, :]], axis=1))
            res = _pool_mlp(xp, xs1, g_mix, l, pool_w_b, g_scale, g_mlp, w1_b, w2_b,
                            row(norm_final), riders)
            xp, xs, nbuf = res[:3]
            pool_p.append(nbuf)
            cast = res[3:]
            if ride_proj and l == 0:
                w_kv_b = cast[2][None]
            if ride_proj and l == 1:
                w_q_b, w_o_b = cast[2].reshape(w_q.shape), cast[3].reshape(w_o.shape)
        else:
            lb = l - n_a
            if l == n_a:
                tails = tuple(min(w, T) for (w, _) in BRANCHES)
                res = _proj(xp, g_kv, 0, w_kv_b, 0, 2 * D, 256, tails=tails)
                kvs = res[:n_br]
                kv_p = [t.reshape(B, rows, 2, HEADS, HEAD_DIM) for t, rows in zip(res[n_br:], tails)]
                kvn = _proj_sample(xs, g_kv, 0, w_kv_b, 0)
            qs_p = _proj(xp, g_mix, l, w_q_b, lb, D, 1024, scale=qk_scale * LOG2E)
            qs = _proj_sample(xs, g_mix, l, w_q_b, lb, scale=qk_scale * LOG2E)
            os_, lses = [], []
            for g, (w, d) in enumerate(BRANCHES):
                L = T // d
                tq, rb = (L, min(d, ATTN_ROWS // L)) if L < ATTN_ROWS else (ATTN_ROWS, 1)
                o_g, ml_g = _attn_branch(qs_p[g], kvs[g], blk_bias[g], tq, rb)
                os_.append(o_g)
                lses.append(ml_g)
            o_s = _attn_sample(qs.reshape(Bs, n_br, HEADS, HEAD_DIM),
                               kvn.reshape(Bs, n_br, 2, HEADS, HEAD_DIM), caches, smp_bias)
            xs1 = _out_sample(xs, o_s.reshape(Bs, D), w_o_b[lb])
            res = _merge_mlp(xp, xs1, os_, lses, w_o_b, lb, g_mlp, w1_b, w2_b, row(norm_final),
                             l, last, riders)
            xp, xs = res[:2]
            cast = res[2:]
        if not last:
            w1_b, w2_b = cast[:2]

    kv_s = [kvn[:, g * 2 * D:(g + 1) * 2 * D].reshape(Bs, 1, 2, HEADS, HEAD_DIM) for g in range(n_br)]
    return (xp, xs.reshape(Bs, 1, D), jnp.stack(pool_p), jnp.stack(pool_s),
            kv_p[0], kv_s[0], kv_p[1], kv_s[1], kv_p[2], kv_s[2])
```

```python
import functools
import math

import jax
import jax.numpy as jnp
import numpy as np
from jax import lax
from jax.experimental import pallas as pl
from jax.experimental.pallas import tpu as pltpu

F32 = jnp.float32
BF16 = jnp.bfloat16

D_MODEL = 1024
HEADS = 8
HEAD_DIM = 128
D_FF = 4 * D_MODEL
POOL_WINDOWS = (2, 4, 8, 16)
POOL_GROUP = D_MODEL // len(POOL_WINDOWS)
POOL_STATE = max(POOL_WINDOWS) - 1
BRANCHES = ((128, 1), (512, 4), (2048, 16))
N_KEYS = 129
NUM_BUCKETS = 32
MAX_DISTANCE = 2048
SAMPLE_POS0 = 8192
EPS = 1e-6
MASKED = -1e30
LOG2E = math.log2(math.e)

LANES = 128
N_SLABS = D_MODEL // LANES
Q_SUB = 128
ATTN_ROWS = 2048
VMEM_LIMIT = 60 * 1024 * 1024


def _cparams(*sem):
    return pltpu.CompilerParams(dimension_semantics=sem, vmem_limit_bytes=VMEM_LIMIT)


def _rms(x, g):
    ms = jnp.mean(x * x, axis=-1, keepdims=True)
    return x * lax.rsqrt(ms + EPS) * g


POOL_BASE = 24
POOL_LO = 8


def _pool_pieces(x_ref, g_ref, w_ref, sc_ref, nb_ref, hbuf, sbuf, x1_ref, t, tm):
    end = POOL_BASE + tm

    def norm():
        hbuf[POOL_BASE:end, :] = _rms(x_ref[0], g_ref[...])

    def group(gi, w):
        sl = slice(gi * POOL_GROUP, (gi + 1) * POOL_GROUP)
        n_stage = w.bit_length() - 1
        s = None
        for si in range(n_stage):
            sh = 1 << si
            lo = POOL_BASE if si == n_stage - 1 else POOL_LO
            if si == 0:
                val = hbuf[lo:end, sl] + hbuf[lo - sh:end - sh, sl]
            else:
                val = sbuf[si - 1, lo:end, :] + sbuf[si - 1, lo - sh:end - sh, :]
            if si == n_stage - 1:
                s = val
            else:
                sbuf[si, lo:end, :] = val
        pos = t * tm + lax.broadcasted_iota(jnp.int32, (tm, 1), 0)
        cnt = jnp.minimum(pos + 1, w).astype(F32)
        pooled = s / cnt - hbuf[POOL_BASE:end, sl]
        y = jnp.dot(pooled.astype(BF16), w_ref[gi], preferred_element_type=F32)
        x1_ref[:, sl] = x_ref[0, :, sl] + y * sc_ref[:, sl]

    def carry():
        nb_ref[0] = hbuf[end - POOL_STATE:end, :]
        hbuf[POOL_BASE - 16:POOL_BASE, :] = hbuf[end - 16:end, :]

    pieces = [norm]
    pieces += [functools.partial(group, gi, w) for gi, w in enumerate(POOL_WINDOWS)]
    return pieces + [carry]


def _mlp_tile(x1_ref, hs_ref, w1_ref, w2_ref, gf_ref, o_ref, us, tc, final_norm, between=()):
    between = list(between)
    n_chunks = D_FF // tc
    assert len(between) <= n_chunks
    for c in range(n_chunks):
        if c < len(between):
            between[c]()
        u = jnp.dot(hs_ref[...], w1_ref[:, c * tc:(c + 1) * tc], preferred_element_type=F32)
        us[:, c * tc:(c + 1) * tc] = jnp.square(jnp.maximum(u, 0.0)).astype(BF16)
    y = x1_ref[...] + jnp.dot(us[...], w2_ref[...], preferred_element_type=F32)
    if final_norm:
        y = _rms(y, gf_ref[...])
    o_ref[...] = y


def _run_pipelined(slots, xs1_ref, gmlp_ref, o_ref, os_ref, run):
    (x1a, hsa), (x1b, hsb) = slots
    j = pl.program_id(0)
    bs = xs1_ref.shape[0]

    @pl.when(j == 0)
    def _():
        x1b[...] = jnp.zeros(x1b.shape, F32)
        hsb[...] = jnp.zeros(hsb.shape, BF16)
        xs1 = xs1_ref[...]
        x1b[0:bs, :] = xs1
        hsb[0:bs, :] = _rms(xs1, gmlp_ref[...]).astype(BF16)

    @pl.when(j % 2 == 0)
    def _():
        run(x1a, hsa, x1b, hsb)

    @pl.when(j % 2 == 1)
    def _():
        run(x1b, hsb, x1a, hsa)

    @pl.when(j == 0)
    def _():
        os_ref[...] = o_ref[0:bs, :]


def _cast_rider(f32_refs, bf16_refs):
    def cast():
        for src, dst in zip(f32_refs, bf16_refs):
            dst[...] = src[...].astype(BF16)
    return cast


def _rider_specs(riders, n_tiles):
    chunk = lambda j: jnp.minimum(j, n_tiles - 1)
    args, in_specs, out_specs, out_shape = [], [], [], []
    for w, ln in riders:
        rows = w.shape[1] // n_tiles
        args.append(w)
        in_specs.append(pl.BlockSpec((None, rows, w.shape[2]), lambda j, ln=ln: (ln, chunk(j), 0)))
        out_specs.append(pl.BlockSpec((rows, w.shape[2]), lambda j: (chunk(j), 0)))
        out_shape.append(jax.ShapeDtypeStruct(w.shape[1:], BF16))
    return args, in_specs, out_specs, out_shape


def _pool_mlp_kernel(x_ref, xs1_ref, gmix_ref, wp_ref, sc_ref, gmlp_ref, w1_ref, w2_ref, gf_ref,
                     *refs, tm, nt, n_tiles, tc, ride):
    f32_refs, refs = refs[:ride], refs[ride:]
    o_ref, os_ref, nb_ref = refs[:3]
    bf16_refs, refs = refs[3:3 + ride], refs[3 + ride:]
    x1a, x1b, hsa, hsb, us, hbuf, sbuf = refs
    cast = _cast_rider(f32_refs, bf16_refs)
    j = pl.program_id(0)
    t = jnp.minimum(j, n_tiles - 1) % nt

    @pl.when(j == 0)
    def _():
        hbuf[0:POOL_BASE, :] = jnp.zeros((POOL_BASE, D_MODEL), F32)
        sbuf[...] = jnp.zeros(sbuf.shape, F32)

    @pl.when(t == 0)
    def _():
        hbuf[POOL_BASE - 16:POOL_BASE, :] = jnp.zeros((16, D_MODEL), F32)

    def run(x1_w, hs_w, x1_r, hs_r):
        def next_norm():
            hs_w[...] = _rms(x1_w[...], gmlp_ref[...]).astype(BF16)

        pieces = _pool_pieces(x_ref, gmix_ref, wp_ref, sc_ref, nb_ref, hbuf, sbuf, x1_w, t, tm)
        _mlp_tile(x1_r, hs_r, w1_ref, w2_ref, gf_ref, o_ref, us, tc, False,
                  between=pieces + [next_norm, cast])

    _run_pipelined(((x1a, hsa), (x1b, hsb)), xs1_ref, gmlp_ref, o_ref, os_ref, run)


def _pool_mlp(x, xs1, gmix, li, w_pool, scale, gmlp, w1, w2, gain_final, riders, tm=512, tc=512):
    B, T, D = x.shape
    Bs = xs1.shape[0]
    nt = T // tm
    n_tiles = B * nt

    def tile(j):
        jj = jnp.minimum(j, n_tiles - 1)
        return jj // nt, jj % nt

    one = pl.Buffered(1)
    ride_args, ride_in, ride_out, ride_shape = _rider_specs(riders, n_tiles)
    res = pl.pallas_call(
        functools.partial(_pool_mlp_kernel, tm=tm, nt=nt, n_tiles=n_tiles, tc=tc,
                          ride=len(riders)),
        grid=(n_tiles + 1,),
        in_specs=[
            pl.BlockSpec((1, tm, D), lambda j: (*tile(j), 0)),
            pl.BlockSpec((Bs, D), lambda j: (0, 0)),
            pl.BlockSpec((None, 1, D), lambda j: (li, 0, 0)),
            pl.BlockSpec((None,) + w_pool.shape[1:], lambda j: (li, 0, 0, 0)),
            pl.BlockSpec((None, 1, D), lambda j: (li, 0, 0)),
            pl.BlockSpec((None, 1, D), lambda j: (li, 0, 0)),
            pl.BlockSpec((D, D_FF), lambda j: (0, 0), pipeline_mode=one),
            pl.BlockSpec((D_FF, D), lambda j: (0, 0), pipeline_mode=one),
            pl.BlockSpec((1, D), lambda j: (0, 0)),
        ] + ride_in,
        out_specs=[
            pl.BlockSpec((tm, D), lambda j: (jnp.maximum(j - 1, 0), 0)),
            pl.BlockSpec((Bs, D), lambda j: (0, 0)),
            pl.BlockSpec((1, POOL_STATE, D), lambda j: (tile(j)[0], 0, 0)),
        ] + ride_out,
        out_shape=[
            jax.ShapeDtypeStruct((B * T, D), F32),
            jax.ShapeDtypeStruct((Bs, D), F32),
            jax.ShapeDtypeStruct((B, POOL_STATE, D), F32),
        ] + ride_shape,
        scratch_shapes=[
            pltpu.VMEM((tm, D), F32), pltpu.VMEM((tm, D), F32),
            pltpu.VMEM((tm, D), BF16), pltpu.VMEM((tm, D), BF16),
            pltpu.VMEM((tm, D_FF), BF16),
            pltpu.VMEM((POOL_BASE + tm, D), F32),
            pltpu.VMEM((len(POOL_WINDOWS) - 1, POOL_BASE + tm, POOL_GROUP), F32),
        ],
        compiler_params=_cparams("arbitrary"),
        name="pool_mlp",
    )(x, xs1, gmix, w_pool, scale, gmlp, w1, w2, gain_final, *ride_args)
    return (res[0].reshape(B, T, D),) + tuple(res[1:])


def _pool_sample_kernel(x_ref, st_ref, g_ref, w_ref, sc_ref, o_ref, h_ref):
    x = x_ref[...]
    h = _rms(x, g_ref[...])
    h_ref[...] = h
    for gi, w in enumerate(POOL_WINDOWS):
        sl = slice(gi * POOL_GROUP, (gi + 1) * POOL_GROUP)
        hg = h[:, sl]
        s = hg
        for j in range(1, w):
            s = s + st_ref[:, POOL_STATE - j, sl]
        pooled = s / float(min(SAMPLE_POS0 + 1, w)) - hg
        y = jnp.dot(pooled.astype(BF16), w_ref[gi], preferred_element_type=F32)
        o_ref[:, sl] = x[:, sl] + y * sc_ref[:, sl]


def _pool_sample(x, state, gain, w_pool, scale):
    Bs, D = x.shape
    return pl.pallas_call(
        _pool_sample_kernel,
        out_shape=[jax.ShapeDtypeStruct((Bs, D), F32), jax.ShapeDtypeStruct((Bs, D), F32)],
        compiler_params=pltpu.CompilerParams(vmem_limit_bytes=VMEM_LIMIT),
        name="pool_sample",
    )(x, state, gain, w_pool, scale)


def _proj_kernel(x_ref, g_ref, w_ref, *refs, tm, nc, scale, tails):
    n_br = len(BRANCHES)
    o_refs = refs[:n_br]
    if tails:
        tail_refs = refs[n_br:2 * n_br]
        hsl, stage, tsl, tsl2 = refs[2 * n_br:2 * n_br + 4]
        lhs = refs[2 * n_br + 4:]
        i = pl.program_id(1)
        nt = pl.num_programs(1)
    else:
        hsl, stage = refs[n_br:n_br + 2]
        lhs = refs[n_br + 2:]
    h = _rms(x_ref[0], g_ref[...])
    for c in range(N_SLABS):
        hsl[c] = h[:, c * LANES:(c + 1) * LANES]
    prev = None
    for g, (_, dil) in enumerate(BRANCHES):
        n = tm // dil
        if dil == 1:
            lhs[g][...] = h.astype(BF16)
        else:
            src, p = prev if prev is not None and dil % prev[1] == 0 else (hsl, 1)
            s = dil // p
            keep = any(d2 > dil and d2 % dil == 0 for (_, d2) in BRANCHES[g + 1:])
            for rl in range(p):
                for rh in range(s):
                    r = p * rh + rl
                    for c in range(N_SLABS):
                        v = src[c, pl.ds(rl * (tm // p) + rh, n, stride=s), :]
                        if keep:
                            stage[c, r * n:(r + 1) * n, :] = v
                        lhs[g][r * n:(r + 1) * n, c * LANES:(c + 1) * LANES] = v.astype(BF16)
            if keep:
                prev = (stage, dil)
        res = jnp.dot(lhs[g][...], w_ref[:, g * nc:(g + 1) * nc], preferred_element_type=F32)
        if scale != 1.0:
            res = res * scale
        for r in range(dil):
            o_refs[g][0, r] = res[r * n:(r + 1) * n].astype(o_refs[g].dtype)

        if tails:
            tail_ref, rows = tail_refs[g], tails[g]
            if rows < tm:
                @pl.when(i == nt - 1)
                def _(tail_ref=tail_ref, rows=rows, res=res):
                    tail_ref[0] = res[tm - rows:tm]
            else:
                @pl.when(i >= nt - rows // tm)
                def _(tail_ref=tail_ref, res=res, dil=dil, n=n):
                    if dil == 1:
                        tail_ref[0] = res
                    else:
                        p = max([d2 for (_, d2) in BRANCHES if 1 < d2 < dil and dil % d2 == 0],
                                default=1)
                        s = dil // p
                        for c in range(nc // LANES):
                            cs = slice(c * LANES, (c + 1) * LANES)
                            if p == 1:
                                for r in range(dil):
                                    tsl[c, pl.ds(r, n, stride=dil), :] = res[r * n:(r + 1) * n, cs]
                            else:
                                for rl in range(p):
                                    for rh in range(s):
                                        r = p * rh + rl
                                        tsl2[c, pl.ds(rl * (tm // p) + rh, n, stride=s), :] = (
                                            res[r * n:(r + 1) * n, cs])
                                for rl in range(p):
                                    tsl[c, pl.ds(rl, tm // p, stride=p), :] = (
                                        tsl2[c, rl * (tm // p):(rl + 1) * (tm // p), :])
                        for c in range(nc // LANES):
                            tail_ref[0, :, c * LANES:(c + 1) * LANES] = tsl[c]


def _proj(x, gains, gi, w, wi, nc, tm, scale=1.0, tails=()):
    B, T, D = x.shape
    nt = T // tm
    out_specs, out_shape = [], []
    for (_, dil) in BRANCHES:
        out_specs.append(pl.BlockSpec((1, dil, tm // dil, nc), lambda b, i: (b, 0, i, 0)))
        out_shape.append(jax.ShapeDtypeStruct((B, dil, T // dil, nc), BF16))
    scratch = [pltpu.VMEM((N_SLABS, tm, LANES), F32), pltpu.VMEM((N_SLABS, tm, LANES), F32)]
    if tails:
        for rows, (_, dil) in zip(tails, BRANCHES):
            assert rows >= tm or dil == 1
            if rows < tm:
                out_specs.append(pl.BlockSpec((1, rows, nc), lambda b, i: (b, 0, 0)))
            else:
                first = nt - rows // tm
                out_specs.append(pl.BlockSpec(
                    (1, tm, nc), lambda b, i, first=first: (b, jnp.maximum(i - first, 0), 0)))
            out_shape.append(jax.ShapeDtypeStruct((B, rows, nc), F32))
        scratch += [pltpu.VMEM((nc // LANES, tm, LANES), F32) for _ in range(2)]
    scratch += [pltpu.VMEM((tm, D), BF16) for _ in BRANCHES]
    return pl.pallas_call(
        functools.partial(_proj_kernel, tm=tm, nc=nc, scale=scale, tails=tuple(tails)),
        grid=(B, nt),
        in_specs=[
            pl.BlockSpec((1, tm, D), lambda b, i: (b, i, 0)),
            pl.BlockSpec((None, 1, D), lambda b, i: (gi, 0, 0)),
            pl.BlockSpec((None, D, len(BRANCHES) * nc), lambda b, i: (wi, 0, 0),
                         pipeline_mode=pl.Buffered(1)),
        ],
        out_specs=out_specs,
        out_shape=out_shape,
        scratch_shapes=scratch,
        compiler_params=_cparams("arbitrary", "arbitrary"),
        name="proj_kv" if tails else "proj_q",
    )(x, gains, w)


def _proj_sample_kernel(x_ref, g_ref, w_ref, o_ref, *, scale):
    h = _rms(x_ref[...], g_ref[...]).astype(BF16)
    res = jnp.dot(h, w_ref[...], preferred_element_type=F32)
    if scale != 1.0:
        res = res * scale
    o_ref[...] = res


def _proj_sample(x, gains, gi, w, wi, scale=1.0, tn=3072):
    Bs, D = x.shape
    nc = w.shape[2]
    return pl.pallas_call(
        functools.partial(_proj_sample_kernel, scale=scale),
        grid=(nc // tn,),
        in_specs=[
            pl.BlockSpec((Bs, D), lambda j: (0, 0)),
            pl.BlockSpec((None, 1, D), lambda j: (gi, 0, 0)),
            pl.BlockSpec((None, D, tn), lambda j: (wi, 0, j)),
        ],
        out_specs=pl.BlockSpec((Bs, tn), lambda j: (0, j)),
        out_shape=jax.ShapeDtypeStruct((Bs, nc), F32),
        compiler_params=_cparams("arbitrary"),
        name="proj_sample",
    )(x, gains, w)


def _attn_kernel(q_ref, kv_ref, bias_ref, o_ref, ml_ref, kprev, vprev, *, tq, rb):
    first = pl.program_id(2) == 0
    lane = lax.broadcasted_iota(jnp.int32, (Q_SUB, LANES), 1)
    ones = jnp.ones((2 * Q_SUB, HEAD_DIM), BF16)
    if rb == 1:
        @pl.when(first)
        def _():
            kprev[...] = jnp.zeros(kprev.shape, BF16)
            vprev[...] = jnp.zeros(vprev.shape, BF16)
    for rr in range(rb):
        for s in range(tq // Q_SUB):
            r0 = s * Q_SUB

            def keys(h, col0, s=s, r0=r0, rr=rr):
                cs = slice(col0 + h * HEAD_DIM, col0 + (h + 1) * HEAD_DIM)
                if s > 0:
                    return kv_ref[0, rr, r0 - Q_SUB:r0 + Q_SUB, cs]
                cur = kv_ref[0, rr, 0:Q_SUB, cs]
                if rb == 1:
                    prev = (kprev if col0 == 0 else vprev)[:, h * HEAD_DIM:(h + 1) * HEAD_DIM]
                else:
                    prev = jnp.zeros((Q_SUB, HEAD_DIM), BF16)
                return jnp.concatenate([prev, cur], axis=0)

            if s > 0:
                boff = 0
            elif rb > 1:
                boff = HEADS
            else:
                boff = jnp.where(first, HEADS, 0)
            ml_tile = jnp.zeros((Q_SUB, LANES), F32)
            for h in range(HEADS):
                cs = slice(h * HEAD_DIM, (h + 1) * HEAD_DIM)
                q = q_ref[0, rr, r0:r0 + Q_SUB, cs]
                sc = lax.dot_general(q, keys(h, 0), (((1,), (1,)), ((), ())),
                                     preferred_element_type=F32)
                sc = sc + bias_ref[boff + h]
                m = jnp.max(sc, axis=-1, keepdims=True)
                p = jnp.exp2(sc - m).astype(BF16)
                v_ext = jnp.concatenate([keys(h, D_MODEL), ones], axis=1)
                o = jnp.dot(p, v_ext, preferred_element_type=F32)
                o_ref[0, rr, r0:r0 + Q_SUB, cs] = o[:, :HEAD_DIM].astype(o_ref.dtype)
                ml_tile = jnp.where(lane == h, m, ml_tile)
                ml_tile = jnp.where(lane == HEADS + h, o[:, HEAD_DIM:], ml_tile)
            ml_ref[0, rr, r0:r0 + Q_SUB, :] = ml_tile
    if rb == 1:
        kprev[...] = kv_ref[0, 0, tq - Q_SUB:tq, 0:D_MODEL]
        vprev[...] = kv_ref[0, 0, tq - Q_SUB:tq, D_MODEL:2 * D_MODEL]


def _attn_branch(q, kv, bias, g, tq, rb):
    B, dil, L, D = q.shape
    assert rb == 1 or tq == L
    return pl.pallas_call(
        functools.partial(_attn_kernel, tq=tq, rb=rb),
        grid=(B, dil // rb, L // tq),
        in_specs=[
            pl.BlockSpec((1, rb, tq, D), lambda b, r, i: (b, r, i, 0)),
            pl.BlockSpec((1, rb, tq, 2 * D), lambda b, r, i: (b, r, i, 0)),
            pl.BlockSpec((None,) + bias.shape[1:], lambda b, r, i: (g, 0, 0, 0)),
        ],
        out_specs=[
            pl.BlockSpec((1, rb, tq, D), lambda b, r, i: (b, r, i, 0)),
            pl.BlockSpec((1, rb, tq, LANES), lambda b, r, i: (b, r, i, 0)),
        ],
        out_shape=[
            jax.ShapeDtypeStruct((B, dil, L, D), BF16),
            jax.ShapeDtypeStruct((B, dil, L, LANES), F32),
        ],
        scratch_shapes=[pltpu.VMEM((Q_SUB, D), BF16), pltpu.VMEM((Q_SUB, D), BF16)],
        compiler_params=_cparams("arbitrary", "arbitrary", "arbitrary"),
        name=f"attn_d{dil}",
    )(q, kv, bias)


def _merge_pieces(x_ref, o_refs, ml_refs, wo_ref, osls, wsl, lhs, x1_ref, tm):
    def weights():
        tiles = []
        for g, (_, dil) in enumerate(BRANCHES):
            if dil == 1:
                tiles.append(ml_refs[g][0, 0])
            else:
                n = tm // dil
                for r in range(dil):
                    wsl[g, pl.ds(r, n, stride=dil), :] = ml_refs[g][0, r]
                tiles.append(wsl[g])
        m = jnp.maximum(jnp.maximum(tiles[0], tiles[1]), tiles[2])
        us = [jnp.exp2(tl - m) for tl in tiles]
        ls = [pltpu.roll(tl, LANES - HEADS, axis=1) for tl in tiles]
        den = us[0] * ls[0] + us[1] * ls[1] + us[2] * ls[2]
        for g in range(len(BRANCHES)):
            wsl[g] = us[g] / den

    def interleave(g, r_lo, r_hi):
        dil = BRANCHES[g][1]
        n = tm // dil
        for r in range(r_lo, r_hi):
            for c in range(N_SLABS):
                osls[g][c, pl.ds(r, n, stride=dil), :] = (
                    o_refs[g][0, r, :, c * LANES:(c + 1) * LANES].astype(F32))

    def interleave_by(g, gp, r_lo, r_hi):
        dil, p = BRANCHES[g][1], BRANCHES[gp][1]
        n, s = tm // dil, dil // p
        for r in range(r_lo, r_hi):
            rl, rh = r % p, r // p
            for c in range(N_SLABS):
                osls[gp][c, pl.ds(rl * (tm // p) + rh, n, stride=s), :] = (
                    o_refs[g][0, r, :, c * LANES:(c + 1) * LANES].astype(F32))

    def interleave_finish(g, gp):
        p = BRANCHES[gp][1]
        for rl in range(p):
            for c in range(N_SLABS):
                osls[g][c, pl.ds(rl, tm // p, stride=p), :] = (
                    osls[gp][c, rl * (tm // p):(rl + 1) * (tm // p), :])

    def combine(c_lo, c_hi):
        for c in range(c_lo, c_hi):
            cs = slice(c * LANES, (c + 1) * LANES)
            merged = wsl[0, :, c:c + 1] * o_refs[0][0, 0, :, cs].astype(F32)
            for g in range(1, len(BRANCHES)):
                merged = merged + wsl[g, :, c:c + 1] * osls[g][c]
            lhs[:, cs] = merged.astype(BF16)

    def project():
        x1_ref[...] = x_ref[0] + jnp.dot(lhs[...], wo_ref[...], preferred_element_type=F32)

    pieces = [weights]
    order = sorted((g for g, (_, dil) in enumerate(BRANCHES) if dil > 1),
                   key=lambda g: -BRANCHES[g][1])
    done = []
    for g in order:
        dil = BRANCHES[g][1]
        gp = next((g2 for g2 in order if g2 not in done and g2 != g
                   and 1 < BRANCHES[g2][1] < dil and dil % BRANCHES[g2][1] == 0), None)
        if gp is None:
            pieces.append(functools.partial(interleave, g, 0, dil))
        else:
            pieces += [functools.partial(interleave_by, g, gp, 0, dil // 2),
                       functools.partial(interleave_by, g, gp, dil // 2, dil),
                       functools.partial(interleave_finish, g, gp)]
        done.append(g)
    tail = pieces[-2:]
    pieces = pieces[:-2] + [lambda: [f() for f in tail]]
    pieces += [functools.partial(combine, 0, N_SLABS // 2),
               functools.partial(combine, N_SLABS // 2, N_SLABS), project]
    return pieces


def _merge_mlp_kernel(x_ref, xs1_ref, o0_ref, o1_ref, o2_ref, ml0_ref, ml1_ref, ml2_ref, wo_ref,
                      gmlp_ref, w1_ref, w2_ref, gf_ref, *refs, tm, tc, final_norm, ride):
    f32_refs, refs = refs[:ride], refs[ride:]
    out_ref, os_ref = refs[:2]
    bf16_refs, refs = refs[2:2 + ride], refs[2 + ride:]
    x1a, x1b, hsa, hsb, us, osl1, osl2, wsl, lhs = refs
    cast = _cast_rider(f32_refs, bf16_refs)

    def run(x1_w, hs_w, x1_r, hs_r):
        def next_norm():
            hs_w[...] = _rms(x1_w[...], gmlp_ref[...]).astype(BF16)
            cast()

        pieces = _merge_pieces(x_ref, (o0_ref, o1_ref, o2_ref), (ml0_ref, ml1_ref, ml2_ref),
                               wo_ref, (None, osl1, osl2), wsl, lhs, x1_w, tm)
        _mlp_tile(x1_r, hs_r, w1_ref, w2_ref, gf_ref, out_ref, us, tc, final_norm,
                  between=pieces + [next_norm])

    _run_pipelined(((x1a, hsa), (x1b, hsb)), xs1_ref, gmlp_ref, out_ref, os_ref, run)


def _merge_mlp(x, xs1, os_, mls, wo, wi, gmlp, w1, w2, gain_final, li, final_norm, riders,
               tm=512, tc=512):
    B, T, D = x.shape
    Bs = xs1.shape[0]
    nt = T // tm
    n_tiles = B * nt

    def tile(j):
        jj = jnp.minimum(j, n_tiles - 1)
        return jj // nt, jj % nt

    one = pl.Buffered(1)
    in_specs = [pl.BlockSpec((1, tm, D), lambda j: (*tile(j), 0)),
                pl.BlockSpec((Bs, D), lambda j: (0, 0))]
    for (_, dil) in BRANCHES:
        in_specs.append(pl.BlockSpec((1, dil, tm // dil, D), lambda j: (tile(j)[0], 0, tile(j)[1], 0)))
    for (_, dil) in BRANCHES:
        in_specs.append(pl.BlockSpec((1, dil, tm // dil, LANES),
                                     lambda j: (tile(j)[0], 0, tile(j)[1], 0)))
    in_specs += [
        pl.BlockSpec((None, D, D), lambda j: (wi, 0, 0), pipeline_mode=one),
        pl.BlockSpec((None, 1, D), lambda j: (li, 0, 0)),
        pl.BlockSpec((D, D_FF), lambda j: (0, 0), pipeline_mode=one),
        pl.BlockSpec((D_FF, D), lambda j: (0, 0), pipeline_mode=one),
        pl.BlockSpec((1, D), lambda j: (0, 0)),
    ]
    ride_args, ride_in, ride_out, ride_shape = _rider_specs(riders, n_tiles)
    res = pl.pallas_call(
        functools.partial(_merge_mlp_kernel, tm=tm, tc=tc, final_norm=final_norm,
                          ride=len(riders)),
        grid=(n_tiles + 1,),
        in_specs=in_specs + ride_in,
        out_specs=[pl.BlockSpec((tm, D), lambda j: (jnp.maximum(j - 1, 0), 0)),
                   pl.BlockSpec((Bs, D), lambda j: (0, 0))] + ride_out,
        out_shape=[jax.ShapeDtypeStruct((B * T, D), F32),
                   jax.ShapeDtypeStruct((Bs, D), F32)] + ride_shape,
        scratch_shapes=[
            pltpu.VMEM((tm, D), F32), pltpu.VMEM((tm, D), F32),
            pltpu.VMEM((tm, D), BF16), pltpu.VMEM((tm, D), BF16),
            pltpu.VMEM((tm, D_FF), BF16),
            pltpu.VMEM((N_SLABS, tm, LANES), F32), pltpu.VMEM((N_SLABS, tm, LANES), F32),
            pltpu.VMEM((len(BRANCHES), tm, LANES), F32),
            pltpu.VMEM((tm, D), BF16),
        ],
        compiler_params=_cparams("arbitrary"),
        name="merge_mlp",
    )(x, xs1, *os_, *mls, wo, gmlp, w1, w2, gain_final, *ride_args)
    return (res[0].reshape(B, T, D),) + tuple(res[1:])


def _attn_sample_kernel(q_ref, kvn_ref, c0_ref, c1_ref, c2_ref, bias_ref, o_ref, *, nb):
    caches = (c0_ref, c1_ref, c2_ref)
    for i in range(nb):
        sc_c, sc_n = [], []
        for g in range(len(BRANCHES)):
            q = q_ref[i, g]
            kc = caches[g][i, :, 0]
            sc_c.append(jnp.sum(kc * q[None], axis=-1, keepdims=True) + bias_ref[g, 0:N_KEYS - 1])
            sc_n.append(jnp.sum(kvn_ref[i, g, 0] * q, axis=-1, keepdims=True)
                        + bias_ref[g, N_KEYS - 1])
        m = sc_n[0]
        for g in range(len(BRANCHES)):
            m = jnp.maximum(m, jnp.maximum(jnp.max(sc_c[g], axis=0), sc_n[g]))
        num = jnp.zeros((HEADS, HEAD_DIM), F32)
        den = jnp.zeros((HEADS, HEAD_DIM), F32)
        for g in range(len(BRANCHES)):
            p_c = jnp.exp2(sc_c[g] - m[None])
            p_n = jnp.exp2(sc_n[g] - m)
            den = den + jnp.sum(p_c, axis=0) + p_n
            num = num + jnp.sum(p_c * caches[g][i, :, 1], axis=0) + p_n * kvn_ref[i, g, 1]
        o_ref[i] = num / den


def _attn_sample(q, kvn, caches, bias, nb=4):
    Bs = q.shape[0]
    in_specs = [
        pl.BlockSpec((nb,) + q.shape[1:], lambda b: (b, 0, 0, 0)),
        pl.BlockSpec((nb,) + kvn.shape[1:], lambda b: (b, 0, 0, 0, 0)),
    ]
    for c in caches:
        in_specs.append(pl.BlockSpec((nb, N_KEYS - 1, None, 2, HEADS, HEAD_DIM),
                                     lambda b: (b, 0, 0, 0, 0, 0)))
    in_specs.append(pl.BlockSpec(bias.shape, lambda b: (0, 0, 0, 0)))
    return pl.pallas_call(
        functools.partial(_attn_sample_kernel, nb=nb),
        grid=(Bs // nb,),
        in_specs=in_specs,
        out_specs=pl.BlockSpec((nb, HEADS, HEAD_DIM), lambda b: (b, 0, 0)),
        out_shape=jax.ShapeDtypeStruct((Bs, HEADS, HEAD_DIM), F32),
        compiler_params=_cparams("arbitrary"),
        name="attn_sample",
    )(q, kvn, *caches, bias)


def _out_sample_kernel(x_ref, o_ref, wo_ref, out_ref):
    out_ref[...] = x_ref[...] + jnp.dot(o_ref[...].astype(BF16), wo_ref[...],
                                        preferred_element_type=F32)


def _out_sample(x, o, wo):
    return pl.pallas_call(
        _out_sample_kernel,
        out_shape=jax.ShapeDtypeStruct(x.shape, F32),
        compiler_params=pltpu.CompilerParams(vmem_limit_bytes=VMEM_LIMIT),
        name="out_sample",
    )(x, o, wo)


def _t5_bucket(dist):
    max_exact = NUM_BUCKETS // 2
    df = jnp.maximum(dist, 1).astype(F32)
    large = max_exact + (jnp.log(df / max_exact) / math.log(MAX_DISTANCE / max_exact)
                         * (NUM_BUCKETS - max_exact)).astype(jnp.int32)
    large = jnp.minimum(large, NUM_BUCKETS - 1)
    return jnp.where(dist < max_exact, dist, large)


def _bias_tables(rel_bias):
    out = []
    for g, (w, d) in enumerate(BRANCHES):
        dist = jnp.arange(N_KEYS, dtype=jnp.int32) * d
        out.append(rel_bias[_t5_bucket(dist)][:, g * HEADS:(g + 1) * HEADS])
    return out


def _block_bias(tab):
    n = tab.shape[0]
    pad_lo = jnp.full((n, Q_SUB - 1), MASKED, F32)
    pad_hi = jnp.full((n, Q_SUB), MASKED, F32)
    f = jnp.concatenate([pad_lo, tab[:, ::-1], pad_hi], axis=1)
    width = f.shape[1]
    flat = jnp.tile(f, (1, Q_SUB))[:, :Q_SUB * (width - 1)]
    toep = flat.reshape(n, Q_SUB, width - 1)
    return toep[:, :, Q_SUB - 1:Q_SUB - 1 + 2 * Q_SUB]


def kernel(x_prompt, x_sample, state_pool, cache_kv_w128, cache_kv_w512, cache_kv_w2048,
           norm_mix, pool_w, pool_scale, norm_mlp, mlp_in, mlp_out, norm_kv, w_kv, w_q, w_o,
           rel_bias, norm_final):
    B, T, D = x_prompt.shape
    Bs = x_sample.shape[0]
    n_a = pool_w.shape[0]
    depth = norm_mix.shape[0]
    n_br = len(BRANCHES)
    qk_scale = HEAD_DIM ** -0.5

    pool_w_b = pool_w.astype(BF16)
    w1_b, w2_b = mlp_in[0].astype(BF16), mlp_out[0].astype(BF16)
    ride_proj = n_a >= 2
    extra = {0: [w_kv[None]], 1: [w_q.reshape(1, -1, w_q.shape[-1]), w_o.reshape(1, -1, D)]}
    if not ride_proj:
        w_kv_b, w_q_b, w_o_b = w_kv.astype(BF16)[None], w_q.astype(BF16), w_o.astype(BF16)
    row = lambda v: v.reshape(1, D)
    g_mix = norm_mix.reshape(depth, 1, D)
    g_mlp = norm_mlp.reshape(depth, 1, D)
    g_kv = norm_kv.reshape(1, 1, D)
    g_scale = pool_scale.reshape(n_a, 1, D)

    tabs = jnp.stack(_bias_tables(rel_bias)) * LOG2E
    no_prev = np.arange(2 * Q_SUB)[None, None, None, :] < Q_SUB
    bb = _block_bias(jnp.swapaxes(tabs, 1, 2).reshape(n_br * HEADS, N_KEYS))
    bb = bb.reshape(n_br, HEADS, Q_SUB, 2 * Q_SUB)
    blk_bias = jnp.concatenate([bb, jnp.where(no_prev, MASKED, bb)], axis=1)
    smp_bias = jnp.broadcast_to(tabs[:, ::-1, :, None], (n_br, N_KEYS, HEADS, HEAD_DIM))

    caches = [c.reshape(Bs, N_KEYS - 1, d, 2, HEADS, HEAD_DIM)
              for c, (w, d) in zip((cache_kv_w128, cache_kv_w512, cache_kv_w2048), BRANCHES)]

    xp = x_prompt
    xs = x_sample.reshape(Bs, D)
    pool_p, pool_s, kv_p = [], [], []
    kvs = kvn = None
    for l in range(depth):
        last = l == depth - 1
        riders = [] if last else [(mlp_in, l + 1), (mlp_out, l + 1)]
        if l < n_a:
            if ride_proj and l in extra:
                riders += [(w, 0) for w in extra[l]]
            xs1, hs = _pool_sample(xs, state_pool[l], row(norm_mix[l]), pool_w_b[l], row(pool_scale[l]))
            pool_s.append(jnp.concatenate([state_pool[l][:, 1:], hs[:, None, :]], axis=1))
            res = _pool_mlp(xp, xs1, g_mix, l, pool_w_b, g_scale, g_mlp, w1_b, w2_b,
                            row(norm_final), riders)
            xp, xs, nbuf = res[:3]
            pool_p.append(nbuf)
            cast = res[3:]
            if ride_proj and l == 0:
                w_kv_b = cast[2][None]
            if ride_proj and l == 1:
                w_q_b, w_o_b = cast[2].reshape(w_q.shape), cast[3].reshape(w_o.shape)
        else:
            lb = l - n_a
            if l == n_a:
                tails = tuple(min(w, T) for (w, _) in BRANCHES)
                res = _proj(xp, g_kv, 0, w_kv_b, 0, 2 * D, 256, tails=tails)
                kvs = res[:n_br]
                kv_p = [t.reshape(B, rows, 2, HEADS, HEAD_DIM) for t, rows in zip(res[n_br:], tails)]
                kvn = _proj_sample(xs, g_kv, 0, w_kv_b, 0)
            qs_p = _proj(xp, g_mix, l, w_q_b, lb, D, 1024, scale=qk_scale * LOG2E)
            qs = _proj_sample(xs, g_mix, l, w_q_b, lb, scale=qk_scale * LOG2E)
            os_, lses = [], []
            for g, (w, d) in enumerate(BRANCHES):
                L = T // d
                tq, rb = (L, min(d, ATTN_ROWS // L)) if L < ATTN_ROWS else (ATTN_ROWS, 1)
                o_g, ml_g = _attn_branch(qs_p[g], kvs[g], blk_bias, g, tq, rb)
                os_.append(o_g)
                lses.append(ml_g)
            o_s = _attn_sample(qs.reshape(Bs, n_br, HEADS, HEAD_DIM),
                               kvn.reshape(Bs, n_br, 2, HEADS, HEAD_DIM), caches, smp_bias)
            xs1 = _out_sample(xs, o_s.reshape(Bs, D), w_o_b[lb])
            res = _merge_mlp(xp, xs1, os_, lses, w_o_b, lb, g_mlp, w1_b, w2_b, row(norm_final),
                             l, last, riders)
            xp, xs = res[:2]
            cast = res[2:]
        if not last:
            w1_b, w2_b = cast[:2]

    kv_s = [kvn[:, g * 2 * D:(g + 1) * 2 * D].reshape(Bs, 1, 2, HEADS, HEAD_DIM) for g in range(n_br)]
    return (xp, xs.reshape(Bs, 1, D), jnp.stack(pool_p), jnp.stack(pool_s),
            kv_p[0], kv_s[0], kv_p[1], kv_s[1], kv_p[2], kv_s[2])
```

```python
import functools
import math

import jax
import jax.numpy as jnp
import numpy as np
from jax import lax
from jax.experimental import pallas as pl
from jax.experimental.pallas import tpu as pltpu

F32 = jnp.float32
BF16 = jnp.bfloat16

D_MODEL = 1024
HEADS = 8
HEAD_DIM = 128
D_FF = 4 * D_MODEL
POOL_WINDOWS = (2, 4, 8, 16)
POOL_GROUP = D_MODEL // len(POOL_WINDOWS)
POOL_STATE = max(POOL_WINDOWS) - 1
BRANCHES = ((128, 1), (512, 4), (2048, 16))
N_KEYS = 129
NUM_BUCKETS = 32
MAX_DISTANCE = 2048
SAMPLE_POS0 = 8192
EPS = 1e-6
MASKED = -1e30
LOG2E = math.log2(math.e)

LANES = 128
N_SLABS = D_MODEL // LANES
Q_SUB = 128
ATTN_ROWS = 2048
VMEM_LIMIT = 60 * 1024 * 1024


def _cparams(*sem):
    return pltpu.CompilerParams(dimension_semantics=sem, vmem_limit_bytes=VMEM_LIMIT)


def _rms(x, g):
    ms = jnp.mean(x * x, axis=-1, keepdims=True)
    return x * lax.rsqrt(ms + EPS) * g


POOL_BASE = 24
POOL_LO = 8


def _pool_pieces(x_ref, g_ref, w_ref, sc_ref, nb_ref, hbuf, sbuf, x1_ref, t, tm):
    end = POOL_BASE + tm

    def norm():
        hbuf[POOL_BASE:end, :] = _rms(x_ref[0], g_ref[...])

    def group(gi, w):
        sl = slice(gi * POOL_GROUP, (gi + 1) * POOL_GROUP)
        n_stage = w.bit_length() - 1
        s = None
        for si in range(n_stage):
            sh = 1 << si
            lo = POOL_BASE if si == n_stage - 1 else POOL_LO
            if si == 0:
                val = hbuf[lo:end, sl] + hbuf[lo - sh:end - sh, sl]
            else:
                val = sbuf[si - 1, lo:end, :] + sbuf[si - 1, lo - sh:end - sh, :]
            if si == n_stage - 1:
                s = val
            else:
                sbuf[si, lo:end, :] = val
        pos = t * tm + lax.broadcasted_iota(jnp.int32, (tm, 1), 0)
        cnt = jnp.minimum(pos + 1, w).astype(F32)
        pooled = s / cnt - hbuf[POOL_BASE:end, sl]
        y = jnp.dot(pooled.astype(BF16), w_ref[gi], preferred_element_type=F32)
        x1_ref[:, sl] = x_ref[0, :, sl] + y * sc_ref[:, sl]

    def carry():
        nb_ref[0] = hbuf[end - POOL_STATE:end, :]
        hbuf[POOL_BASE - 16:POOL_BASE, :] = hbuf[end - 16:end, :]

    pieces = [norm]
    pieces += [functools.partial(group, gi, w) for gi, w in enumerate(POOL_WINDOWS)]
    return pieces + [carry]


def _mlp_tile(x1_ref, hs_ref, w1_ref, w2_ref, gf_ref, o_ref, us, tc, final_norm, between=()):
    between = list(between)
    n_chunks = D_FF // tc
    assert len(between) <= n_chunks
    for c in range(n_chunks):
        if c < len(between):
            between[c]()
        u = jnp.dot(hs_ref[...], w1_ref[:, c * tc:(c + 1) * tc], preferred_element_type=F32)
        us[:, c * tc:(c + 1) * tc] = jnp.square(jnp.maximum(u, 0.0)).astype(BF16)
    y = x1_ref[...] + jnp.dot(us[...], w2_ref[...], preferred_element_type=F32)
    if final_norm:
        y = _rms(y, gf_ref[...])
    o_ref[...] = y


def _run_pipelined(slots, xs1_ref, gmlp_ref, o_ref, os_ref, run):
    (x1a, hsa), (x1b, hsb) = slots
    j = pl.program_id(0)
    bs = xs1_ref.shape[0]

    @pl.when(j == 0)
    def _():
        x1b[...] = jnp.zeros(x1b.shape, F32)
        hsb[...] = jnp.zeros(hsb.shape, BF16)
        xs1 = xs1_ref[...]
        x1b[0:bs, :] = xs1
        hsb[0:bs, :] = _rms(xs1, gmlp_ref[...]).astype(BF16)

    @pl.when(j % 2 == 0)
    def _():
        run(x1a, hsa, x1b, hsb)

    @pl.when(j % 2 == 1)
    def _():
        run(x1b, hsb, x1a, hsa)

    @pl.when(j == 0)
    def _():
        os_ref[...] = o_ref[0:bs, :]


def _cast_rider(f32_refs, bf16_refs):
    def cast():
        for src, dst in zip(f32_refs, bf16_refs):
            dst[...] = src[...].astype(BF16)
    return cast


def _rider_specs(riders, n_tiles):
    chunk = lambda j: jnp.minimum(j, n_tiles - 1)
    args, in_specs, out_specs, out_shape = [], [], [], []
    for w, ln in riders:
        rows = w.shape[1] // n_tiles
        args.append(w)
        in_specs.append(pl.BlockSpec((None, rows, w.shape[2]), lambda j, ln=ln: (ln, chunk(j), 0)))
        out_specs.append(pl.BlockSpec((rows, w.shape[2]), lambda j: (chunk(j), 0)))
        out_shape.append(jax.ShapeDtypeStruct(w.shape[1:], BF16))
    return args, in_specs, out_specs, out_shape


def _pool_mlp_kernel(x_ref, xs1_ref, gmix_ref, wp_ref, sc_ref, gmlp_ref, w1_ref, w2_ref, gf_ref,
                     *refs, tm, nt, n_tiles, tc, ride):
    f32_refs, refs = refs[:ride], refs[ride:]
    o_ref, os_ref, nb_ref = refs[:3]
    bf16_refs, refs = refs[3:3 + ride], refs[3 + ride:]
    x1a, x1b, hsa, hsb, us, hbuf, sbuf = refs
    cast = _cast_rider(f32_refs, bf16_refs)
    j = pl.program_id(0)
    t = jnp.minimum(j, n_tiles - 1) % nt

    @pl.when(j == 0)
    def _():
        hbuf[0:POOL_BASE, :] = jnp.zeros((POOL_BASE, D_MODEL), F32)
        sbuf[...] = jnp.zeros(sbuf.shape, F32)

    @pl.when(t == 0)
    def _():
        hbuf[POOL_BASE - 16:POOL_BASE, :] = jnp.zeros((16, D_MODEL), F32)

    def run(x1_w, hs_w, x1_r, hs_r):
        def next_norm():
            hs_w[...] = _rms(x1_w[...], gmlp_ref[...]).astype(BF16)

        pieces = _pool_pieces(x_ref, gmix_ref, wp_ref, sc_ref, nb_ref, hbuf, sbuf, x1_w, t, tm)
        _mlp_tile(x1_r, hs_r, w1_ref, w2_ref, gf_ref, o_ref, us, tc, False,
                  between=pieces + [next_norm, cast])

    _run_pipelined(((x1a, hsa), (x1b, hsb)), xs1_ref, gmlp_ref, o_ref, os_ref, run)


def _pool_mlp(x, xs1, gmix, li, w_pool, scale, gmlp, w1, w2, gain_final, riders, tm=512, tc=512):
    B, T, D = x.shape
    Bs = xs1.shape[0]
    nt = T // tm
    n_tiles = B * nt

    def tile(j):
        jj = jnp.minimum(j, n_tiles - 1)
        return jj // nt, jj % nt

    one = pl.Buffered(1)
    ride_args, ride_in, ride_out, ride_shape = _rider_specs(riders, n_tiles)
    res = pl.pallas_call(
        functools.partial(_pool_mlp_kernel, tm=tm, nt=nt, n_tiles=n_tiles, tc=tc,
                          ride=len(riders)),
        grid=(n_tiles + 1,),
        in_specs=[
            pl.BlockSpec((1, tm, D), lambda j: (*tile(j), 0)),
            pl.BlockSpec((Bs, D), lambda j: (0, 0)),
            pl.BlockSpec((None, 1, D), lambda j: (li, 0, 0)),
            pl.BlockSpec((None,) + w_pool.shape[1:], lambda j: (li, 0, 0, 0)),
            pl.BlockSpec((None, 1, D), lambda j: (li, 0, 0)),
            pl.BlockSpec((None, 1, D), lambda j: (li, 0, 0)),
            pl.BlockSpec((D, D_FF), lambda j: (0, 0), pipeline_mode=one),
            pl.BlockSpec((D_FF, D), lambda j: (0, 0), pipeline_mode=one),
            pl.BlockSpec((1, D), lambda j: (0, 0)),
        ] + ride_in,
        out_specs=[
            pl.BlockSpec((tm, D), lambda j: (jnp.maximum(j - 1, 0), 0)),
            pl.BlockSpec((Bs, D), lambda j: (0, 0)),
            pl.BlockSpec((1, POOL_STATE, D), lambda j: (tile(j)[0], 0, 0)),
        ] + ride_out,
        out_shape=[
            jax.ShapeDtypeStruct((B * T, D), F32),
            jax.ShapeDtypeStruct((Bs, D), F32),
            jax.ShapeDtypeStruct((B, POOL_STATE, D), F32),
        ] + ride_shape,
        scratch_shapes=[
            pltpu.VMEM((tm, D), F32), pltpu.VMEM((tm, D), F32),
            pltpu.VMEM((tm, D), BF16), pltpu.VMEM((tm, D), BF16),
            pltpu.VMEM((tm, D_FF), BF16),
            pltpu.VMEM((POOL_BASE + tm, D), F32),
            pltpu.VMEM((len(POOL_WINDOWS) - 1, POOL_BASE + tm, POOL_GROUP), F32),
        ],
        compiler_params=_cparams("arbitrary"),
        name="pool_mlp",
    )(x, xs1, gmix, w_pool, scale, gmlp, w1, w2, gain_final, *ride_args)
    return (res[0].reshape(B, T, D),) + tuple(res[1:])


def _pool_sample_kernel(x_ref, st_ref, g_ref, w_ref, sc_ref, o_ref, h_ref):
    x = x_ref[...]
    h = _rms(x, g_ref[...])
    h_ref[...] = h
    for gi, w in enumerate(POOL_WINDOWS):
        sl = slice(gi * POOL_GROUP, (gi + 1) * POOL_GROUP)
        hg = h[:, sl]
        s = hg
        for j in range(1, w):
            s = s + st_ref[:, POOL_STATE - j, sl]
        pooled = s / float(min(SAMPLE_POS0 + 1, w)) - hg
        y = jnp.dot(pooled.astype(BF16), w_ref[gi], preferred_element_type=F32)
        o_ref[:, sl] = x[:, sl] + y * sc_ref[:, sl]


def _pool_sample(x, state, gain, w_pool, scale):
    Bs, D = x.shape
    return pl.pallas_call(
        _pool_sample_kernel,
        out_shape=[jax.ShapeDtypeStruct((Bs, D), F32), jax.ShapeDtypeStruct((Bs, D), F32)],
        compiler_params=pltpu.CompilerParams(vmem_limit_bytes=VMEM_LIMIT),
        name="pool_sample",
    )(x, state, gain, w_pool, scale)


def _proj_kernel(x_ref, g_ref, w_ref, *refs, tm, nc, scale, tails):
    n_br = len(BRANCHES)
    o_refs = refs[:n_br]
    if tails:
        tail_refs = refs[n_br:2 * n_br]
        hsl, stage, tsl = refs[2 * n_br:2 * n_br + 3]
        lhs = refs[2 * n_br + 3:]
        i = pl.program_id(1)
        nt = pl.num_programs(1)
    else:
        hsl, stage = refs[n_br:n_br + 2]
        lhs = refs[n_br + 2:]
    h = _rms(x_ref[0], g_ref[...])
    for c in range(N_SLABS):
        hsl[c] = h[:, c * LANES:(c + 1) * LANES]
    prev = None
    for g, (_, dil) in enumerate(BRANCHES):
        n = tm // dil
        if dil == 1:
            lhs[g][...] = h.astype(BF16)
        else:
            src, p = prev if prev is not None and dil % prev[1] == 0 else (hsl, 1)
            s = dil // p
            keep = any(d2 > dil and d2 % dil == 0 for (_, d2) in BRANCHES[g + 1:])
            for rl in range(p):
                for rh in range(s):
                    r = p * rh + rl
                    for c in range(N_SLABS):
                        v = src[c, pl.ds(rl * (tm // p) + rh, n, stride=s), :]
                        if keep:
                            stage[c, r * n:(r + 1) * n, :] = v
                        lhs[g][r * n:(r + 1) * n, c * LANES:(c + 1) * LANES] = v.astype(BF16)
            if keep:
                prev = (stage, dil)
        res = jnp.dot(lhs[g][...], w_ref[:, g * nc:(g + 1) * nc], preferred_element_type=F32)
        if scale != 1.0:
            res = res * scale
        for r in range(dil):
            o_refs[g][0, r] = res[r * n:(r + 1) * n].astype(o_refs[g].dtype)

        if tails:
            tail_ref, rows = tail_refs[g], tails[g]
            if rows < tm:
                @pl.when(i == nt - 1)
                def _(tail_ref=tail_ref, rows=rows, res=res):
                    tail_ref[0] = res[tm - rows:tm]
            else:
                @pl.when(i >= nt - rows // tm)
                def _(tail_ref=tail_ref, res=res, dil=dil, n=n, g=g):
                    if dil == 1:
                        tail_ref[0] = res
                    else:
                        p = max([d2 for (_, d2) in BRANCHES if 1 < d2 < dil and dil % d2 == 0],
                                default=1)
                        s = dil // p
                        if p > 1:
                            assert g == n_br - 1 and nc // LANES <= 2 * N_SLABS
                        for c in range(nc // LANES):
                            cs = slice(c * LANES, (c + 1) * LANES)
                            if p == 1:
                                for r in range(dil):
                                    tsl[c, pl.ds(r, n, stride=dil), :] = res[r * n:(r + 1) * n, cs]
                            else:
                                tmp, ct = (hsl, c) if c < N_SLABS else (stage, c - N_SLABS)
                                for rl in range(p):
                                    for rh in range(s):
                                        r = p * rh + rl
                                        tmp[ct, pl.ds(rl * (tm // p) + rh, n, stride=s), :] = (
                                            res[r * n:(r + 1) * n, cs])
                                for rl in range(p):
                                    tsl[c, pl.ds(rl, tm // p, stride=p), :] = (
                                        tmp[ct, rl * (tm // p):(rl + 1) * (tm // p), :])
                        for c in range(nc // LANES):
                            tail_ref[0, :, c * LANES:(c + 1) * LANES] = tsl[c]


def _proj(x, gains, gi, w, wi, nc, tm, scale=1.0, tails=()):
    B, T, D = x.shape
    nt = T // tm
    out_specs, out_shape = [], []
    for (_, dil) in BRANCHES:
        out_specs.append(pl.BlockSpec((1, dil, tm // dil, nc), lambda b, i: (b, 0, i, 0)))
        out_shape.append(jax.ShapeDtypeStruct((B, dil, T // dil, nc), BF16))
    scratch = [pltpu.VMEM((N_SLABS, tm, LANES), F32), pltpu.VMEM((N_SLABS, tm, LANES), F32)]
    if tails:
        for rows, (_, dil) in zip(tails, BRANCHES):
            assert rows >= tm or dil == 1
            if rows < tm:
                out_specs.append(pl.BlockSpec((1, rows, nc), lambda b, i: (b, 0, 0)))
            else:
                first = nt - rows // tm
                out_specs.append(pl.BlockSpec(
                    (1, tm, nc), lambda b, i, first=first: (b, jnp.maximum(i - first, 0), 0)))
            out_shape.append(jax.ShapeDtypeStruct((B, rows, nc), F32))
        scratch.append(pltpu.VMEM((nc // LANES, tm, LANES), F32))
    scratch += [pltpu.VMEM((tm, D), BF16) for _ in BRANCHES]
    return pl.pallas_call(
        functools.partial(_proj_kernel, tm=tm, nc=nc, scale=scale, tails=tuple(tails)),
        grid=(B, nt),
        in_specs=[
            pl.BlockSpec((1, tm, D), lambda b, i: (b, i, 0)),
            pl.BlockSpec((None, 1, D), lambda b, i: (gi, 0, 0)),
            pl.BlockSpec((None, D, len(BRANCHES) * nc), lambda b, i: (wi, 0, 0),
                         pipeline_mode=pl.Buffered(1)),
        ],
        out_specs=out_specs,
        out_shape=out_shape,
        scratch_shapes=scratch,
        compiler_params=_cparams("arbitrary", "arbitrary"),
        name="proj_kv" if tails else "proj_q",
    )(x, gains, w)


def _proj_sample_kernel(x_ref, g_ref, w_ref, o_ref, *, scale):
    h = _rms(x_ref[...], g_ref[...]).astype(BF16)
    res = jnp.dot(h, w_ref[...], preferred_element_type=F32)
    if scale != 1.0:
        res = res * scale
    o_ref[...] = res


def _proj_sample(x, gains, gi, w, wi, scale=1.0, tn=3072):
    Bs, D = x.shape
    nc = w.shape[2]
    return pl.pallas_call(
        functools.partial(_proj_sample_kernel, scale=scale),
        grid=(nc // tn,),
        in_specs=[
            pl.BlockSpec((Bs, D), lambda j: (0, 0)),
            pl.BlockSpec((None, 1, D), lambda j: (gi, 0, 0)),
            pl.BlockSpec((None, D, tn), lambda j: (wi, 0, j)),
        ],
        out_specs=pl.BlockSpec((Bs, tn), lambda j: (0, j)),
        out_shape=jax.ShapeDtypeStruct((Bs, nc), F32),
        compiler_params=_cparams("arbitrary"),
        name="proj_sample",
    )(x, gains, w)


KV_SLOTS = 3


def _attn_kernel(q_ref, kv_hbm, bias_ref, o_ref, ml_ref, kprev, vprev, kvbuf, sem, *, tq, rb, grid):
    n0, n1, n2 = grid
    total = n0 * n1 * n2
    step = (pl.program_id(0) * n1 + pl.program_id(1)) * n2 + pl.program_id(2)

    def kv_copy(t, slot):
        rem = t % (n1 * n2)
        row0 = pl.multiple_of((rem % n2) * tq, tq)
        src = kv_hbm.at[t // (n1 * n2), pl.ds((rem // n2) * rb, rb), pl.ds(row0, tq), :]
        return pltpu.make_async_copy(src, kvbuf.at[slot], sem.at[slot])

    @pl.when(step == 0)
    def _():
        for t in range(min(KV_SLOTS - 1, total)):
            kv_copy(t, t).start()

    @pl.when(step + KV_SLOTS - 1 < total)
    def _():
        kv_copy(step + KV_SLOTS - 1, (step + KV_SLOTS - 1) % KV_SLOTS).start()

    slot = step % KV_SLOTS
    kv_copy(step, slot).wait()
    kv_ref = kvbuf.at[slot]
    first = pl.program_id(2) == 0
    lane = lax.broadcasted_iota(jnp.int32, (Q_SUB, LANES), 1)
    ones = jnp.ones((2 * Q_SUB, HEAD_DIM), BF16)
    if rb == 1:
        @pl.when(first)
        def _():
            kprev[...] = jnp.zeros(kprev.shape, BF16)
            vprev[...] = jnp.zeros(vprev.shape, BF16)
    for rr in range(rb):
        for s in range(tq // Q_SUB):
            r0 = s * Q_SUB

            def keys(h, col0, s=s, r0=r0, rr=rr):
                cs = slice(col0 + h * HEAD_DIM, col0 + (h + 1) * HEAD_DIM)
                if s > 0:
                    return kv_ref[rr, r0 - Q_SUB:r0 + Q_SUB, cs]
                cur = kv_ref[rr, 0:Q_SUB, cs]
                if rb == 1:
                    prev = (kprev if col0 == 0 else vprev)[:, h * HEAD_DIM:(h + 1) * HEAD_DIM]
                else:
                    prev = jnp.zeros((Q_SUB, HEAD_DIM), BF16)
                return jnp.concatenate([prev, cur], axis=0)

            if s > 0:
                boff = 0
            elif rb > 1:
                boff = HEADS
            else:
                boff = jnp.where(first, HEADS, 0)
            ml_tile = jnp.zeros((Q_SUB, LANES), F32)
            for h in range(HEADS):
                cs = slice(h * HEAD_DIM, (h + 1) * HEAD_DIM)
                q = q_ref[0, rr, r0:r0 + Q_SUB, cs]
                sc = lax.dot_general(q, keys(h, 0), (((1,), (1,)), ((), ())),
                                     preferred_element_type=F32)
                sc = sc + bias_ref[boff + h]
                m = jnp.max(sc, axis=-1, keepdims=True)
                p = jnp.exp2(sc - m).astype(BF16)
                v_ext = jnp.concatenate([keys(h, D_MODEL), ones], axis=1)
                o = jnp.dot(p, v_ext, preferred_element_type=F32)
                o_ref[0, rr, r0:r0 + Q_SUB, cs] = o[:, :HEAD_DIM].astype(o_ref.dtype)
                ml_tile = jnp.where(lane == h, m, ml_tile)
                ml_tile = jnp.where(lane == HEADS + h, o[:, HEAD_DIM:], ml_tile)
            ml_ref[0, rr, r0:r0 + Q_SUB, :] = ml_tile
    if rb == 1:
        kprev[...] = kv_ref[0, tq - Q_SUB:tq, 0:D_MODEL]
        vprev[...] = kv_ref[0, tq - Q_SUB:tq, D_MODEL:2 * D_MODEL]


def _attn_branch(q, kv, bias, tq, rb):
    B, dil, L, D = q.shape
    assert rb == 1 or tq == L
    grid = (B, dil // rb, L // tq)
    return pl.pallas_call(
        functools.partial(_attn_kernel, tq=tq, rb=rb, grid=grid),
        grid=grid,
        in_specs=[
            pl.BlockSpec((1, rb, tq, D), lambda b, r, i: (b, r, i, 0)),
            pl.BlockSpec(memory_space=pl.ANY),
            pl.BlockSpec(bias.shape, lambda b, r, i: (0, 0, 0)),
        ],
        out_specs=[
            pl.BlockSpec((1, rb, tq, D), lambda b, r, i: (b, r, i, 0)),
            pl.BlockSpec((1, rb, tq, LANES), lambda b, r, i: (b, r, i, 0)),
        ],
        out_shape=[
            jax.ShapeDtypeStruct((B, dil, L, D), BF16),
            jax.ShapeDtypeStruct((B, dil, L, LANES), F32),
        ],
        scratch_shapes=[pltpu.VMEM((Q_SUB, D), BF16), pltpu.VMEM((Q_SUB, D), BF16),
                        pltpu.VMEM((KV_SLOTS, rb, tq, 2 * D), BF16),
                        pltpu.SemaphoreType.DMA((KV_SLOTS,))],
        compiler_params=_cparams("arbitrary", "arbitrary", "arbitrary"),
        name=f"attn_d{dil}",
    )(q, kv, bias)


def _merge_pieces(x_ref, o_refs, ml_refs, wo_ref, osls, wsl, lhs, x1_ref, tm):
    def weights():
        tiles = []
        for g, (_, dil) in enumerate(BRANCHES):
            if dil == 1:
                tiles.append(ml_refs[g][0, 0])
            else:
                n = tm // dil
                for r in range(dil):
                    wsl[g, pl.ds(r, n, stride=dil), :] = ml_refs[g][0, r]
                tiles.append(wsl[g])
        m = jnp.maximum(jnp.maximum(tiles[0], tiles[1]), tiles[2])
        us = [jnp.exp2(tl - m) for tl in tiles]
        ls = [pltpu.roll(tl, LANES - HEADS, axis=1) for tl in tiles]
        den = us[0] * ls[0] + us[1] * ls[1] + us[2] * ls[2]
        for g in range(len(BRANCHES)):
            wsl[g] = us[g] / den

    def interleave(g, r_lo, r_hi):
        dil = BRANCHES[g][1]
        n = tm // dil
        for r in range(r_lo, r_hi):
            for c in range(N_SLABS):
                osls[g][c, pl.ds(r, n, stride=dil), :] = (
                    o_refs[g][0, r, :, c * LANES:(c + 1) * LANES].astype(F32))

    def interleave_by(g, gp, r_lo, r_hi):
        dil, p = BRANCHES[g][1], BRANCHES[gp][1]
        n, s = tm // dil, dil // p
        for r in range(r_lo, r_hi):
            rl, rh = r % p, r // p
            for c in range(N_SLABS):
                osls[gp][c, pl.ds(rl * (tm // p) + rh, n, stride=s), :] = (
                    o_refs[g][0, r, :, c * LANES:(c + 1) * LANES].astype(F32))

    def interleave_finish(g, gp):
        p = BRANCHES[gp][1]
        for rl in range(p):
            for c in range(N_SLABS):
                osls[g][c, pl.ds(rl, tm // p, stride=p), :] = (
                    osls[gp][c, rl * (tm // p):(rl + 1) * (tm // p), :])

    def combine(c_lo, c_hi):
        for c in range(c_lo, c_hi):
            cs = slice(c * LANES, (c + 1) * LANES)
            merged = wsl[0, :, c:c + 1] * o_refs[0][0, 0, :, cs].astype(F32)
            for g in range(1, len(BRANCHES)):
                merged = merged + wsl[g, :, c:c + 1] * osls[g][c]
            lhs[:, cs] = merged.astype(BF16)

    def project():
        x1_ref[...] = x_ref[0] + jnp.dot(lhs[...], wo_ref[...], preferred_element_type=F32)

    pieces = [weights]
    order = sorted((g for g, (_, dil) in enumerate(BRANCHES) if dil > 1),
                   key=lambda g: -BRANCHES[g][1])
    done = []
    for g in order:
        dil = BRANCHES[g][1]
        gp = next((g2 for g2 in order if g2 not in done and g2 != g
                   and 1 < BRANCHES[g2][1] < dil and dil % BRANCHES[g2][1] == 0), None)
        if gp is None:
            pieces.append(functools.partial(interleave, g, 0, dil))
        else:
            pieces += [functools.partial(interleave_by, g, gp, 0, dil // 2),
                       functools.partial(interleave_by, g, gp, dil // 2, dil),
                       functools.partial(interleave_finish, g, gp)]
        done.append(g)
    tail = pieces[-2:]
    pieces = pieces[:-2] + [lambda: [f() for f in tail]]
    pieces += [functools.partial(combine, 0, N_SLABS // 2),
               functools.partial(combine, N_SLABS // 2, N_SLABS), project]
    return pieces


def _merge_mlp_kernel(x_ref, xs1_ref, o0_ref, o1_ref, o2_ref, ml0_ref, ml1_ref, ml2_ref, wo_ref,
                      gmlp_ref, w1_ref, w2_ref, gf_ref, *refs, tm, tc, final_norm, ride):
    f32_refs, refs = refs[:ride], refs[ride:]
    out_ref, os_ref = refs[:2]
    bf16_refs, refs = refs[2:2 + ride], refs[2 + ride:]
    x1a, x1b, hsa, hsb, us, osl1, osl2, wsl, lhs = refs
    cast = _cast_rider(f32_refs, bf16_refs)

    def run(x1_w, hs_w, x1_r, hs_r):
        def next_norm():
            hs_w[...] = _rms(x1_w[...], gmlp_ref[...]).astype(BF16)
            cast()

        pieces = _merge_pieces(x_ref, (o0_ref, o1_ref, o2_ref), (ml0_ref, ml1_ref, ml2_ref),
                               wo_ref, (None, osl1, osl2), wsl, lhs, x1_w, tm)
        _mlp_tile(x1_r, hs_r, w1_ref, w2_ref, gf_ref, out_ref, us, tc, final_norm,
                  between=pieces + [next_norm])

    _run_pipelined(((x1a, hsa), (x1b, hsb)), xs1_ref, gmlp_ref, out_ref, os_ref, run)


def _merge_mlp(x, xs1, os_, mls, wo, wi, gmlp, w1, w2, gain_final, li, final_norm, riders,
               tm=512, tc=512):
    B, T, D = x.shape
    Bs = xs1.shape[0]
    nt = T // tm
    n_tiles = B * nt

    def tile(j):
        jj = jnp.minimum(j, n_tiles - 1)
        return jj // nt, jj % nt

    one = pl.Buffered(1)
    in_specs = [pl.BlockSpec((1, tm, D), lambda j: (*tile(j), 0)),
                pl.BlockSpec((Bs, D), lambda j: (0, 0))]
    for (_, dil) in BRANCHES:
        in_specs.append(pl.BlockSpec((1, dil, tm // dil, D), lambda j: (tile(j)[0], 0, tile(j)[1], 0)))
    for (_, dil) in BRANCHES:
        in_specs.append(pl.BlockSpec((1, dil, tm // dil, LANES),
                                     lambda j: (tile(j)[0], 0, tile(j)[1], 0)))
    in_specs += [
        pl.BlockSpec((None, D, D), lambda j: (wi, 0, 0), pipeline_mode=one),
        pl.BlockSpec((None, 1, D), lambda j: (li, 0, 0)),
        pl.BlockSpec((D, D_FF), lambda j: (0, 0), pipeline_mode=one),
        pl.BlockSpec((D_FF, D), lambda j: (0, 0), pipeline_mode=one),
        pl.BlockSpec((1, D), lambda j: (0, 0)),
    ]
    ride_args, ride_in, ride_out, ride_shape = _rider_specs(riders, n_tiles)
    res = pl.pallas_call(
        functools.partial(_merge_mlp_kernel, tm=tm, tc=tc, final_norm=final_norm,
                          ride=len(riders)),
        grid=(n_tiles + 1,),
        in_specs=in_specs + ride_in,
        out_specs=[pl.BlockSpec((tm, D), lambda j: (jnp.maximum(j - 1, 0), 0)),
                   pl.BlockSpec((Bs, D), lambda j: (0, 0))] + ride_out,
        out_shape=[jax.ShapeDtypeStruct((B * T, D), F32),
                   jax.ShapeDtypeStruct((Bs, D), F32)] + ride_shape,
        scratch_shapes=[
            pltpu.VMEM((tm, D), F32), pltpu.VMEM((tm, D), F32),
            pltpu.VMEM((tm, D), BF16), pltpu.VMEM((tm, D), BF16),
            pltpu.VMEM((tm, D_FF), BF16),
            pltpu.VMEM((N_SLABS, tm, LANES), F32), pltpu.VMEM((N_SLABS, tm, LANES), F32),
            pltpu.VMEM((len(BRANCHES), tm, LANES), F32),
            pltpu.VMEM((tm, D), BF16),
        ],
        compiler_params=_cparams("arbitrary"),
        name="merge_mlp",
    )(x, xs1, *os_, *mls, wo, gmlp, w1, w2, gain_final, *ride_args)
    return (res[0].reshape(B, T, D),) + tuple(res[1:])


def _attn_sample_kernel(q_ref, kvn_ref, c0_ref, c1_ref, c2_ref, bias_ref, o_ref, *, nb):
    caches = (c0_ref, c1_ref, c2_ref)
    for i in range(nb):
        sc_c, sc_n = [], []
        for g in range(len(BRANCHES)):
            q = q_ref[i, g]
            kc = caches[g][i, :, 0]
            sc_c.append(jnp.sum(kc * q[None], axis=-1, keepdims=True) + bias_ref[g, 0:N_KEYS - 1])
            sc_n.append(jnp.sum(kvn_ref[i, g, 0] * q, axis=-1, keepdims=True)
                        + bias_ref[g, N_KEYS - 1])
        m = sc_n[0]
        for g in range(len(BRANCHES)):
            m = jnp.maximum(m, jnp.maximum(jnp.max(sc_c[g], axis=0), sc_n[g]))
        num = jnp.zeros((HEADS, HEAD_DIM), F32)
        den = jnp.zeros((HEADS, HEAD_DIM), F32)
        for g in range(len(BRANCHES)):
            p_c = jnp.exp2(sc_c[g] - m[None])
            p_n = jnp.exp2(sc_n[g] - m)
            den = den + jnp.sum(p_c, axis=0) + p_n
            num = num + jnp.sum(p_c * caches[g][i, :, 1], axis=0) + p_n * kvn_ref[i, g, 1]
        o_ref[i] = num / den


def _attn_sample(q, kvn, caches, bias, nb=4):
    Bs = q.shape[0]
    in_specs = [
        pl.BlockSpec((nb,) + q.shape[1:], lambda b: (b, 0, 0, 0)),
        pl.BlockSpec((nb,) + kvn.shape[1:], lambda b: (b, 0, 0, 0, 0)),
    ]
    for c in caches:
        in_specs.append(pl.BlockSpec((nb, N_KEYS - 1, None, 2, HEADS, HEAD_DIM),
                                     lambda b: (b, 0, 0, 0, 0, 0)))
    in_specs.append(pl.BlockSpec(bias.shape, lambda b: (0, 0, 0, 0)))
    return pl.pallas_call(
        functools.partial(_attn_sample_kernel, nb=nb),
        grid=(Bs // nb,),
        in_specs=in_specs,
        out_specs=pl.BlockSpec((nb, HEADS, HEAD_DIM), lambda b: (b, 0, 0)),
        out_shape=jax.ShapeDtypeStruct((Bs, HEADS, HEAD_DIM), F32),
        compiler_params=_cparams("arbitrary"),
        name="attn_sample",
    )(q, kvn, *caches, bias)


def _out_sample_kernel(x_ref, o_ref, wo_ref, out_ref):
    out_ref[...] = x_ref[...] + jnp.dot(o_ref[...].astype(BF16), wo_ref[...],
                                        preferred_element_type=F32)


def _out_sample(x, o, wo):
    return pl.pallas_call(
        _out_sample_kernel,
        out_shape=jax.ShapeDtypeStruct(x.shape, F32),
        compiler_params=pltpu.CompilerParams(vmem_limit_bytes=VMEM_LIMIT),
        name="out_sample",
    )(x, o, wo)


def _t5_bucket(dist):
    max_exact = NUM_BUCKETS // 2
    df = jnp.maximum(dist, 1).astype(F32)
    large = max_exact + (jnp.log(df / max_exact) / math.log(MAX_DISTANCE / max_exact)
                         * (NUM_BUCKETS - max_exact)).astype(jnp.int32)
    large = jnp.minimum(large, NUM_BUCKETS - 1)
    return jnp.where(dist < max_exact, dist, large)


def _bias_tables(rel_bias):
    out = []
    for g, (w, d) in enumerate(BRANCHES):
        dist = jnp.arange(N_KEYS, dtype=jnp.int32) * d
        out.append(rel_bias[_t5_bucket(dist)][:, g * HEADS:(g + 1) * HEADS])
    return out


def _block_bias(tab):
    pad_lo = jnp.full((HEADS, Q_SUB - 1), MASKED, F32)
    pad_hi = jnp.full((HEADS, Q_SUB), MASKED, F32)
    f = jnp.concatenate([pad_lo, tab[::-1].T, pad_hi], axis=1)
    width = f.shape[1]
    flat = jnp.tile(f, (1, Q_SUB))[:, :Q_SUB * (width - 1)]
    toep = flat.reshape(HEADS, Q_SUB, width - 1)
    return toep[:, :, Q_SUB - 1:Q_SUB - 1 + 2 * Q_SUB]


def kernel(x_prompt, x_sample, state_pool, cache_kv_w128, cache_kv_w512, cache_kv_w2048,
           norm_mix, pool_w, pool_scale, norm_mlp, mlp_in, mlp_out, norm_kv, w_kv, w_q, w_o,
           rel_bias, norm_final):
    B, T, D = x_prompt.shape
    Bs = x_sample.shape[0]
    n_a = pool_w.shape[0]
    depth = norm_mix.shape[0]
    n_br = len(BRANCHES)
    qk_scale = HEAD_DIM ** -0.5

    pool_w_b = pool_w.astype(BF16)
    w1_b, w2_b = mlp_in[0].astype(BF16), mlp_out[0].astype(BF16)
    ride_proj = n_a >= 2
    extra = {0: [w_kv[None]], 1: [w_q.reshape(1, -1, w_q.shape[-1]), w_o.reshape(1, -1, D)]}
    if not ride_proj:
        w_kv_b, w_q_b, w_o_b = w_kv.astype(BF16)[None], w_q.astype(BF16), w_o.astype(BF16)
    row = lambda v: v.reshape(1, D)
    g_mix = norm_mix.reshape(depth, 1, D)
    g_mlp = norm_mlp.reshape(depth, 1, D)
    g_kv = norm_kv.reshape(1, 1, D)
    g_scale = pool_scale.reshape(n_a, 1, D)

    tabs = _bias_tables(rel_bias)
    no_prev = np.arange(2 * Q_SUB)[None, None, :] < Q_SUB
    blk_bias = []
    for t in tabs:
        bb = _block_bias(t * LOG2E)
        blk_bias.append(jnp.concatenate([bb, jnp.where(no_prev, MASKED, bb)], axis=0))
    smp_bias = jnp.stack([jnp.broadcast_to((t * LOG2E)[::-1][:, :, None], (N_KEYS, HEADS, HEAD_DIM))
                          for t in tabs])

    caches = [c.reshape(Bs, N_KEYS - 1, d, 2, HEADS, HEAD_DIM)
              for c, (w, d) in zip((cache_kv_w128, cache_kv_w512, cache_kv_w2048), BRANCHES)]

    xp = x_prompt
    xs = x_sample.reshape(Bs, D)
    pool_p, pool_s, kv_p = [], [], []
    kvs = kvn = None
    for l in range(depth):
        last = l == depth - 1
        riders = [] if last else [(mlp_in, l + 1), (mlp_out, l + 1)]
        if l < n_a:
            if ride_proj and l in extra:
                riders += [(w, 0) for w in extra[l]]
            xs1, hs = _pool_sample(xs, state_pool[l], row(norm_mix[l]), pool_w_b[l], row(pool_scale[l]))
            pool_s.append(jnp.concatenate([state_pool[l][:, 1:], hs[:, None, :]], axis=1))
            res = _pool_mlp(xp, xs1, g_mix, l, pool_w_b, g_scale, g_mlp, w1_b, w2_b,
                            row(norm_final), riders)
            xp, xs, nbuf = res[:3]
            pool_p.append(nbuf)
            cast = res[3:]
            if ride_proj and l == 0:
                w_kv_b = cast[2][None]
            if ride_proj and l == 1:
                w_q_b, w_o_b = cast[2].reshape(w_q.shape), cast[3].reshape(w_o.shape)
        else:
            lb = l - n_a
            if l == n_a:
                tails = tuple(min(w, T) for (w, _) in BRANCHES)
                res = _proj(xp, g_kv, 0, w_kv_b, 0, 2 * D, 512, tails=tails)
                kvs = res[:n_br]
                kv_p = [t.reshape(B, rows, 2, HEADS, HEAD_DIM) for t, rows in zip(res[n_br:], tails)]
                kvn = _proj_sample(xs, g_kv, 0, w_kv_b, 0)
            qs_p = _proj(xp, g_mix, l, w_q_b, lb, D, 1024, scale=qk_scale * LOG2E)
            qs = _proj_sample(xs, g_mix, l, w_q_b, lb, scale=qk_scale * LOG2E)
            os_, lses = [], []
            for g, (w, d) in enumerate(BRANCHES):
                L = T // d
                tq, rb = (L, min(d, ATTN_ROWS // L)) if L < ATTN_ROWS else (ATTN_ROWS, 1)
                o_g, ml_g = _attn_branch(qs_p[g], kvs[g], blk_bias[g], tq, rb)
                os_.append(o_g)
                lses.append(ml_g)
            o_s = _attn_sample(qs.reshape(Bs, n_br, HEADS, HEAD_DIM),
                               kvn.reshape(Bs, n_br, 2, HEADS, HEAD_DIM), caches, smp_bias)
            xs1 = _out_sample(xs, o_s.reshape(Bs, D), w_o_b[lb])
            res = _merge_mlp(xp, xs1, os_, lses, w_o_b, lb, g_mlp, w1_b, w2_b, row(norm_final),
                             l, last, riders)
            xp, xs = res[:2]
            cast = res[2:]
        if not last:
            w1_b, w2_b = cast[:2]

    kv_s = [kvn[:, g * 2 * D:(g + 1) * 2 * D].reshape(Bs, 1, 2, HEADS, HEAD_DIM) for g in range(n_br)]
    return (xp, xs.reshape(Bs, 1, D), jnp.stack(pool_p), jnp.stack(pool_s),
            kv_p[0], kv_s[0], kv_p[1], kv_s[1], kv_p[2], kv_s[2])
```
